```python
import math
import jax, jax.numpy as jnp
from jax import lax
import numpy as np

D_MODEL = 2048
BATCH = 8
SEQ = 4096
DEPTH = 2

HEAD_DIM = 128
A_HEADS = 4
A_VDIM = 2 * HEAD_DIM
B_HEADS = 8
B_BRANCHES = ((128, 1), (512, 4), (2048, 16))
C_Q_HEADS = 16
C_KV_HEADS = 4
C_RADIUS = 128
Q_BLOCK = 128
N_EXPERTS = 16
EXPERT_HIDDEN = 2048
EC_CAPACITY_FACTOR = 2

N_EVEN = (DEPTH + 1) // 2
N_ODD = DEPTH // 2
DEEPNORM_ALPHA = (2.0 * DEPTH) ** 0.25
DEEPNORM_BETA = (8.0 * DEPTH) ** -0.25
LN_EPS = 1e-5
NEG = -1e30

A_QK = A_HEADS * 2 * HEAD_DIM
A_V = A_HEADS * A_VDIM
B_W = B_HEADS * HEAD_DIM
AB_IN = 2 * A_QK + A_V + 3 * B_W
AB_OUT = A_V + B_W
C_QW = C_Q_HEADS * HEAD_DIM
C_KVW = C_KV_HEADS * HEAD_DIM
C_IN = C_QW + 2 * C_KVW
C_OUT = C_QW

kernel_name = "hybrid_diff_dilated_swa_ec_moe_encoder"


def alibi_slopes(n):
    return jnp.asarray(np.array([2.0 ** (-8.0 * (i + 1) / n) for i in range(n)], dtype=np.float32))


def layer_norm(x, g, b):
    xf = x.astype(jnp.float32)
    mu = xf.mean(-1, keepdims=True)
    var = jnp.square(xf - mu).mean(-1, keepdims=True)
    return ((xf - mu) * lax.rsqrt(var + LN_EPS) * g + b).astype(x.dtype)


def rms_norm(x, g):
    xf = x.astype(jnp.float32)
    return (xf * lax.rsqrt(jnp.square(xf).mean(-1, keepdims=True) + LN_EPS) * g).astype(x.dtype)


def banded_attention(q, k, v, slopes, radius, dist_scale, sink=None):
    bsz, n, hkv, g, dh = q.shape
    blk = radius
    nb = -(-n // blk)
    n_pad = nb * blk
    qb = jnp.pad(q, ((0, 0), (0, n_pad - n), (0, 0), (0, 0), (0, 0))).reshape(bsz, nb, blk, hkv, g, dh)
    pad_kv = ((0, 0), (blk, n_pad - n + blk), (0, 0), (0, 0))
    kp = jnp.pad(k, pad_kv).reshape(bsz, nb + 2, blk, hkv, dh)
    vp = jnp.pad(v, pad_kv).reshape(bsz, nb + 2, blk, hkv, dh)

    def windows(t):
        return jnp.concatenate([t[:, :-2], t[:, 1:-1], t[:, 2:]], axis=2)

    kw, vw = windows(kp), windows(vp)
    rel = jnp.arange(3 * blk)[None, :] - blk - jnp.arange(blk)[:, None]
    kpos = (jnp.arange(nb)[:, None] - 1) * blk + jnp.arange(3 * blk)[None, :]
    mask = (jnp.abs(rel) <= radius)[None] & ((kpos >= 0) & (kpos < n))[:, None, :]
    dist = (dist_scale * jnp.abs(rel)).astype(jnp.float32)
    s = jnp.einsum('bnihgd,bnjhd->bnhgij', qb, kw).astype(jnp.float32) * (dh ** -0.5)
    s = s - slopes.astype(jnp.float32)[:, :, None, None] * dist
    s = jnp.where(mask[None, :, None, None], s, NEG)
    m = s.max(-1)
    if sink is not None:
        sk = sink.astype(jnp.float32)[None, None, :, :, None]
        m = jnp.maximum(m, sk)
    p = jnp.exp(s - m[..., None])
    denom = p.sum(-1)
    if sink is not None:
        denom = denom + jnp.exp(sk - m)
    out = jnp.einsum('bnhgij,bnjhd->bnihgd', (p / denom[..., None]).astype(v.dtype), vw)
    lse = m + jnp.log(denom)
    out = out.reshape(bsz, n_pad, hkv, g, dh)[:, :n]
    lse = jnp.moveaxis(lse, -1, 2).reshape(bsz, n_pad, hkv, g)[:, :n]
    return out, lse


def dilated_mixture(q, k, v, slopes):
    bsz, s, h, dh = q.shape
    outs, lses = [], []
    for window, dil in B_BRANCHES:
        ns = s // dil

        def fold(t):
            return t.reshape(bsz, ns, dil, h, dh).transpose(0, 2, 1, 3, 4).reshape(bsz * dil, ns, h, dh)

        o, lse = banded_attention(fold(q)[:, :, :, None], fold(k), fold(v), slopes[:, None],
                                  window // (2 * dil), dil)
        outs.append(o[:, :, :, 0].reshape(bsz, dil, ns, h, dh).transpose(0, 2, 1, 3, 4).reshape(bsz, s, h, dh))
        lses.append(lse[..., 0].reshape(bsz, dil, ns, h).transpose(0, 2, 1, 3).reshape(bsz, s, h))
    w = jax.nn.softmax(jnp.stack(lses), axis=0)
    return jnp.einsum('rbsh,rbshd->bshd', w.astype(q.dtype), jnp.stack(outs))


def diff_attention(q, k, v, lam, slopes):
    bsz, s, h, _, dh = q.shape
    nblk = s // Q_BLOCK
    qb = q.reshape(bsz, nblk, Q_BLOCK, h, 2, dh).transpose(1, 0, 2, 3, 4, 5)
    kpos = jnp.arange(s)
    sl = slopes[None, :, None, None, None]

    def block(args):
        qi, start = args
        sc = jnp.einsum('bihmd,bjhmd->bhmij', qi, k).astype(jnp.float32) * (dh ** -0.5)
        qpos = start + jnp.arange(Q_BLOCK)
        dist = jnp.abs(qpos[:, None] - kpos[None, :]).astype(jnp.float32)
        p = jax.nn.softmax(sc - sl * dist, axis=-1)
        wdiff = p[:, :, 0] - lam * p[:, :, 1]
        return jnp.einsum('bhij,bjhe->bihe', wdiff.astype(v.dtype), v)

    out = lax.map(block, (qb, jnp.arange(nblk) * Q_BLOCK))
    return out.transpose(1, 0, 2, 3, 4).reshape(bsz, s, h, 2 * dh)


def mixer_ab(h, w_in, w_out, lam_vecs, subln_g, layer_idx):
    bsz, s, _ = h.shape
    proj = h @ w_in
    cuts = [A_QK, 2 * A_QK, 2 * A_QK + A_V, 2 * A_QK + A_V + B_W, 2 * A_QK + A_V + 2 * B_W]
    aq, ak, av, bq, bk, bv = jnp.split(proj, cuts, axis=-1)
    lam_init = 0.8 - 0.6 * math.exp(-0.3 * layer_idx)
    lv = lam_vecs.astype(jnp.float32)
    lam = jnp.exp(jnp.sum(lv[0] * lv[1])) - jnp.exp(jnp.sum(lv[2] * lv[3])) + lam_init
    ya = diff_attention(aq.reshape(bsz, s, A_HEADS, 2, HEAD_DIM), ak.reshape(bsz, s, A_HEADS, 2, HEAD_DIM),
                        av.reshape(bsz, s, A_HEADS, A_VDIM), lam, alibi_slopes(A_HEADS))
    ya = rms_norm(ya, subln_g) * (1.0 - lam_init)
    yb = dilated_mixture(bq.reshape(bsz, s, B_HEADS, HEAD_DIM), bk.reshape(bsz, s, B_HEADS, HEAD_DIM),
                         bv.reshape(bsz, s, B_HEADS, HEAD_DIM), alibi_slopes(B_HEADS))
    y = jnp.concatenate([ya.reshape(bsz, s, A_V), yb.reshape(bsz, s, B_W)], axis=-1)
    return y @ w_out


def mixer_c(h, w_in, w_out, sink):
    bsz, s, _ = h.shape
    g = C_Q_HEADS // C_KV_HEADS
    q, k, v = jnp.split(h @ w_in, [C_QW, C_QW + C_KVW], axis=-1)
    o, _ = banded_attention(q.reshape(bsz, s, C_KV_HEADS, g, HEAD_DIM),
                            k.reshape(bsz, s, C_KV_HEADS, HEAD_DIM), v.reshape(bsz, s, C_KV_HEADS, HEAD_DIM),
                            alibi_slopes(C_Q_HEADS).reshape(C_KV_HEADS, g), C_RADIUS, 1,
                            sink.reshape(C_KV_HEADS, g))
    return o.reshape(bsz, s, C_OUT) @ w_out


def expert_choice_ffn(h, w_router, w_gate, w_up, w_down):
    bsz, s, d = h.shape
    cap = EC_CAPACITY_FACTOR * s // N_EXPERTS
    aff = jax.nn.softmax((h @ w_router).astype(jnp.float32), axis=-1)
    gsel, idx = lax.top_k(jnp.swapaxes(aff, 1, 2), cap)
    xg = jax.vmap(lambda hb, ib: hb[ib])(h, idx)
    hid = jax.nn.silu(jnp.einsum('becd,edf->becf', xg, w_gate)) * jnp.einsum('becd,edf->becf', xg, w_up)
    y = jnp.einsum('becf,efd->becd', hid, w_down) * gsel[..., None].astype(h.dtype)
    return jax.vmap(lambda ib, yb: jnp.zeros((s, d), yb.dtype).at[ib.reshape(-1)].add(yb.reshape(-1, d)))(idx, y)


def setup_inputs(seed: int = 0) -> dict:
    key = jax.random.key(seed)
    ks = jax.random.split(key, 17)
    f32 = jnp.float32

    def nrm(k, shape, scale):
        return jax.random.normal(k, shape, f32) * scale

    beta = DEEPNORM_BETA
    ab_col = jnp.asarray(np.concatenate([np.ones(2 * A_QK), np.full(A_V, beta), np.ones(2 * B_W),
                                         np.full(B_W, beta)]).astype(np.float32))
    c_col = jnp.asarray(np.concatenate([np.ones(C_QW + C_KVW), np.full(C_KVW, beta)]).astype(np.float32))
    return {
        "x": nrm(ks[0], (BATCH, SEQ, D_MODEL), 1.0),
        "c": nrm(ks[1], (BATCH, D_MODEL), 1.0),
        "ada_w": nrm(ks[2], (DEPTH, D_MODEL, 6 * D_MODEL), 0.5 * D_MODEL ** -0.5),
        "ada_b": nrm(ks[3], (DEPTH, 6 * D_MODEL), 0.02),
        "ln_g": 1.0 + nrm(ks[4], (DEPTH, 2, D_MODEL), 0.02),
        "ln_b": nrm(ks[5], (DEPTH, 2, D_MODEL), 0.02),
        "ab_w_in": nrm(ks[6], (N_EVEN, D_MODEL, AB_IN), D_MODEL ** -0.5) * ab_col,
        "ab_w_out": nrm(ks[7], (N_EVEN, AB_OUT, D_MODEL), AB_OUT ** -0.5 * beta),
        "diff_lambda": nrm(ks[8], (N_EVEN, 4, HEAD_DIM), 0.1),
        "diff_subln_g": 1.0 + nrm(ks[9], (N_EVEN, A_VDIM), 0.02),
        "c_w_in": nrm(ks[10], (N_ODD, D_MODEL, C_IN), D_MODEL ** -0.5) * c_col,
        "c_w_out": nrm(ks[11], (N_ODD, C_OUT, D_MODEL), C_OUT ** -0.5 * beta),
        "c_sink": nrm(ks[12], (N_ODD, C_Q_HEADS), 0.5),
        "router_w": nrm(ks[13], (DEPTH, D_MODEL, N_EXPERTS), D_MODEL ** -0.5),
        "w_gate": nrm(ks[14], (DEPTH, N_EXPERTS, D_MODEL, EXPERT_HIDDEN), D_MODEL ** -0.5),
        "w_up": nrm(ks[15], (DEPTH, N_EXPERTS, D_MODEL, EXPERT_HIDDEN), D_MODEL ** -0.5),
        "w_down": nrm(ks[16], (DEPTH, N_EXPERTS, EXPERT_HIDDEN, D_MODEL), EXPERT_HIDDEN ** -0.5 * beta),
    }


def reference(x, c, ada_w, ada_b, ln_g, ln_b, ab_w_in, ab_w_out, diff_lambda, diff_subln_g,
              c_w_in, c_w_out, c_sink, router_w, w_gate, w_up, w_down):
    cs = jax.nn.silu(c)
    for l in range(DEPTH):
        mod = (cs @ ada_w[l] + ada_b[l])[:, None, :]
        sh1, sc1, g1, sh2, sc2, g2 = jnp.split(mod, 6, axis=-1)
        h = x * (1.0 + sc1) + sh1
        i = l // 2
        if l % 2 == 0:
            y = mixer_ab(h, ab_w_in[i], ab_w_out[i], diff_lambda[i], diff_subln_g[i], l)
        else:
            y = mixer_c(h, c_w_in[i], c_w_out[i], c_sink[i])
        x = layer_norm(DEEPNORM_ALPHA * x + g1 * y, ln_g[l, 0], ln_b[l, 0])
        h = x * (1.0 + sc2) + sh2
        y = expert_choice_ffn(h, router_w[l], w_gate[l], w_up[l], w_down[l])
        x = layer_norm(DEEPNORM_ALPHA * x + g2 * y, ln_g[l, 1], ln_b[l, 1])
    return x
```

```python
import functools
import math

import numpy as np
import jax
import jax.numpy as jnp
from jax import lax
from jax.experimental import pallas as pl
from jax.experimental.pallas import tpu as pltpu

F32 = jnp.float32
BF16 = jnp.bfloat16
I32 = jnp.int32
U32 = jnp.uint32

HEAD_DIM = 128
A_HEADS = 4
A_VDIM = 2 * HEAD_DIM
B_HEADS = 8
B_BRANCHES = ((128, 1), (512, 4), (2048, 16))
C_Q_HEADS = 16
C_KV_HEADS = 4
C_RADIUS = 128
EC_CAPACITY_FACTOR = 2
LN_EPS = 1e-5
NEG = -1e30
LOG2E = 1.4426950408889634

A_QK = A_HEADS * 2 * HEAD_DIM
A_V = A_HEADS * A_VDIM
B_W = B_HEADS * HEAD_DIM
AB_IN = 2 * A_QK + A_V + 3 * B_W
C_QW = C_Q_HEADS * HEAD_DIM
C_KVW = C_KV_HEADS * HEAD_DIM
C_IN = C_QW + 2 * C_KVW

VMEM_LIMIT_BYTES = 56 * 1024 * 1024


def _params(*sem):
    return pltpu.CompilerParams(dimension_semantics=sem, vmem_limit_bytes=VMEM_LIMIT_BYTES)


def _tile(n, preferred):
    t = min(n, preferred)
    while n % t:
        t //= 2
    return t


def _alibi_slopes(n):
    return np.array([2.0 ** (-8.0 * (i + 1) / n) for i in range(n)], dtype=np.float32)


def _nt_dot(a, b):
    return lax.dot_general(a, b, (((1,), (1,)), ((), ())), preferred_element_type=F32)


def _mod_kernel(c_ref, w_ref, b_ref, o_ref):
    c = c_ref[...]
    cs = (c * jax.nn.sigmoid(c)).astype(BF16)
    o_ref[0] = jnp.dot(cs, w_ref[0].astype(BF16), preferred_element_type=F32) + b_ref[0]


def _modulation(c, ada_w, ada_b):
    L, D, N = ada_w.shape
    B = c.shape[0]
    tn = _tile(N, 1024)
    return pl.pallas_call(
        _mod_kernel,
        grid=(L, N // tn),
        in_specs=[
            pl.BlockSpec((B, D), lambda l, j: (0, 0)),
            pl.BlockSpec((1, D, tn), lambda l, j: (l, 0, j)),
            pl.BlockSpec((1, 1, tn), lambda l, j: (l, 0, j)),
        ],
        out_specs=pl.BlockSpec((1, B, tn), lambda l, j: (l, 0, j)),
        out_shape=jax.ShapeDtypeStruct((L, B, N), F32),
        compiler_params=_params("parallel", "parallel"),
        name="adaln_mod",
    )(c, ada_w, ada_b.reshape(L, 1, N))


def _inproj_kernel(x_ref, sh_ref, sc_ref, w_ref, cs_ref, o_ref, h_ref):
    @pl.when(pl.program_id(2) == 0)
    def _():
        h_ref[...] = (x_ref[0] * (1.0 + sc_ref[0]) + sh_ref[0]).astype(BF16)

    acc = jnp.dot(h_ref[...], w_ref[...], preferred_element_type=F32)
    o_ref[0] = (acc * cs_ref[...]).astype(BF16)


def _in_projection(x, mod, w_bf16, colscale):
    B, S, D = x.shape
    N = w_bf16.shape[1]
    tm = min(S, 1024)
    tn = _tile(N, 1024)
    return pl.pallas_call(
        _inproj_kernel,
        grid=(B, S // tm, N // tn),
        in_specs=[
            pl.BlockSpec((1, tm, D), lambda b, i, j: (b, i, 0)),
            pl.BlockSpec((1, 1, D), lambda b, i, j: (b, 0, 0)),
            pl.BlockSpec((1, 1, D), lambda b, i, j: (b, 0, 1)),
            pl.BlockSpec((D, tn), lambda b, i, j: (0, j)),
            pl.BlockSpec((1, tn), lambda b, i, j: (0, j)),
        ],
        out_specs=pl.BlockSpec((1, tm, tn), lambda b, i, j: (b, i, j)),
        out_shape=jax.ShapeDtypeStruct((B, S, N), BF16),
        scratch_shapes=[pltpu.VMEM((tm, D), BF16)],
        compiler_params=_params("parallel", "parallel", "arbitrary"),
        name="in_proj",
    )(x, mod, mod, w_bf16, colscale)


def _diff_kernel(slope_ref, q_ref, k_ref, v_ref, lam_ref, g_ref, o_ref,
                 m1_ref, l1_ref, a1_ref, m2_ref, l2_ref, a2_ref, *, tq, tk, n_chunks, lam_init):
    h = pl.program_id(1)
    i = pl.program_id(2)
    slope2 = slope_ref[h]
    q = q_ref[0]
    q1 = q[:, :HEAD_DIM]
    q2 = q[:, HEAD_DIM:]
    relf = (lax.broadcasted_iota(I32, (tq, tk), 1) - lax.broadcasted_iota(I32, (tq, tk), 0)).astype(F32)

    m1_ref[...] = jnp.full(m1_ref.shape, NEG, F32)
    m2_ref[...] = jnp.full(m2_ref.shape, NEG, F32)
    l1_ref[...] = jnp.zeros(l1_ref.shape, F32)
    l2_ref[...] = jnp.zeros(l2_ref.shape, F32)
    a1_ref[...] = jnp.zeros(a1_ref.shape, F32)
    a2_ref[...] = jnp.zeros(a2_ref.shape, F32)

    def chunk(c, carry):
        k0 = pl.multiple_of(c * tk, tk)
        kc = k_ref[0, pl.ds(k0, tk), :]
        vc = v_ref[0, pl.ds(k0, tk), :]
        bias = slope2 * jnp.abs(relf + (k0 - i * tq).astype(F32))

        def one(qm, km, m_ref, l_ref, a_ref):
            s = _nt_dot(qm, km) - bias
            m_old = m_ref[...]
            m_new = jnp.maximum(m_old, jnp.max(s, axis=-1, keepdims=True))
            p = jnp.exp2(s - m_new)
            alpha = jnp.exp2(m_old - m_new)
            l_ref[...] = alpha * l_ref[...] + jnp.sum(p, axis=-1, keepdims=True)
            a_ref[...] = alpha * a_ref[...] + jnp.dot(p.astype(BF16), vc, preferred_element_type=F32)
            m_ref[...] = m_new

        one(q1, kc[:, :HEAD_DIM], m1_ref, l1_ref, a1_ref)
        one(q2, kc[:, HEAD_DIM:], m2_ref, l2_ref, a2_ref)
        return carry

    lax.fori_loop(0, n_chunks, chunk, 0)

    lv = lam_ref[...]
    s01 = jnp.sum(lv[0:1, :] * lv[1:2, :], axis=-1, keepdims=True)
    s23 = jnp.sum(lv[2:3, :] * lv[3:4, :], axis=-1, keepdims=True)
    lam = jnp.exp(s01) - jnp.exp(s23) + lam_init
    o = a1_ref[...] / l1_ref[...] - lam * (a2_ref[...] / l2_ref[...])
    ms = jnp.mean(o * o, axis=-1, keepdims=True)
    o = o * lax.rsqrt(ms + LN_EPS) * g_ref[...] * (1.0 - lam_init)
    o_ref[0] = o.astype(BF16)


def _diff_attention(proj, lam_vecs, subln_g, layer_idx):
    B, S, _ = proj.shape
    tq = min(S, 512)
    tk = min(S, 1024)
    lam_init = 0.8 - 0.6 * math.exp(-0.3 * layer_idx)
    slopes2 = jnp.asarray(_alibi_slopes(A_HEADS) * LOG2E)
    nq = A_QK // A_VDIM
    kern = functools.partial(_diff_kernel, tq=tq, tk=tk, n_chunks=S // tk, lam_init=lam_init)
    return pl.pallas_call(
        kern,
        grid=(B, A_HEADS, S // tq),
        in_specs=[
            pl.BlockSpec(memory_space=pltpu.SMEM),
            pl.BlockSpec((1, tq, A_VDIM), lambda b, h, i: (b, i, h)),
            pl.BlockSpec((1, S, A_VDIM), lambda b, h, i: (b, 0, nq + h)),
            pl.BlockSpec((1, S, A_VDIM), lambda b, h, i: (b, 0, 2 * nq + h)),
            pl.BlockSpec((4, HEAD_DIM), lambda b, h, i: (0, 0)),
            pl.BlockSpec((1, A_VDIM), lambda b, h, i: (0, 0)),
        ],
        out_specs=pl.BlockSpec((1, tq, A_VDIM), lambda b, h, i: (b, i, h)),
        out_shape=jax.ShapeDtypeStruct((B, S, A_V), BF16),
        scratch_shapes=[
            pltpu.VMEM((tq, 1), F32), pltpu.VMEM((tq, 1), F32), pltpu.VMEM((tq, A_VDIM), F32),
            pltpu.VMEM((tq, 1), F32), pltpu.VMEM((tq, 1), F32), pltpu.VMEM((tq, A_VDIM), F32),
        ],
        compiler_params=_params("parallel", "parallel", "parallel"),
        name="diff_attn",
    )(slopes2, proj, proj, proj, lam_vecs, subln_g.reshape(1, A_VDIM))


def _banded_kernel(slope_ref, sink_ref, q_ref, kp_ref, kc_ref, kn_ref, vp_ref, vc_ref, vn_ref, *rest,
                   tq, radius, n_q, group, n_rows, dist_scale, use_sink, want_lse):
    if want_lse:
        o_ref, lse_ref, kwin, vwin = rest
    else:
        o_ref, kwin, vwin = rest
        lse_ref = None
    t = pl.program_id(2)
    w = tq + 2 * radius
    kwin[0:radius, :] = kp_ref[0]
    kwin[radius:radius + tq, :] = kc_ref[0]
    kwin[radius + tq:w, :] = kn_ref[0]
    vwin[0:radius, :] = vp_ref[0]
    vwin[radius:radius + tq, :] = vc_ref[0]
    vwin[radius + tq:w, :] = vn_ref[0]

    ii = lax.broadcasted_iota(I32, (tq, w), 0)
    jj = lax.broadcasted_iota(I32, (tq, w), 1)
    rel = jnp.abs(jj - radius - ii)
    kpos = t * tq - radius + jj
    valid = (rel <= radius) & (kpos >= 0) & (kpos < n_rows)
    dist = rel.astype(F32) * float(dist_scale)
    lane = lax.broadcasted_iota(I32, (tq, HEAD_DIM), 1)
    lse_tile = jnp.zeros((tq, HEAD_DIM), F32)

    for h in range(n_q):
        hk = h // group
        qh = q_ref[0, :, h * HEAD_DIM:(h + 1) * HEAD_DIM]
        kh = kwin[:, hk * HEAD_DIM:(hk + 1) * HEAD_DIM]
        vh = vwin[:, hk * HEAD_DIM:(hk + 1) * HEAD_DIM]
        s = jnp.where(valid, _nt_dot(qh, kh) - slope_ref[h] * dist, NEG)
        m = jnp.max(s, axis=-1, keepdims=True)
        if use_sink:
            m = jnp.maximum(m, sink_ref[h])
        p = jnp.exp2(s - m)
        den = jnp.sum(p, axis=-1, keepdims=True)
        if use_sink:
            den = den + jnp.exp2(sink_ref[h] - m)
        o = jnp.dot(p.astype(BF16), vh, preferred_element_type=F32) / den
        o_ref[0, :, h * HEAD_DIM:(h + 1) * HEAD_DIM] = o.astype(BF16)
        if want_lse:
            lse_tile = jnp.where(lane == h, m + jnp.log2(den), lse_tile)
    if want_lse:
        lse_ref[0] = lse_tile


def _banded_attention(src, *, n_rows, n_res, src_cols, q_blk, k_blk, v_blk, n_q, group, radius,
                      dist_scale, slopes2, sink2, want_lse):
    B = src.shape[0]
    qw = n_q * HEAD_DIM
    kw = (n_q // group) * HEAD_DIM
    tq = min(n_rows, 256)
    nt = n_rows // tq
    per_t = tq // radius
    last_halo = n_rows // radius - 1
    qpg = src_cols // qw
    kpg = src_cols // kw

    def q_map(b, r, t):
        return (b, t, r * qpg + q_blk)

    def cur_map(blk):
        return lambda b, r, t: (b, t, r * kpg + blk)

    def prev_map(blk):
        return lambda b, r, t: (b, jnp.maximum(t * per_t - 1, 0), r * kpg + blk)

    def next_map(blk):
        return lambda b, r, t: (b, jnp.minimum((t + 1) * per_t, last_halo), r * kpg + blk)

    out_shape = [jax.ShapeDtypeStruct((B, n_rows, n_res * qw), BF16)]
    out_specs = [pl.BlockSpec((1, tq, qw), lambda b, r, t: (b, t, r))]
    if want_lse:
        out_shape.append(jax.ShapeDtypeStruct((B, n_rows, n_res * HEAD_DIM), F32))
        out_specs.append(pl.BlockSpec((1, tq, HEAD_DIM), lambda b, r, t: (b, t, r)))
    kern = functools.partial(_banded_kernel, tq=tq, radius=radius, n_q=n_q, group=group, n_rows=n_rows,
                             dist_scale=dist_scale, use_sink=sink2 is not None, want_lse=want_lse)
    if sink2 is None:
        sink2 = jnp.zeros((n_q,), F32)
    return pl.pallas_call(
        kern,
        grid=(B, n_res, nt),
        in_specs=[
            pl.BlockSpec(memory_space=pltpu.SMEM),
            pl.BlockSpec(memory_space=pltpu.SMEM),
            pl.BlockSpec((1, tq, qw), q_map),
            pl.BlockSpec((1, radius, kw), prev_map(k_blk)),
            pl.BlockSpec((1, tq, kw), cur_map(k_blk)),
            pl.BlockSpec((1, radius, kw), next_map(k_blk)),
            pl.BlockSpec((1, radius, kw), prev_map(v_blk)),
            pl.BlockSpec((1, tq, kw), cur_map(v_blk)),
            pl.BlockSpec((1, radius, kw), next_map(v_blk)),
        ],
        out_specs=out_specs,
        out_shape=out_shape,
        scratch_shapes=[pltpu.VMEM((tq + 2 * radius, kw), BF16), pltpu.VMEM((tq + 2 * radius, kw), BF16)],
        compiler_params=_params("parallel", "parallel", "parallel"),
        name="banded_attn",
    )(slopes2, sink2, src, src, src, src, src, src, src)


def _post_attention(y, x, g1, lng, lnb, sc2, sh2, rwh_ref, rwl_ref, x1_ref, hp_ref, lg_ref, alpha):
    z = alpha * x + g1 * y
    mu = jnp.mean(z, axis=-1, keepdims=True)
    zc = z - mu
    var = jnp.mean(zc * zc, axis=-1, keepdims=True)
    x1 = zc * lax.rsqrt(var + LN_EPS) * lng + lnb
    x1_ref[0] = x1
    h2 = x1 * (1.0 + sc2) + sh2
    hb = h2.astype(BF16)
    hf = hb.astype(F32)
    bits = lax.bitcast_convert_type(hf, U32)
    half = bits.shape[1] // 2
    hp_ref[0] = bits[:, :half] | (bits[:, half:] >> 16)
    lo = (h2 - hf).astype(BF16)
    rwh = rwh_ref[...]
    lg_ref[0] = (jnp.dot(hb, rwh, preferred_element_type=F32)
                 + jnp.dot(hb, rwl_ref[...], preferred_element_type=F32)
                 + jnp.dot(lo, rwh, preferred_element_type=F32))


def _outproj_ab_kernel(ya_ref, o1_ref, o2_ref, o3_ref, l1_ref, l2_ref, l3_ref, w_ref, x_ref, g1_ref,
                       lng_ref, lnb_ref, sh2_ref, sc2_ref, rwh_ref, rwl_ref, x1_ref, hp_ref, lg_ref,
                       yb_ref, *, alpha):
    l1 = l1_ref[0]
    l2 = l2_ref[0]
    l3 = l3_ref[0]
    mx = jnp.maximum(jnp.maximum(l1, l2), l3)
    e1 = jnp.exp2(l1 - mx)
    e2 = jnp.exp2(l2 - mx)
    e3 = jnp.exp2(l3 - mx)
    inv = 1.0 / (e1 + e2 + e3)
    w1 = e1 * inv
    w2 = e2 * inv
    w3 = e3 * inv
    for h in range(B_HEADS):
        sl = slice(h * HEAD_DIM, (h + 1) * HEAD_DIM)
        yb = (w1[:, h:h + 1] * o1_ref[0, :, sl].astype(F32)
              + w2[:, h:h + 1] * o2_ref[0, :, sl].astype(F32)
              + w3[:, h:h + 1] * o3_ref[0, :, sl].astype(F32))
        yb_ref[:, sl] = yb.astype(BF16)
    y = (jnp.dot(ya_ref[0], w_ref[0:A_V, :], preferred_element_type=F32)
         + jnp.dot(yb_ref[...], w_ref[A_V:A_V + B_W, :], preferred_element_type=F32))
    _post_attention(y, x_ref[0], g1_ref[0], lng_ref[...], lnb_ref[...], sc2_ref[0], sh2_ref[0],
                    rwh_ref, rwl_ref, x1_ref, hp_ref, lg_ref, alpha)


def _outproj_c_kernel(o_ref, w_ref, x_ref, g1_ref, lng_ref, lnb_ref, sh2_ref, sc2_ref, rwh_ref, rwl_ref,
                      x1_ref, hp_ref, lg_ref, *, alpha):
    y = jnp.dot(o_ref[0], w_ref[...], preferred_element_type=F32)
    _post_attention(y, x_ref[0], g1_ref[0], lng_ref[...], lnb_ref[...], sc2_ref[0], sh2_ref[0],
                    rwh_ref, rwl_ref, x1_ref, hp_ref, lg_ref, alpha)


def _out_projection(attn_inputs, w_bf16, x, mod, lng, lnb, rw_hi, rw_lo, alpha, mixer):
    B, S, D = x.shape
    E = rw_hi.shape[1]
    tm = min(S, 512)
    row = lambda b, i: (b, i, 0)
    const2 = lambda b, i: (0, 0)
    tail_specs = [
        pl.BlockSpec(w_bf16.shape, const2),
        pl.BlockSpec((1, tm, D), row),
        pl.BlockSpec((1, 1, D), lambda b, i: (b, 0, 2)),
        pl.BlockSpec((1, D), const2),
        pl.BlockSpec((1, D), const2),
        pl.BlockSpec((1, 1, D), lambda b, i: (b, 0, 3)),
        pl.BlockSpec((1, 1, D), lambda b, i: (b, 0, 4)),
        pl.BlockSpec((D, E), const2),
        pl.BlockSpec((D, E), const2),
    ]
    tail_args = (w_bf16, x, mod, lng.reshape(1, D), lnb.reshape(1, D), mod, mod, rw_hi, rw_lo)
    out_shape = [jax.ShapeDtypeStruct((B, S, D), F32), jax.ShapeDtypeStruct((B, S, D // 2), U32),
                 jax.ShapeDtypeStruct((B, S, E), F32)]
    out_specs = [pl.BlockSpec((1, tm, D), row), pl.BlockSpec((1, tm, D // 2), row), pl.BlockSpec((1, tm, E), row)]
    if mixer == "ab":
        ya, o1, o2, o3, l1, l2, l3 = attn_inputs
        in_specs = ([pl.BlockSpec((1, tm, A_V), row)] + [pl.BlockSpec((1, tm, B_W), row)] * 3
                    + [pl.BlockSpec((1, tm, HEAD_DIM), row)] * 3 + tail_specs)
        kern = functools.partial(_outproj_ab_kernel, alpha=alpha)
        scratch = [pltpu.VMEM((tm, B_W), BF16)]
        args = (ya, o1, o2, o3, l1, l2, l3) + tail_args
    else:
        (o,) = attn_inputs
        in_specs = [pl.BlockSpec((1, tm, o.shape[2]), row)] + tail_specs
        kern = functools.partial(_outproj_c_kernel, alpha=alpha)
        scratch = []
        args = (o,) + tail_args
    return pl.pallas_call(
        kern,
        grid=(B, S // tm),
        in_specs=in_specs,
        out_specs=out_specs,
        out_shape=out_shape,
        scratch_shapes=scratch,
        compiler_params=_params("parallel", "parallel"),
        name="out_proj_" + mixer,
    )(*args)


def _excl_cumsum_lanes(x, tri):
    n = x.shape[1] // HEAD_DIM
    carry = jnp.zeros((x.shape[0], 1), F32)
    outs = []
    for c in range(n):
        xc = x[:, c * HEAD_DIM:(c + 1) * HEAD_DIM]
        outs.append(jnp.dot(xc.astype(BF16), tri, preferred_element_type=F32) + carry)
        carry = carry + jnp.sum(xc, axis=1, keepdims=True)
    return jnp.concatenate(outs, axis=1)


def _tables_kernel(lg_ref, cnt_ref, dst_ref, gk_ref, *, n_exp, cap):
    lg = lg_ref[0]
    S = lg.shape[1]
    ex = jnp.exp(lg - jnp.max(lg, axis=0, keepdims=True))
    aff = ex / jnp.sum(ex, axis=0, keepdims=True)
    bits = lax.bitcast_convert_type(aff, I32)

    def search(i, cur):
        cand = cur | jnp.left_shift(jnp.int32(1), 30 - i)
        cnt = jnp.sum(jnp.where(bits >= cand, 1.0, 0.0), axis=1, keepdims=True)
        return jnp.where(cnt >= cap, cand, cur)

    thr = lax.fori_loop(0, 31, search, jnp.zeros((n_exp, 1), I32))
    tri = jnp.where(lax.broadcasted_iota(I32, (HEAD_DIM, HEAD_DIM), 0)
                    < lax.broadcasted_iota(I32, (HEAD_DIM, HEAD_DIM), 1), 1.0, 0.0).astype(BF16)
    gt = bits > thr
    eq = jnp.where(bits == thr, 1.0, 0.0)
    need = cap - jnp.sum(jnp.where(gt, 1.0, 0.0), axis=1, keepdims=True)
    sel = jnp.where(gt, 1.0, jnp.where(_excl_cumsum_lanes(eq, tri) < need, eq, 0.0))
    pos = _excl_cumsum_lanes(sel, tri)

    slot = lax.broadcasted_iota(I32, (n_exp, S), 0).astype(F32)
    run = jnp.zeros((1, S), F32)
    dst = jnp.zeros((n_exp, S), F32)
    gk = jnp.zeros((n_exp, S), F32)
    for e in range(n_exp):
        se = sel[e:e + 1, :]
        hit = (slot == run) & (se > 0.0)
        dst = jnp.where(hit, pos[e:e + 1, :] + float(e * cap), dst)
        gk = jnp.where(hit, aff[e:e + 1, :], gk)
        run = run + se
    cnt_ref[0] = run.astype(I32)
    dst_ref[0] = dst.astype(I32)
    gk_ref[0] = gk


def _routing_tables(logits_t, cap):
    B, E, S = logits_t.shape
    kern = functools.partial(_tables_kernel, n_exp=E, cap=cap)
    return pl.pallas_call(
        kern,
        grid=(B,),
        in_specs=[pl.BlockSpec((1, E, S), lambda b: (b, 0, 0))],
        out_specs=[pl.BlockSpec((1, 1, S), lambda b: (b, 0, 0)), pl.BlockSpec((1, E, S), lambda b: (b, 0, 0)),
                   pl.BlockSpec((1, E, S), lambda b: (b, 0, 0))],
        out_shape=[jax.ShapeDtypeStruct((B, 1, S), I32), jax.ShapeDtypeStruct((B, E, S), I32),
                   jax.ShapeDtypeStruct((B, E, S), F32)],
        compiler_params=_params("parallel"),
        name="routing_tables",
    )(logits_t)


def _wait_rows(src_ref, dst_ref, sem, n):
    n8 = (n // 8) * 8

    @pl.when(n8 > 0)
    def _():
        rows = pl.ds(0, pl.multiple_of(n8, 8))
        pltpu.make_async_copy(src_ref.at[rows], dst_ref.at[rows], sem).wait()

    def one(_, carry):
        pltpu.make_async_copy(src_ref.at[pl.ds(0, 1)], dst_ref.at[pl.ds(0, 1)], sem).wait()
        return carry

    lax.fori_loop(0, n - n8, one, 0)


def _dispatch_kernel(cnt_ref, dst_ref, gk_ref, hp_ref, xg_ref, grow_ref, sem, *, tc, cap, seq, bcap):
    b = pl.program_id(0)
    tok0 = b * seq + pl.program_id(1) * tc

    def per_token(t, n):
        def per_choice(k, n):
            code = dst_ref[0, k, t]
            e = code // cap
            j = code - e * cap
            row = e * bcap + b * cap + j
            pltpu.make_async_copy(hp_ref.at[pl.ds(tok0 + t, 1)], xg_ref.at[pl.ds(row, 1)], sem).start()
            grow_ref[0, e, j] = gk_ref[0, k, t]
            return n + 1

        return lax.fori_loop(0, cnt_ref[0, 0, t], per_choice, n)

    n = lax.fori_loop(0, tc, per_token, 0)

    _wait_rows(hp_ref, xg_ref, sem, n)


def _dispatch(cnt, dst, gk, hp, n_exp, cap):
    B, _, S = cnt.shape
    T, W = hp.shape
    tc = min(S, 512)
    kern = functools.partial(_dispatch_kernel, tc=tc, cap=cap, seq=S, bcap=B * cap)
    smem = functools.partial(pl.BlockSpec, memory_space=pltpu.SMEM)
    return pl.pallas_call(
        kern,
        grid=(B, S // tc),
        in_specs=[smem((1, 1, tc), lambda b, c: (b, 0, c)), smem((1, n_exp, tc), lambda b, c: (b, 0, c)),
                  smem((1, n_exp, tc), lambda b, c: (b, 0, c)), pl.BlockSpec(memory_space=pl.ANY)],
        out_specs=[pl.BlockSpec(memory_space=pl.ANY), smem((1, n_exp, cap), lambda b, c: (b, 0, 0))],
        out_shape=[jax.ShapeDtypeStruct((n_exp * B * cap, W), hp.dtype),
                   jax.ShapeDtypeStruct((B, n_exp, cap), F32)],
        scratch_shapes=[pltpu.SemaphoreType.DMA],
        compiler_params=_params("arbitrary", "arbitrary"),
        name="moe_dispatch",
    )(cnt, dst, gk, hp)


def _return_kernel(cnt_ref, dst_ref, off_ref, y_ref, yt_ref, sem, *, tc, cap, bcap):
    b = pl.program_id(0)
    base = off_ref[0, 0, 0]

    def per_token(t, n):
        def per_choice(k, n):
            code = dst_ref[0, k, t]
            e = code // cap
            row = e * bcap + b * cap + (code - e * cap)
            pltpu.make_async_copy(y_ref.at[pl.ds(row, 1)], yt_ref.at[pl.ds(base + n, 1)], sem).start()
            return n + 1

        return lax.fori_loop(0, cnt_ref[0, 0, t], per_choice, n)

    n = lax.fori_loop(0, tc, per_token, 0)

    _wait_rows(y_ref, yt_ref, sem, n)


def _return_rows(cnt, dst, off, y, n_exp, cap):
    B, _, S = cnt.shape
    tc = min(S, 512)
    kern = functools.partial(_return_kernel, tc=tc, cap=cap, bcap=B * cap)
    smem = functools.partial(pl.BlockSpec, memory_space=pltpu.SMEM)
    return pl.pallas_call(
        kern,
        grid=(B, S // tc),
        in_specs=[smem((1, 1, tc), lambda b, c: (b, 0, c)), smem((1, n_exp, tc), lambda b, c: (b, 0, c)),
                  smem((1, 1, tc), lambda b, c: (b, 0, c)), pl.BlockSpec(memory_space=pl.ANY)],
        out_specs=pl.BlockSpec(memory_space=pl.ANY),
        out_shape=jax.ShapeDtypeStruct(y.shape, y.dtype),
        scratch_shapes=[pltpu.SemaphoreType.DMA],
        compiler_params=_params("arbitrary", "arbitrary"),
        name="moe_return",
    )(cnt, dst, off, y)


def _ffn_up_kernel(xp_ref, wg_ref, wu_ref, o_ref, xs_ref):
    @pl.when(pl.program_id(2) == 0)
    def _():
        u = xp_ref[...]
        half = u.shape[1]
        xs_ref[:, :half] = lax.bitcast_convert_type(u & jnp.uint32(0xFFFF0000), F32).astype(BF16)
        xs_ref[:, half:] = lax.bitcast_convert_type(u << 16, F32).astype(BF16)

    x = xs_ref[...]
    g = jnp.dot(x, wg_ref[0].astype(BF16), preferred_element_type=F32)
    u = jnp.dot(x, wu_ref[0].astype(BF16), preferred_element_type=F32)
    o_ref[...] = (g * jax.nn.sigmoid(g) * u).astype(BF16)


def _ffn_up(xg, w_gate, w_up, rows_per_expert):
    E, D, Fh = w_gate.shape
    tm = min(rows_per_expert, 2048)
    mt = rows_per_expert // tm
    tn = min(Fh, 256)
    return pl.pallas_call(
        _ffn_up_kernel,
        grid=(E, mt, Fh // tn),
        in_specs=[pl.BlockSpec((tm, D // 2), lambda e, m, n: (e * mt + m, 0)),
                  pl.BlockSpec((1, D, tn), lambda e, m, n: (e, 0, n)),
                  pl.BlockSpec((1, D, tn), lambda e, m, n: (e, 0, n))],
        out_specs=pl.BlockSpec((tm, tn), lambda e, m, n: (e * mt + m, n)),
        out_shape=jax.ShapeDtypeStruct((E * rows_per_expert, Fh), BF16),
        scratch_shapes=[pltpu.VMEM((tm, D), BF16)],
        compiler_params=_params("parallel", "parallel", "arbitrary"),
        name="ffn_up",
    )(xg, w_gate, w_up)


def _ffn_down_kernel(h_ref, wd_ref, g_ref, o_ref):
    o_ref[...] = jnp.dot(h_ref[...], wd_ref[0].astype(BF16), preferred_element_type=F32) * g_ref[...]


def _ffn_down(hid, w_down, gate_col, rows_per_expert):
    E, Fh, D = w_down.shape
    tm = min(rows_per_expert, 2048)
    mt = rows_per_expert // tm
    tn = min(D, 512)
    return pl.pallas_call(
        _ffn_down_kernel,
        grid=(E, mt, D // tn),
        in_specs=[pl.BlockSpec((tm, Fh), lambda e, m, n: (e * mt + m, 0)),
                  pl.BlockSpec((1, Fh, tn), lambda e, m, n: (e, 0, n)),
                  pl.BlockSpec((tm, 1), lambda e, m, n: (e * mt + m, 0))],
        out_specs=pl.BlockSpec((tm, tn), lambda e, m, n: (e * mt + m, n)),
        out_shape=jax.ShapeDtypeStruct((E * rows_per_expert, D), F32),
        compiler_params=_params("parallel", "parallel", "parallel"),
        name="ffn_down",
    )(hid, w_down, gate_col)


def _combine_kernel(offt_ref, x_ref, off_ref, cnt_ref, g2_ref, lng_ref, lnb_ref, yt_ref, o_ref,
                    buf, acc, sem, *, tm, ch, n_total, n_tiles_s, alpha):
    tile = pl.program_id(0) * n_tiles_s + pl.program_id(1)
    o0 = offt_ref[tile]
    o1 = offt_ref[tile + 1]
    step = ch - 8
    n_chunks = (o1 - o0 + step - 1) // step
    lo_col = off_ref[0]
    hi_col = lo_col + cnt_ref[0]
    lane = lax.broadcasted_iota(I32, (tm, ch), 1)
    acc[...] = jnp.zeros(acc.shape, F32)

    def chunk(c, carry):
        want = o0 + c * step
        start = pl.multiple_of((jnp.minimum(want, n_total - ch) // 8) * 8, 8)
        cp = pltpu.make_async_copy(yt_ref.at[pl.ds(start, ch)], buf, sem)
        cp.start()
        cp.wait()
        row = lane + start
        own = (row >= lo_col) & (row < hi_col) & (row >= want) & (row < want + step)
        acc[...] += jnp.dot(jnp.where(own, 1.0, 0.0).astype(BF16), buf[...].astype(BF16),
                            preferred_element_type=F32)
        return carry

    lax.fori_loop(0, n_chunks, chunk, 0)
    z = alpha * x_ref[0] + g2_ref[0] * acc[...]
    mu = jnp.mean(z, axis=-1, keepdims=True)
    zc = z - mu
    var = jnp.mean(zc * zc, axis=-1, keepdims=True)
    o_ref[0] = zc * lax.rsqrt(var + LN_EPS) * lng_ref[...] + lnb_ref[...]


def _combine(x1, mod, lng, lnb, yt, off, cnt, alpha):
    B, S, D = x1.shape
    n_total = yt.shape[0]
    tm = min(S, 256)
    ch = min(n_total, 256)
    n_tiles_s = S // tm
    off_flat = off.reshape(B * S)
    offt = jnp.concatenate([off_flat[::tm], jnp.full((1,), n_total, I32)])
    kern = functools.partial(_combine_kernel, tm=tm, ch=ch, n_total=n_total, n_tiles_s=n_tiles_s, alpha=alpha)
    grid_spec = pltpu.PrefetchScalarGridSpec(
        num_scalar_prefetch=1,
        grid=(B, n_tiles_s),
        in_specs=[
            pl.BlockSpec((1, tm, D), lambda b, i, o: (b, i, 0)),
            pl.BlockSpec((1, tm, 1), lambda b, i, o: (b, i, 0)),
            pl.BlockSpec((1, tm, 1), lambda b, i, o: (b, i, 0)),
            pl.BlockSpec((1, 1, D), lambda b, i, o: (b, 0, 5)),
            pl.BlockSpec((1, D), lambda b, i, o: (0, 0)),
            pl.BlockSpec((1, D), lambda b, i, o: (0, 0)),
            pl.BlockSpec(memory_space=pl.ANY),
        ],
        out_specs=pl.BlockSpec((1, tm, D), lambda b, i, o: (b, i, 0)),
        scratch_shapes=[pltpu.VMEM((ch, D), F32), pltpu.VMEM((tm, D), F32), pltpu.SemaphoreType.DMA],
    )
    return pl.pallas_call(
        kern,
        grid_spec=grid_spec,
        out_shape=jax.ShapeDtypeStruct((B, S, D), F32),
        compiler_params=_params("arbitrary", "arbitrary"),
        name="moe_combine",
    )(offt, x1, off.reshape(B, S, 1), cnt.reshape(B, S, 1), mod, lng.reshape(1, D), lnb.reshape(1, D), yt)


def _moe_sublayer(x1, hp, logits, mod, lng, lnb, w_gate, w_up, w_down, alpha):
    B, S, D = x1.shape
    E = logits.shape[-1]
    cap = EC_CAPACITY_FACTOR * S // E
    cnt, dst, gk = _routing_tables(jnp.swapaxes(logits, 1, 2), cap)
    xg, grow = _dispatch(cnt, dst, gk, hp.reshape(B * S, D // 2), E, cap)
    gate_col = jnp.swapaxes(grow, 0, 1).reshape(E * B * cap, 1)
    hid = _ffn_up(xg, w_gate, w_up, B * cap)
    y = _ffn_down(hid, w_down, gate_col, B * cap)
    cflat = cnt.reshape(B * S)
    off = (jnp.cumsum(cflat) - cflat).astype(I32).reshape(B, 1, S)
    yt = _return_rows(cnt, dst, off, y, E, cap)
    return _combine(x1, mod, lng, lnb, yt, off, cnt, alpha)


def _split_bf16(w):
    hi = w.astype(BF16)
    return hi, (w - hi.astype(F32)).astype(BF16)


def kernel(x, c, ada_w, ada_b, ln_g, ln_b, ab_w_in, ab_w_out, diff_lambda, diff_subln_g, c_w_in, c_w_out,
           c_sink, router_w, w_gate, w_up, w_down):
    B, S, D = x.shape
    depth = ada_w.shape[0]
    alpha = (2.0 * depth) ** 0.25
    qscale = HEAD_DIM ** -0.5 * LOG2E
    mod_all = _modulation(c, ada_w, ada_b)

    ab_scale = np.ones((1, AB_IN), np.float32)
    ab_scale[:, :A_QK] = qscale
    ab_scale[:, 2 * A_QK + A_V:2 * A_QK + A_V + B_W] = qscale
    c_scale = np.ones((1, C_IN), np.float32)
    c_scale[:, :C_QW] = qscale

    for l in range(depth):
        mod = mod_all[l][:, None, :]
        i = l // 2
        rw_hi, rw_lo = _split_bf16(router_w[l])
        if l % 2 == 0:
            proj = _in_projection(x, mod, ab_w_in[i].astype(BF16), jnp.asarray(ab_scale))
            ya = _diff_attention(proj, diff_lambda[i], diff_subln_g[i], l)
            slopes_b = jnp.asarray(_alibi_slopes(B_HEADS) * LOG2E)
            outs = []
            lses = []
            for window, dil in B_BRANCHES:
                n_rows = S // dil
                src = proj.reshape(B, n_rows, dil * AB_IN)
                o, lse = _banded_attention(
                    src, n_rows=n_rows, n_res=dil, src_cols=AB_IN, q_blk=3, k_blk=4, v_blk=5, n_q=B_HEADS,
                    group=1, radius=window // (2 * dil), dist_scale=dil, slopes2=slopes_b, sink2=None,
                    want_lse=True)
                outs.append(o.reshape(B, S, B_W))
                lses.append(lse.reshape(B, S, HEAD_DIM))
            x1, hp, logits = _out_projection((ya, *outs, *lses), ab_w_out[i].astype(BF16), x, mod,
                                             ln_g[l, 0], ln_b[l, 0], rw_hi, rw_lo, alpha, "ab")
        else:
            proj = _in_projection(x, mod, c_w_in[i].astype(BF16), jnp.asarray(c_scale))
            (o,) = _banded_attention(
                proj, n_rows=S, n_res=1, src_cols=C_IN, q_blk=0, k_blk=C_QW // C_KVW, v_blk=C_QW // C_KVW + 1,
                n_q=C_Q_HEADS, group=C_Q_HEADS // C_KV_HEADS, radius=C_RADIUS, dist_scale=1,
                slopes2=jnp.asarray(_alibi_slopes(C_Q_HEADS) * LOG2E), sink2=c_sink[i] * LOG2E, want_lse=False)
            x1, hp, logits = _out_projection((o,), c_w_out[i].astype(BF16), x, mod, ln_g[l, 0], ln_b[l, 0],
                                             rw_hi, rw_lo, alpha, "c")
        x = _moe_sublayer(x1, hp, logits, mod, ln_g[l, 1], ln_b[l, 1], w_gate[l], w_up[l], w_down[l], alpha)
    return x
```

```python
import functools
import math

import numpy as np
import jax
import jax.numpy as jnp
from jax import lax
from jax.experimental import pallas as pl
from jax.experimental.pallas import tpu as pltpu

F32 = jnp.float32
BF16 = jnp.bfloat16
I32 = jnp.int32
U32 = jnp.uint32

HEAD_DIM = 128
A_HEADS = 4
A_VDIM = 2 * HEAD_DIM
B_HEADS = 8
B_BRANCHES = ((128, 1), (512, 4), (2048, 16))
C_Q_HEADS = 16
C_KV_HEADS = 4
C_RADIUS = 128
EC_CAPACITY_FACTOR = 2
LN_EPS = 1e-5
NEG = -1e30
LOG2E = 1.4426950408889634

A_QK = A_HEADS * 2 * HEAD_DIM
A_V = A_HEADS * A_VDIM
B_W = B_HEADS * HEAD_DIM
AB_IN = 2 * A_QK + A_V + 3 * B_W
C_QW = C_Q_HEADS * HEAD_DIM
C_KVW = C_KV_HEADS * HEAD_DIM
C_IN = C_QW + 2 * C_KVW

VMEM_LIMIT_BYTES = 56 * 1024 * 1024


def _params(*sem):
    return pltpu.CompilerParams(dimension_semantics=sem, vmem_limit_bytes=VMEM_LIMIT_BYTES)


def _tile(n, preferred):
    t = min(n, preferred)
    while n % t:
        t //= 2
    return t


def _alibi_slopes(n):
    return np.array([2.0 ** (-8.0 * (i + 1) / n) for i in range(n)], dtype=np.float32)


def _nt_dot(a, b):
    return lax.dot_general(a, b, (((1,), (1,)), ((), ())), preferred_element_type=F32)


def _mod_kernel(c_ref, w_ref, b_ref, o_ref):
    c = c_ref[...]
    cs = (c * jax.nn.sigmoid(c)).astype(BF16)
    o_ref[0] = jnp.dot(cs, w_ref[0].astype(BF16), preferred_element_type=F32) + b_ref[0]


def _modulation(c, ada_w, ada_b):
    L, D, N = ada_w.shape
    B = c.shape[0]
    tn = _tile(N, 1024)
    return pl.pallas_call(
        _mod_kernel,
        grid=(L, N // tn),
        in_specs=[
            pl.BlockSpec((B, D), lambda l, j: (0, 0)),
            pl.BlockSpec((1, D, tn), lambda l, j: (l, 0, j)),
            pl.BlockSpec((1, 1, tn), lambda l, j: (l, 0, j)),
        ],
        out_specs=pl.BlockSpec((1, B, tn), lambda l, j: (l, 0, j)),
        out_shape=jax.ShapeDtypeStruct((L, B, N), F32),
        compiler_params=_params("parallel", "parallel"),
        name="adaln_mod",
    )(c, ada_w, ada_b.reshape(L, 1, N))


def _inproj_kernel(x_ref, sh_ref, sc_ref, w_ref, cs_ref, o_ref, h_ref):
    @pl.when(pl.program_id(2) == 0)
    def _():
        h_ref[...] = (x_ref[0] * (1.0 + sc_ref[0]) + sh_ref[0]).astype(BF16)

    acc = jnp.dot(h_ref[...], w_ref[...], preferred_element_type=F32)
    o_ref[0] = (acc * cs_ref[...]).astype(BF16)


def _in_projection(x, mod, w_bf16, colscale):
    B, S, D = x.shape
    N = w_bf16.shape[1]
    tm = min(S, 1024)
    tn = _tile(N, 1024)
    return pl.pallas_call(
        _inproj_kernel,
        grid=(B, S // tm, N // tn),
        in_specs=[
            pl.BlockSpec((1, tm, D), lambda b, i, j: (b, i, 0)),
            pl.BlockSpec((1, 1, D), lambda b, i, j: (b, 0, 0)),
            pl.BlockSpec((1, 1, D), lambda b, i, j: (b, 0, 1)),
            pl.BlockSpec((D, tn), lambda b, i, j: (0, j)),
            pl.BlockSpec((1, tn), lambda b, i, j: (0, j)),
        ],
        out_specs=pl.BlockSpec((1, tm, tn), lambda b, i, j: (b, i, j)),
        out_shape=jax.ShapeDtypeStruct((B, S, N), BF16),
        scratch_shapes=[pltpu.VMEM((tm, D), BF16)],
        compiler_params=_params("parallel", "parallel", "arbitrary"),
        name="in_proj",
    )(x, mod, mod, w_bf16, colscale)


def _diff_kernel(slope_ref, q_ref, k_ref, v_ref, lam_ref, g_ref, o_ref,
                 m1_ref, l1_ref, a1_ref, m2_ref, l2_ref, a2_ref, *, tq, tk, n_chunks, lam_init):
    h = pl.program_id(1)
    i = pl.program_id(2)
    slope2 = slope_ref[h]
    q = q_ref[0]
    q1 = q[:, :HEAD_DIM]
    q2 = q[:, HEAD_DIM:]
    relf = (lax.broadcasted_iota(I32, (tq, tk), 1) - lax.broadcasted_iota(I32, (tq, tk), 0)).astype(F32)

    m1_ref[...] = jnp.full(m1_ref.shape, NEG, F32)
    m2_ref[...] = jnp.full(m2_ref.shape, NEG, F32)
    l1_ref[...] = jnp.zeros(l1_ref.shape, F32)
    l2_ref[...] = jnp.zeros(l2_ref.shape, F32)
    a1_ref[...] = jnp.zeros(a1_ref.shape, F32)
    a2_ref[...] = jnp.zeros(a2_ref.shape, F32)

    def chunk(c, carry):
        k0 = pl.multiple_of(c * tk, tk)
        kc = k_ref[0, pl.ds(k0, tk), :]
        vc = v_ref[0, pl.ds(k0, tk), :]
        bias = slope2 * jnp.abs(relf + (k0 - i * tq).astype(F32))

        def one(qm, km, m_ref, l_ref, a_ref):
            s = _nt_dot(qm, km) - bias
            m_old = m_ref[...]
            m_new = jnp.maximum(m_old, jnp.max(s, axis=-1, keepdims=True))
            p = jnp.exp2(s - m_new)
            alpha = jnp.exp2(m_old - m_new)
            l_ref[...] = alpha * l_ref[...] + jnp.sum(p, axis=-1, keepdims=True)
            a_ref[...] = alpha * a_ref[...] + jnp.dot(p.astype(BF16), vc, preferred_element_type=F32)
            m_ref[...] = m_new

        one(q1, kc[:, :HEAD_DIM], m1_ref, l1_ref, a1_ref)
        one(q2, kc[:, HEAD_DIM:], m2_ref, l2_ref, a2_ref)
        return carry

    lax.fori_loop(0, n_chunks, chunk, 0)

    lv = lam_ref[...]
    s01 = jnp.sum(lv[0:1, :] * lv[1:2, :], axis=-1, keepdims=True)
    s23 = jnp.sum(lv[2:3, :] * lv[3:4, :], axis=-1, keepdims=True)
    lam = jnp.exp(s01) - jnp.exp(s23) + lam_init
    o = a1_ref[...] / l1_ref[...] - lam * (a2_ref[...] / l2_ref[...])
    ms = jnp.mean(o * o, axis=-1, keepdims=True)
    o = o * lax.rsqrt(ms + LN_EPS) * g_ref[...] * (1.0 - lam_init)
    o_ref[0] = o.astype(BF16)


def _diff_attention(proj, lam_vecs, subln_g, layer_idx):
    B, S, _ = proj.shape
    tq = min(S, 512)
    tk = min(S, 1024)
    lam_init = 0.8 - 0.6 * math.exp(-0.3 * layer_idx)
    slopes2 = jnp.asarray(_alibi_slopes(A_HEADS) * LOG2E)
    nq = A_QK // A_VDIM
    kern = functools.partial(_diff_kernel, tq=tq, tk=tk, n_chunks=S // tk, lam_init=lam_init)
    return pl.pallas_call(
        kern,
        grid=(B, A_HEADS, S // tq),
        in_specs=[
            pl.BlockSpec(memory_space=pltpu.SMEM),
            pl.BlockSpec((1, tq, A_VDIM), lambda b, h, i: (b, i, h)),
            pl.BlockSpec((1, S, A_VDIM), lambda b, h, i: (b, 0, nq + h)),
            pl.BlockSpec((1, S, A_VDIM), lambda b, h, i: (b, 0, 2 * nq + h)),
            pl.BlockSpec((4, HEAD_DIM), lambda b, h, i: (0, 0)),
            pl.BlockSpec((1, A_VDIM), lambda b, h, i: (0, 0)),
        ],
        out_specs=pl.BlockSpec((1, tq, A_VDIM), lambda b, h, i: (b, i, h)),
        out_shape=jax.ShapeDtypeStruct((B, S, A_V), BF16),
        scratch_shapes=[
            pltpu.VMEM((tq, 1), F32), pltpu.VMEM((tq, 1), F32), pltpu.VMEM((tq, A_VDIM), F32),
            pltpu.VMEM((tq, 1), F32), pltpu.VMEM((tq, 1), F32), pltpu.VMEM((tq, A_VDIM), F32),
        ],
        compiler_params=_params("parallel", "parallel", "parallel"),
        name="diff_attn",
    )(slopes2, proj, proj, proj, lam_vecs, subln_g.reshape(1, A_VDIM))


def _banded_kernel(slope_ref, sink_ref, q_ref, kp_ref, kc_ref, kn_ref, vp_ref, vc_ref, vn_ref, *rest,
                   tq, radius, n_q, group, n_rows, dist_scale, use_sink, want_lse):
    if want_lse:
        o_ref, lse_ref, kwin, vwin = rest
    else:
        o_ref, kwin, vwin = rest
        lse_ref = None
    t = pl.program_id(2)
    w = tq + 2 * radius
    kwin[0:radius, :] = kp_ref[0]
    kwin[radius:radius + tq, :] = kc_ref[0]
    kwin[radius + tq:w, :] = kn_ref[0]
    vwin[0:radius, :] = vp_ref[0]
    vwin[radius:radius + tq, :] = vc_ref[0]
    vwin[radius + tq:w, :] = vn_ref[0]

    ii = lax.broadcasted_iota(I32, (tq, w), 0)
    jj = lax.broadcasted_iota(I32, (tq, w), 1)
    rel = jnp.abs(jj - radius - ii)
    kpos = t * tq - radius + jj
    valid = (rel <= radius) & (kpos >= 0) & (kpos < n_rows)
    dist = rel.astype(F32) * float(dist_scale)
    lane = lax.broadcasted_iota(I32, (tq, HEAD_DIM), 1)
    lse_tile = jnp.zeros((tq, HEAD_DIM), F32)

    for h in range(n_q):
        hk = h // group
        qh = q_ref[0, :, h * HEAD_DIM:(h + 1) * HEAD_DIM]
        kh = kwin[:, hk * HEAD_DIM:(hk + 1) * HEAD_DIM]
        vh = vwin[:, hk * HEAD_DIM:(hk + 1) * HEAD_DIM]
        s = jnp.where(valid, _nt_dot(qh, kh) - slope_ref[h] * dist, NEG)
        m = jnp.max(s, axis=-1, keepdims=True)
        if use_sink:
            m = jnp.maximum(m, sink_ref[h])
        p = jnp.exp2(s - m)
        den = jnp.sum(p, axis=-1, keepdims=True)
        if use_sink:
            den = den + jnp.exp2(sink_ref[h] - m)
        o = jnp.dot(p.astype(BF16), vh, preferred_element_type=F32) / den
        o_ref[0, :, h * HEAD_DIM:(h + 1) * HEAD_DIM] = o.astype(BF16)
        if want_lse:
            lse_tile = jnp.where(lane == h, m + jnp.log2(den), lse_tile)
    if want_lse:
        lse_ref[0] = lse_tile


def _banded_attention(src, *, n_rows, n_res, src_cols, q_blk, k_blk, v_blk, n_q, group, radius,
                      dist_scale, slopes2, sink2, want_lse):
    B = src.shape[0]
    qw = n_q * HEAD_DIM
    kw = (n_q // group) * HEAD_DIM
    tq = min(n_rows, 256)
    nt = n_rows // tq
    per_t = tq // radius
    last_halo = n_rows // radius - 1
    qpg = src_cols // qw
    kpg = src_cols // kw

    def q_map(b, r, t):
        return (b, t, r * qpg + q_blk)

    def cur_map(blk):
        return lambda b, r, t: (b, t, r * kpg + blk)

    def prev_map(blk):
        return lambda b, r, t: (b, jnp.maximum(t * per_t - 1, 0), r * kpg + blk)

    def next_map(blk):
        return lambda b, r, t: (b, jnp.minimum((t + 1) * per_t, last_halo), r * kpg + blk)

    out_shape = [jax.ShapeDtypeStruct((B, n_rows, n_res * qw), BF16)]
    out_specs = [pl.BlockSpec((1, tq, qw), lambda b, r, t: (b, t, r))]
    if want_lse:
        out_shape.append(jax.ShapeDtypeStruct((B, n_rows, n_res * HEAD_DIM), F32))
        out_specs.append(pl.BlockSpec((1, tq, HEAD_DIM), lambda b, r, t: (b, t, r)))
    kern = functools.partial(_banded_kernel, tq=tq, radius=radius, n_q=n_q, group=group, n_rows=n_rows,
                             dist_scale=dist_scale, use_sink=sink2 is not None, want_lse=want_lse)
    if sink2 is None:
        sink2 = jnp.zeros((n_q,), F32)
    return pl.pallas_call(
        kern,
        grid=(B, n_res, nt),
        in_specs=[
            pl.BlockSpec(memory_space=pltpu.SMEM),
            pl.BlockSpec(memory_space=pltpu.SMEM),
            pl.BlockSpec((1, tq, qw), q_map),
            pl.BlockSpec((1, radius, kw), prev_map(k_blk)),
            pl.BlockSpec((1, tq, kw), cur_map(k_blk)),
            pl.BlockSpec((1, radius, kw), next_map(k_blk)),
            pl.BlockSpec((1, radius, kw), prev_map(v_blk)),
            pl.BlockSpec((1, tq, kw), cur_map(v_blk)),
            pl.BlockSpec((1, radius, kw), next_map(v_blk)),
        ],
        out_specs=out_specs,
        out_shape=out_shape,
        scratch_shapes=[pltpu.VMEM((tq + 2 * radius, kw), BF16), pltpu.VMEM((tq + 2 * radius, kw), BF16)],
        compiler_params=_params("parallel", "parallel", "parallel"),
        name="banded_attn",
    )(slopes2, sink2, src, src, src, src, src, src, src)


def _post_attention(y, x, g1, lng, lnb, sc2, sh2, rwh_ref, rwl_ref, x1_ref, hp_ref, lg_ref, alpha):
    z = alpha * x + g1 * y
    mu = jnp.mean(z, axis=-1, keepdims=True)
    zc = z - mu
    var = jnp.mean(zc * zc, axis=-1, keepdims=True)
    x1 = zc * lax.rsqrt(var + LN_EPS) * lng + lnb
    x1_ref[0] = x1
    h2 = x1 * (1.0 + sc2) + sh2
    hb = h2.astype(BF16)
    hf = hb.astype(F32)
    bits = lax.bitcast_convert_type(hf, U32)
    half = bits.shape[1] // 2
    packed = bits[:, :half] | (bits[:, half:] >> 16)
    for ck in range(half // HEAD_DIM):
        hp_ref[0, :, ck, :] = packed[:, ck * HEAD_DIM:(ck + 1) * HEAD_DIM]
    lo = (h2 - hf).astype(BF16)
    rwh = rwh_ref[...]
    lg_ref[0] = (jnp.dot(hb, rwh, preferred_element_type=F32)
                 + jnp.dot(hb, rwl_ref[...], preferred_element_type=F32)
                 + jnp.dot(lo, rwh, preferred_element_type=F32))


def _outproj_ab_kernel(ya_ref, o1_ref, o2_ref, o3_ref, l1_ref, l2_ref, l3_ref, w_ref, x_ref, g1_ref,
                       lng_ref, lnb_ref, sh2_ref, sc2_ref, rwh_ref, rwl_ref, x1_ref, hp_ref, lg_ref,
                       yb_ref, *, alpha):
    l1 = l1_ref[0]
    l2 = l2_ref[0]
    l3 = l3_ref[0]
    mx = jnp.maximum(jnp.maximum(l1, l2), l3)
    e1 = jnp.exp2(l1 - mx)
    e2 = jnp.exp2(l2 - mx)
    e3 = jnp.exp2(l3 - mx)
    inv = 1.0 / (e1 + e2 + e3)
    w1 = e1 * inv
    w2 = e2 * inv
    w3 = e3 * inv
    for h in range(B_HEADS):
        sl = slice(h * HEAD_DIM, (h + 1) * HEAD_DIM)
        yb = (w1[:, h:h + 1] * o1_ref[0, :, sl].astype(F32)
              + w2[:, h:h + 1] * o2_ref[0, :, sl].astype(F32)
              + w3[:, h:h + 1] * o3_ref[0, :, sl].astype(F32))
        yb_ref[:, sl] = yb.astype(BF16)
    y = (jnp.dot(ya_ref[0], w_ref[0:A_V, :], preferred_element_type=F32)
         + jnp.dot(yb_ref[...], w_ref[A_V:A_V + B_W, :], preferred_element_type=F32))
    _post_attention(y, x_ref[0], g1_ref[0], lng_ref[...], lnb_ref[...], sc2_ref[0], sh2_ref[0],
                    rwh_ref, rwl_ref, x1_ref, hp_ref, lg_ref, alpha)


def _outproj_c_kernel(o_ref, w_ref, x_ref, g1_ref, lng_ref, lnb_ref, sh2_ref, sc2_ref, rwh_ref, rwl_ref,
                      x1_ref, hp_ref, lg_ref, *, alpha):
    y = jnp.dot(o_ref[0], w_ref[...], preferred_element_type=F32)
    _post_attention(y, x_ref[0], g1_ref[0], lng_ref[...], lnb_ref[...], sc2_ref[0], sh2_ref[0],
                    rwh_ref, rwl_ref, x1_ref, hp_ref, lg_ref, alpha)


def _out_projection(attn_inputs, w_bf16, x, mod, lng, lnb, rw_hi, rw_lo, alpha, mixer):
    B, S, D = x.shape
    E = rw_hi.shape[1]
    tm = min(S, 512)
    row = lambda b, i: (b, i, 0)
    const2 = lambda b, i: (0, 0)
    tail_specs = [
        pl.BlockSpec(w_bf16.shape, const2),
        pl.BlockSpec((1, tm, D), row),
        pl.BlockSpec((1, 1, D), lambda b, i: (b, 0, 2)),
        pl.BlockSpec((1, D), const2),
        pl.BlockSpec((1, D), const2),
        pl.BlockSpec((1, 1, D), lambda b, i: (b, 0, 3)),
        pl.BlockSpec((1, 1, D), lambda b, i: (b, 0, 4)),
        pl.BlockSpec((D, E), const2),
        pl.BlockSpec((D, E), const2),
    ]
    tail_args = (w_bf16, x, mod, lng.reshape(1, D), lnb.reshape(1, D), mod, mod, rw_hi, rw_lo)
    nck = D // 2 // HEAD_DIM
    out_shape = [jax.ShapeDtypeStruct((B, S, D), F32), jax.ShapeDtypeStruct((B, S, nck, HEAD_DIM), U32),
                 jax.ShapeDtypeStruct((B, S, E), F32)]
    out_specs = [pl.BlockSpec((1, tm, D), row), pl.BlockSpec((1, tm, nck, HEAD_DIM), lambda b, i: (b, i, 0, 0)),
                 pl.BlockSpec((1, tm, E), row)]
    if mixer == "ab":
        ya, o1, o2, o3, l1, l2, l3 = attn_inputs
        in_specs = ([pl.BlockSpec((1, tm, A_V), row)] + [pl.BlockSpec((1, tm, B_W), row)] * 3
                    + [pl.BlockSpec((1, tm, HEAD_DIM), row)] * 3 + tail_specs)
        kern = functools.partial(_outproj_ab_kernel, alpha=alpha)
        scratch = [pltpu.VMEM((tm, B_W), BF16)]
        args = (ya, o1, o2, o3, l1, l2, l3) + tail_args
    else:
        (o,) = attn_inputs
        in_specs = [pl.BlockSpec((1, tm, o.shape[2]), row)] + tail_specs
        kern = functools.partial(_outproj_c_kernel, alpha=alpha)
        scratch = []
        args = (o,) + tail_args
    return pl.pallas_call(
        kern,
        grid=(B, S // tm),
        in_specs=in_specs,
        out_specs=out_specs,
        out_shape=out_shape,
        scratch_shapes=scratch,
        compiler_params=_params("parallel", "parallel"),
        name="out_proj_" + mixer,
    )(*args)


def _excl_cumsum_lanes(x, tri):
    n = x.shape[1] // HEAD_DIM
    carry = jnp.zeros((x.shape[0], 1), F32)
    outs = []
    for c in range(n):
        xc = x[:, c * HEAD_DIM:(c + 1) * HEAD_DIM]
        outs.append(jnp.dot(xc.astype(BF16), tri, preferred_element_type=F32) + carry)
        carry = carry + jnp.sum(xc, axis=1, keepdims=True)
    return jnp.concatenate(outs, axis=1)


def _tables_kernel(lg_ref, cnt_ref, dst_ref, gk_ref, *, n_exp, cap):
    lg = lg_ref[0]
    S = lg.shape[1]
    ex = jnp.exp(lg - jnp.max(lg, axis=0, keepdims=True))
    aff = ex / jnp.sum(ex, axis=0, keepdims=True)
    bits = lax.bitcast_convert_type(aff, I32)

    def search(i, cur):
        cand = cur | jnp.left_shift(jnp.int32(1), 30 - i)
        cnt = jnp.sum(jnp.where(bits >= cand, 1.0, 0.0), axis=1, keepdims=True)
        return jnp.where(cnt >= cap, cand, cur)

    thr = lax.fori_loop(0, 31, search, jnp.zeros((n_exp, 1), I32))
    tri = jnp.where(lax.broadcasted_iota(I32, (HEAD_DIM, HEAD_DIM), 0)
                    < lax.broadcasted_iota(I32, (HEAD_DIM, HEAD_DIM), 1), 1.0, 0.0).astype(BF16)
    gt = bits > thr
    eq = jnp.where(bits == thr, 1.0, 0.0)
    need = cap - jnp.sum(jnp.where(gt, 1.0, 0.0), axis=1, keepdims=True)
    sel = jnp.where(gt, 1.0, jnp.where(_excl_cumsum_lanes(eq, tri) < need, eq, 0.0))
    pos = _excl_cumsum_lanes(sel, tri)

    slot = lax.broadcasted_iota(I32, (n_exp, S), 0).astype(F32)
    run = jnp.zeros((1, S), F32)
    dst = jnp.zeros((n_exp, S), F32)
    gk = jnp.zeros((n_exp, S), F32)
    for e in range(n_exp):
        se = sel[e:e + 1, :]
        hit = (slot == run) & (se > 0.0)
        dst = jnp.where(hit, pos[e:e + 1, :] + float(e * cap), dst)
        gk = jnp.where(hit, aff[e:e + 1, :], gk)
        run = run + se
    cnt_ref[0] = run.astype(I32)
    dst_ref[0] = dst.astype(I32)
    gk_ref[0] = gk


def _routing_tables(logits_t, cap):
    B, E, S = logits_t.shape
    kern = functools.partial(_tables_kernel, n_exp=E, cap=cap)
    return pl.pallas_call(
        kern,
        grid=(B,),
        in_specs=[pl.BlockSpec((1, E, S), lambda b: (b, 0, 0))],
        out_specs=[pl.BlockSpec((1, 1, S), lambda b: (b, 0, 0)), pl.BlockSpec((1, E, S), lambda b: (b, 0, 0)),
                   pl.BlockSpec((1, E, S), lambda b: (b, 0, 0))],
        out_shape=[jax.ShapeDtypeStruct((B, 1, S), I32), jax.ShapeDtypeStruct((B, E, S), I32),
                   jax.ShapeDtypeStruct((B, E, S), F32)],
        compiler_params=_params("parallel"),
        name="routing_tables",
    )(logits_t)


def _wait_rows(src_ref, dst_ref, sem, n):
    @pl.when(n > 0)
    def _():
        pltpu.make_async_copy(src_ref.at[pl.ds(0, n)], dst_ref.at[pl.ds(0, n)], sem).wait()


def _dispatch_kernel(cnt_ref, dst_ref, gk_ref, hp_ref, xg_ref, grow_ref, sem, *, tc, cap, seq, bcap):
    b = pl.program_id(0)
    tok0 = b * seq + pl.program_id(1) * tc

    def per_token(t, n):
        def per_choice(k, n):
            code = dst_ref[0, k, t]
            e = code // cap
            j = code - e * cap
            row = e * bcap + b * cap + j
            pltpu.make_async_copy(hp_ref.at[tok0 + t], xg_ref.at[row], sem).start()
            grow_ref[0, e, j] = gk_ref[0, k, t]
            return n + 1

        return lax.fori_loop(0, cnt_ref[0, 0, t], per_choice, n)

    n = lax.fori_loop(0, tc, per_token, 0)

    _wait_rows(hp_ref, xg_ref, sem, n)


def _dispatch(cnt, dst, gk, hp, n_exp, cap):
    B, _, S = cnt.shape
    W = hp.shape[1:]
    tc = min(S, 512)
    kern = functools.partial(_dispatch_kernel, tc=tc, cap=cap, seq=S, bcap=B * cap)
    smem = functools.partial(pl.BlockSpec, memory_space=pltpu.SMEM)
    return pl.pallas_call(
        kern,
        grid=(B, S // tc),
        in_specs=[smem((1, 1, tc), lambda b, c: (b, 0, c)), smem((1, n_exp, tc), lambda b, c: (b, 0, c)),
                  smem((1, n_exp, tc), lambda b, c: (b, 0, c)), pl.BlockSpec(memory_space=pl.ANY)],
        out_specs=[pl.BlockSpec(memory_space=pl.ANY), smem((1, n_exp, cap), lambda b, c: (b, 0, 0))],
        out_shape=[jax.ShapeDtypeStruct((n_exp * B * cap, *W), hp.dtype),
                   jax.ShapeDtypeStruct((B, n_exp, cap), F32)],
        scratch_shapes=[pltpu.SemaphoreType.DMA],
        compiler_params=_params("arbitrary", "arbitrary"),
        name="moe_dispatch",
    )(cnt, dst, gk, hp)


def _return_kernel(cnt_ref, dst_ref, off_ref, y_ref, yt_ref, sem, *, tc, cap, bcap):
    b = pl.program_id(0)
    base = off_ref[0, 0, 0]

    def per_token(t, n):
        def per_choice(k, n):
            code = dst_ref[0, k, t]
            e = code // cap
            row = e * bcap + b * cap + (code - e * cap)
            pltpu.make_async_copy(y_ref.at[row], yt_ref.at[base + n], sem).start()
            return n + 1

        return lax.fori_loop(0, cnt_ref[0, 0, t], per_choice, n)

    n = lax.fori_loop(0, tc, per_token, 0)

    _wait_rows(y_ref, yt_ref, sem, n)


def _return_rows(cnt, dst, off, y, n_exp, cap):
    B, _, S = cnt.shape
    tc = min(S, 512)
    kern = functools.partial(_return_kernel, tc=tc, cap=cap, bcap=B * cap)
    smem = functools.partial(pl.BlockSpec, memory_space=pltpu.SMEM)
    return pl.pallas_call(
        kern,
        grid=(B, S // tc),
        in_specs=[smem((1, 1, tc), lambda b, c: (b, 0, c)), smem((1, n_exp, tc), lambda b, c: (b, 0, c)),
                  smem((1, 1, tc), lambda b, c: (b, 0, c)), pl.BlockSpec(memory_space=pl.ANY)],
        out_specs=pl.BlockSpec(memory_space=pl.ANY),
        out_shape=jax.ShapeDtypeStruct(y.shape, y.dtype),
        scratch_shapes=[pltpu.SemaphoreType.DMA],
        compiler_params=_params("arbitrary", "arbitrary"),
        name="moe_return",
    )(cnt, dst, off, y)


def _unpack_rows(xp_ref, xs_ref):
    nck = xp_ref.shape[1]
    half = nck * HEAD_DIM
    for ck in range(nck):
        u = xp_ref[:, ck, :]
        lo = ck * HEAD_DIM
        xs_ref[:, lo:lo + HEAD_DIM] = lax.bitcast_convert_type(u & jnp.uint32(0xFFFF0000), F32).astype(BF16)
        xs_ref[:, half + lo:half + lo + HEAD_DIM] = lax.bitcast_convert_type(u << 16, F32).astype(BF16)


def _pack_pairs(hi, lo):
    hb = lax.bitcast_convert_type(hi.astype(BF16).astype(F32), U32)
    lb = lax.bitcast_convert_type(lo.astype(BF16).astype(F32), U32)
    return hb | (lb >> 16)


def _ffn_up_kernel(xp_ref, wg_ref, wu_ref, o_ref, xs_ref):
    @pl.when(pl.program_id(2) == 0)
    def _():
        _unpack_rows(xp_ref, xs_ref)

    x = xs_ref[...]
    g = jnp.dot(x, wg_ref[0].astype(BF16), preferred_element_type=F32)
    u = jnp.dot(x, wu_ref[0].astype(BF16), preferred_element_type=F32)
    o_ref[...] = (g * jax.nn.sigmoid(g) * u).astype(BF16)


def _ffn_up(xg, w_gate, w_up, rows_per_expert):
    E, D, Fh = w_gate.shape
    tm = min(rows_per_expert, 2048)
    mt = rows_per_expert // tm
    tn = min(Fh, 256)
    return pl.pallas_call(
        _ffn_up_kernel,
        grid=(E, mt, Fh // tn),
        in_specs=[pl.BlockSpec((tm, *xg.shape[1:]), lambda e, m, n: (e * mt + m, 0, 0)),
                  pl.BlockSpec((1, D, tn), lambda e, m, n: (e, 0, n)),
                  pl.BlockSpec((1, D, tn), lambda e, m, n: (e, 0, n))],
        out_specs=pl.BlockSpec((tm, tn), lambda e, m, n: (e * mt + m, n)),
        out_shape=jax.ShapeDtypeStruct((E * rows_per_expert, Fh), BF16),
        scratch_shapes=[pltpu.VMEM((tm, D), BF16)],
        compiler_params=_params("parallel", "parallel", "arbitrary"),
        name="ffn_up",
    )(xg, w_gate, w_up)


def _ffn_down_kernel(h_ref, wa_ref, wb_ref, g_ref, o_ref, *, n_steps, cps):
    h = h_ref[...]
    g = g_ref[...]
    ya = jnp.dot(h, wa_ref[0].astype(BF16), preferred_element_type=F32) * g
    yb = jnp.dot(h, wb_ref[0].astype(BF16), preferred_element_type=F32) * g
    packed = _pack_pairs(ya, yb)
    n = pl.program_id(2)
    for k in range(n_steps):
        @pl.when(n == k)
        def _():
            for cc in range(cps):
                o_ref[:, k * cps + cc, :] = packed[:, cc * HEAD_DIM:(cc + 1) * HEAD_DIM]


def _ffn_down(hid, w_down, gate_col, rows_per_expert):
    E, Fh, D = w_down.shape
    tm = min(rows_per_expert, 2048)
    mt = rows_per_expert // tm
    half = D // 2
    tnh = min(half, 256)
    n_steps = half // tnh
    nck = half // HEAD_DIM
    kern = functools.partial(_ffn_down_kernel, n_steps=n_steps, cps=tnh // HEAD_DIM)
    return pl.pallas_call(
        kern,
        grid=(E, mt, n_steps),
        in_specs=[pl.BlockSpec((tm, Fh), lambda e, m, n: (e * mt + m, 0)),
                  pl.BlockSpec((1, Fh, tnh), lambda e, m, n: (e, 0, n)),
                  pl.BlockSpec((1, Fh, tnh), lambda e, m, n: (e, 0, n + n_steps)),
                  pl.BlockSpec((tm, 1), lambda e, m, n: (e * mt + m, 0))],
        out_specs=pl.BlockSpec((tm, nck, HEAD_DIM), lambda e, m, n: (e * mt + m, 0, 0)),
        out_shape=jax.ShapeDtypeStruct((E * rows_per_expert, nck, HEAD_DIM), U32),
        compiler_params=_params("parallel", "parallel", "arbitrary"),
        name="ffn_down",
    )(hid, w_down, w_down, gate_col)


def _combine_kernel(offt_ref, x_ref, off_ref, cnt_ref, g2_ref, lng_ref, lnb_ref, yt_ref, o_ref,
                    buf, rows, acc, sem, *, tm, ch, n_total, n_tiles, n_tiles_s, alpha):
    tile = pl.program_id(0) * n_tiles_s + pl.program_id(1)
    o0 = offt_ref[tile]
    o1 = offt_ref[tile + 1]
    n_chunks = jnp.maximum((o1 - o0 + ch - 1) // ch, 1)
    lo_col = off_ref[0]
    hi_col = lo_col + cnt_ref[0]
    lane = lax.broadcasted_iota(I32, (tm, ch), 1)
    acc[...] = jnp.zeros(acc.shape, F32)

    def fetch(want, slot):
        start = jnp.minimum(want, n_total - ch)
        return pltpu.make_async_copy(yt_ref.at[pl.ds(start, ch)], buf.at[slot], sem.at[slot])

    @pl.when(tile == 0)
    def _():
        fetch(o0, 0).start()

    def chunk(c, carry):
        slot = c % 2
        want = o0 + c * ch
        fetch(want, slot).wait()

        @pl.when(c + 1 < n_chunks)
        def _():
            fetch(want + ch, 1 - slot).start()

        _unpack_rows(buf.at[slot], rows)
        row = lane + jnp.minimum(want, n_total - ch)
        own = (row >= lo_col) & (row < hi_col) & (row >= want)
        acc[...] += jnp.dot(jnp.where(own, 1.0, 0.0).astype(BF16), rows[...], preferred_element_type=F32)
        return carry

    lax.fori_loop(0, n_chunks, chunk, 0)

    @pl.when(tile + 1 < n_tiles)
    def _():
        fetch(o1, 0).start()

    z = alpha * x_ref[0] + g2_ref[0] * acc[...]
    mu = jnp.mean(z, axis=-1, keepdims=True)
    zc = z - mu
    var = jnp.mean(zc * zc, axis=-1, keepdims=True)
    o_ref[0] = zc * lax.rsqrt(var + LN_EPS) * lng_ref[...] + lnb_ref[...]


def _combine(x1, mod, lng, lnb, yt, off, cnt, alpha):
    B, S, D = x1.shape
    n_total = yt.shape[0]
    tm = min(S, 256)
    ch = min(n_total, 256)
    n_tiles_s = S // tm
    off_flat = off.reshape(B * S)
    offt = jnp.concatenate([off_flat[::tm], jnp.full((1,), n_total, I32)])
    kern = functools.partial(_combine_kernel, tm=tm, ch=ch, n_total=n_total, n_tiles=B * n_tiles_s,
                             n_tiles_s=n_tiles_s, alpha=alpha)
    grid_spec = pltpu.PrefetchScalarGridSpec(
        num_scalar_prefetch=1,
        grid=(B, n_tiles_s),
        in_specs=[
            pl.BlockSpec((1, tm, D), lambda b, i, o: (b, i, 0)),
            pl.BlockSpec((1, tm, 1), lambda b, i, o: (b, i, 0)),
            pl.BlockSpec((1, tm, 1), lambda b, i, o: (b, i, 0)),
            pl.BlockSpec((1, 1, D), lambda b, i, o: (b, 0, 5)),
            pl.BlockSpec((1, D), lambda b, i, o: (0, 0)),
            pl.BlockSpec((1, D), lambda b, i, o: (0, 0)),
            pl.BlockSpec(memory_space=pl.ANY),
        ],
        out_specs=pl.BlockSpec((1, tm, D), lambda b, i, o: (b, i, 0)),
        scratch_shapes=[pltpu.VMEM((2, ch, *yt.shape[1:]), U32), pltpu.VMEM((ch, D), BF16),
                        pltpu.VMEM((tm, D), F32), pltpu.SemaphoreType.DMA((2,))],
    )
    return pl.pallas_call(
        kern,
        grid_spec=grid_spec,
        out_shape=jax.ShapeDtypeStruct((B, S, D), F32),
        compiler_params=_params("arbitrary", "arbitrary"),
        name="moe_combine",
    )(offt, x1, off.reshape(B, S, 1), cnt.reshape(B, S, 1), mod, lng.reshape(1, D), lnb.reshape(1, D), yt)


def _moe_sublayer(x1, hp, logits, mod, lng, lnb, w_gate, w_up, w_down, alpha):
    B, S, D = x1.shape
    E = logits.shape[-1]
    cap = EC_CAPACITY_FACTOR * S // E
    cnt, dst, gk = _routing_tables(jnp.swapaxes(logits, 1, 2), cap)
    xg, grow = _dispatch(cnt, dst, gk, hp.reshape(B * S, *hp.shape[2:]), E, cap)
    gate_col = jnp.swapaxes(grow, 0, 1).reshape(E * B * cap, 1)
    hid = _ffn_up(xg, w_gate, w_up, B * cap)
    y = _ffn_down(hid, w_down, gate_col, B * cap)
    cflat = cnt.reshape(B * S)
    off = (jnp.cumsum(cflat) - cflat).astype(I32).reshape(B, 1, S)
    yt = _return_rows(cnt, dst, off, y, E, cap)
    return _combine(x1, mod, lng, lnb, yt, off, cnt, alpha)


def _split_bf16(w):
    hi = w.astype(BF16)
    return hi, (w - hi.astype(F32)).astype(BF16)


def kernel(x, c, ada_w, ada_b, ln_g, ln_b, ab_w_in, ab_w_out, diff_lambda, diff_subln_g, c_w_in, c_w_out,
           c_sink, router_w, w_gate, w_up, w_down):
    B, S, D = x.shape
    depth = ada_w.shape[0]
    alpha = (2.0 * depth) ** 0.25
    qscale = HEAD_DIM ** -0.5 * LOG2E
    mod_all = _modulation(c, ada_w, ada_b)

    ab_scale = np.ones((1, AB_IN), np.float32)
    ab_scale[:, :A_QK] = qscale
    ab_scale[:, 2 * A_QK + A_V:2 * A_QK + A_V + B_W] = qscale
    c_scale = np.ones((1, C_IN), np.float32)
    c_scale[:, :C_QW] = qscale

    for l in range(depth):
        mod = mod_all[l][:, None, :]
        i = l // 2
        rw_hi, rw_lo = _split_bf16(router_w[l])
        if l % 2 == 0:
            proj = _in_projection(x, mod, ab_w_in[i].astype(BF16), jnp.asarray(ab_scale))
            ya = _diff_attention(proj, diff_lambda[i], diff_subln_g[i], l)
            slopes_b = jnp.asarray(_alibi_slopes(B_HEADS) * LOG2E)
            outs = []
            lses = []
            for window, dil in B_BRANCHES:
                n_rows = S // dil
                src = proj.reshape(B, n_rows, dil * AB_IN)
                o, lse = _banded_attention(
                    src, n_rows=n_rows, n_res=dil, src_cols=AB_IN, q_blk=3, k_blk=4, v_blk=5, n_q=B_HEADS,
                    group=1, radius=window // (2 * dil), dist_scale=dil, slopes2=slopes_b, sink2=None,
                    want_lse=True)
                outs.append(o.reshape(B, S, B_W))
                lses.append(lse.reshape(B, S, HEAD_DIM))
            x1, hp, logits = _out_projection((ya, *outs, *lses), ab_w_out[i].astype(BF16), x, mod,
                                             ln_g[l, 0], ln_b[l, 0], rw_hi, rw_lo, alpha, "ab")
        else:
            proj = _in_projection(x, mod, c_w_in[i].astype(BF16), jnp.asarray(c_scale))
            (o,) = _banded_attention(
                proj, n_rows=S, n_res=1, src_cols=C_IN, q_blk=0, k_blk=C_QW // C_KVW, v_blk=C_QW // C_KVW + 1,
                n_q=C_Q_HEADS, group=C_Q_HEADS // C_KV_HEADS, radius=C_RADIUS, dist_scale=1,
                slopes2=jnp.asarray(_alibi_slopes(C_Q_HEADS) * LOG2E), sink2=c_sink[i] * LOG2E, want_lse=False)
            x1, hp, logits = _out_projection((o,), c_w_out[i].astype(BF16), x, mod, ln_g[l, 0], ln_b[l, 0],
                                             rw_hi, rw_lo, alpha, "c")
        x = _moe_sublayer(x1, hp, logits, mod, ln_g[l, 1], ln_b[l, 1], w_gate[l], w_up[l], w_down[l], alpha)
    return x
```

```python
import functools
import math

import numpy as np
import jax
import jax.numpy as jnp
from jax import lax
from jax.experimental import pallas as pl
from jax.experimental.pallas import tpu as pltpu

F32 = jnp.float32
BF16 = jnp.bfloat16
I32 = jnp.int32
U32 = jnp.uint32

HEAD_DIM = 128
A_HEADS = 4
A_VDIM = 2 * HEAD_DIM
B_HEADS = 8
B_BRANCHES = ((128, 1), (512, 4), (2048, 16))
C_Q_HEADS = 16
C_KV_HEADS = 4
C_RADIUS = 128
EC_CAPACITY_FACTOR = 2
LN_EPS = 1e-5
NEG = -1e30
LOG2E = 1.4426950408889634

A_QK = A_HEADS * 2 * HEAD_DIM
A_V = A_HEADS * A_VDIM
B_W = B_HEADS * HEAD_DIM
AB_IN = 2 * A_QK + A_V + 3 * B_W
C_QW = C_Q_HEADS * HEAD_DIM
C_KVW = C_KV_HEADS * HEAD_DIM
C_IN = C_QW + 2 * C_KVW

VMEM_LIMIT_BYTES = 56 * 1024 * 1024


def _params(*sem):
    return pltpu.CompilerParams(dimension_semantics=sem, vmem_limit_bytes=VMEM_LIMIT_BYTES)


def _tile(n, preferred):
    t = min(n, preferred)
    while n % t:
        t //= 2
    return t


def _alibi_slopes(n):
    return np.array([2.0 ** (-8.0 * (i + 1) / n) for i in range(n)], dtype=np.float32)


def _nt_dot(a, b):
    return lax.dot_general(a, b, (((1,), (1,)), ((), ())), preferred_element_type=F32)


def _mod_kernel(c_ref, w_ref, b_ref, o_ref):
    c = c_ref[...]
    cs = (c * jax.nn.sigmoid(c)).astype(BF16)
    o_ref[0] = jnp.dot(cs, w_ref[0].astype(BF16), preferred_element_type=F32) + b_ref[0]


def _modulation(c, ada_w, ada_b):
    L, D, N = ada_w.shape
    B = c.shape[0]
    tn = _tile(N, 1024)
    return pl.pallas_call(
        _mod_kernel,
        grid=(L, N // tn),
        in_specs=[
            pl.BlockSpec((B, D), lambda l, j: (0, 0)),
            pl.BlockSpec((1, D, tn), lambda l, j: (l, 0, j)),
            pl.BlockSpec((1, 1, tn), lambda l, j: (l, 0, j)),
        ],
        out_specs=pl.BlockSpec((1, B, tn), lambda l, j: (l, 0, j)),
        out_shape=jax.ShapeDtypeStruct((L, B, N), F32),
        compiler_params=_params("parallel", "parallel"),
        name="adaln_mod",
    )(c, ada_w, ada_b.reshape(L, 1, N))


def _inproj_kernel(x_ref, sh_ref, sc_ref, w_ref, cs_ref, o_ref, h_ref):
    @pl.when(pl.program_id(2) == 0)
    def _():
        h_ref[...] = (x_ref[0] * (1.0 + sc_ref[0]) + sh_ref[0]).astype(BF16)

    acc = jnp.dot(h_ref[...], w_ref[...], preferred_element_type=F32)
    o_ref[0] = (acc * cs_ref[...]).astype(BF16)


def _in_projection(x, mod, w_bf16, colscale):
    B, S, D = x.shape
    N = w_bf16.shape[1]
    tm = min(S, 1024)
    tn = _tile(N, 1024)
    return pl.pallas_call(
        _inproj_kernel,
        grid=(B, S // tm, N // tn),
        in_specs=[
            pl.BlockSpec((1, tm, D), lambda b, i, j: (b, i, 0)),
            pl.BlockSpec((1, 1, D), lambda b, i, j: (b, 0, 0)),
            pl.BlockSpec((1, 1, D), lambda b, i, j: (b, 0, 1)),
            pl.BlockSpec((D, tn), lambda b, i, j: (0, j)),
            pl.BlockSpec((1, tn), lambda b, i, j: (0, j)),
        ],
        out_specs=pl.BlockSpec((1, tm, tn), lambda b, i, j: (b, i, j)),
        out_shape=jax.ShapeDtypeStruct((B, S, N), BF16),
        scratch_shapes=[pltpu.VMEM((tm, D), BF16)],
        compiler_params=_params("parallel", "parallel", "arbitrary"),
        name="in_proj",
    )(x, mod, mod, w_bf16, colscale)


def _diff_kernel(slope_ref, q_ref, k_ref, v_ref, lam_ref, g_ref, o_ref,
                 m1_ref, l1_ref, a1_ref, m2_ref, l2_ref, a2_ref, *, tq, tk, n_chunks, lam_init):
    h = pl.program_id(1)
    i = pl.program_id(2)
    slope2 = slope_ref[h]
    q = q_ref[0]
    q1 = q[:, :HEAD_DIM]
    q2 = q[:, HEAD_DIM:]
    relf = (lax.broadcasted_iota(I32, (tq, tk), 1) - lax.broadcasted_iota(I32, (tq, tk), 0)).astype(F32)

    m1_ref[...] = jnp.full(m1_ref.shape, NEG, F32)
    m2_ref[...] = jnp.full(m2_ref.shape, NEG, F32)
    l1_ref[...] = jnp.zeros(l1_ref.shape, F32)
    l2_ref[...] = jnp.zeros(l2_ref.shape, F32)
    a1_ref[...] = jnp.zeros(a1_ref.shape, F32)
    a2_ref[...] = jnp.zeros(a2_ref.shape, F32)

    def chunk(c, carry):
        k0 = pl.multiple_of(c * tk, tk)
        kc = k_ref[0, pl.ds(k0, tk), :]
        vc = v_ref[0, pl.ds(k0, tk), :]
        bias = slope2 * jnp.abs(relf + (k0 - i * tq).astype(F32))

        def one(qm, km, m_ref, l_ref, a_ref):
            s = _nt_dot(qm, km) - bias
            m_old = m_ref[...]
            m_new = jnp.maximum(m_old, jnp.max(s, axis=-1, keepdims=True))
            p = jnp.exp2(s - m_new)
            alpha = jnp.exp2(m_old - m_new)
            l_ref[...] = alpha * l_ref[...] + jnp.sum(p, axis=-1, keepdims=True)
            a_ref[...] = alpha * a_ref[...] + jnp.dot(p.astype(BF16), vc, preferred_element_type=F32)
            m_ref[...] = m_new

        one(q1, kc[:, :HEAD_DIM], m1_ref, l1_ref, a1_ref)
        one(q2, kc[:, HEAD_DIM:], m2_ref, l2_ref, a2_ref)
        return carry

    lax.fori_loop(0, n_chunks, chunk, 0)

    lv = lam_ref[...]
    s01 = jnp.sum(lv[0:1, :] * lv[1:2, :], axis=-1, keepdims=True)
    s23 = jnp.sum(lv[2:3, :] * lv[3:4, :], axis=-1, keepdims=True)
    lam = jnp.exp(s01) - jnp.exp(s23) + lam_init
    o = a1_ref[...] / l1_ref[...] - lam * (a2_ref[...] / l2_ref[...])
    ms = jnp.mean(o * o, axis=-1, keepdims=True)
    o = o * lax.rsqrt(ms + LN_EPS) * g_ref[...] * (1.0 - lam_init)
    o_ref[0] = o.astype(BF16)


def _diff_attention(proj, lam_vecs, subln_g, layer_idx):
    B, S, _ = proj.shape
    tq = min(S, 512)
    tk = min(S, 1024)
    lam_init = 0.8 - 0.6 * math.exp(-0.3 * layer_idx)
    slopes2 = jnp.asarray(_alibi_slopes(A_HEADS) * LOG2E)
    nq = A_QK // A_VDIM
    kern = functools.partial(_diff_kernel, tq=tq, tk=tk, n_chunks=S // tk, lam_init=lam_init)
    return pl.pallas_call(
        kern,
        grid=(B, A_HEADS, S // tq),
        in_specs=[
            pl.BlockSpec(memory_space=pltpu.SMEM),
            pl.BlockSpec((1, tq, A_VDIM), lambda b, h, i: (b, i, h)),
            pl.BlockSpec((1, S, A_VDIM), lambda b, h, i: (b, 0, nq + h)),
            pl.BlockSpec((1, S, A_VDIM), lambda b, h, i: (b, 0, 2 * nq + h)),
            pl.BlockSpec((4, HEAD_DIM), lambda b, h, i: (0, 0)),
            pl.BlockSpec((1, A_VDIM), lambda b, h, i: (0, 0)),
        ],
        out_specs=pl.BlockSpec((1, tq, A_VDIM), lambda b, h, i: (b, i, h)),
        out_shape=jax.ShapeDtypeStruct((B, S, A_V), BF16),
        scratch_shapes=[
            pltpu.VMEM((tq, 1), F32), pltpu.VMEM((tq, 1), F32), pltpu.VMEM((tq, A_VDIM), F32),
            pltpu.VMEM((tq, 1), F32), pltpu.VMEM((tq, 1), F32), pltpu.VMEM((tq, A_VDIM), F32),
        ],
        compiler_params=_params("parallel", "parallel", "parallel"),
        name="diff_attn",
    )(slopes2, proj, proj, proj, lam_vecs, subln_g.reshape(1, A_VDIM))


def _banded_kernel(slope_ref, sink_ref, q_ref, kp_ref, kc_ref, kn_ref, vp_ref, vc_ref, vn_ref, *rest,
                   tq, radius, n_q, group, n_rows, dist_scale, use_sink, want_lse):
    if want_lse:
        o_ref, lse_ref, kwin, vwin = rest
    else:
        o_ref, kwin, vwin = rest
        lse_ref = None
    t = pl.program_id(2)
    w = tq + 2 * radius
    kwin[0:radius, :] = kp_ref[0]
    kwin[radius:radius + tq, :] = kc_ref[0]
    kwin[radius + tq:w, :] = kn_ref[0]
    vwin[0:radius, :] = vp_ref[0]
    vwin[radius:radius + tq, :] = vc_ref[0]
    vwin[radius + tq:w, :] = vn_ref[0]

    ii = lax.broadcasted_iota(I32, (tq, w), 0)
    jj = lax.broadcasted_iota(I32, (tq, w), 1)
    rel = jnp.abs(jj - radius - ii)
    kpos = t * tq - radius + jj
    valid = (rel <= radius) & (kpos >= 0) & (kpos < n_rows)
    dist = rel.astype(F32) * float(dist_scale)
    lane = lax.broadcasted_iota(I32, (tq, HEAD_DIM), 1)
    lse_tile = jnp.zeros((tq, HEAD_DIM), F32)

    for h in range(n_q):
        hk = h // group
        qh = q_ref[0, :, h * HEAD_DIM:(h + 1) * HEAD_DIM]
        kh = kwin[:, hk * HEAD_DIM:(hk + 1) * HEAD_DIM]
        vh = vwin[:, hk * HEAD_DIM:(hk + 1) * HEAD_DIM]
        s = jnp.where(valid, _nt_dot(qh, kh) - slope_ref[h] * dist, NEG)
        m = jnp.max(s, axis=-1, keepdims=True)
        if use_sink:
            m = jnp.maximum(m, sink_ref[h])
        p = jnp.exp2(s - m)
        den = jnp.sum(p, axis=-1, keepdims=True)
        if use_sink:
            den = den + jnp.exp2(sink_ref[h] - m)
        o = jnp.dot(p.astype(BF16), vh, preferred_element_type=F32) / den
        o_ref[0, :, h * HEAD_DIM:(h + 1) * HEAD_DIM] = o.astype(BF16)
        if want_lse:
            lse_tile = jnp.where(lane == h, m + jnp.log2(den), lse_tile)
    if want_lse:
        lse_ref[0] = lse_tile


def _banded_attention(src, *, n_rows, n_res, src_cols, q_blk, k_blk, v_blk, n_q, group, radius,
                      dist_scale, slopes2, sink2, want_lse):
    B = src.shape[0]
    qw = n_q * HEAD_DIM
    kw = (n_q // group) * HEAD_DIM
    tq = min(n_rows, 256)
    nt = n_rows // tq
    per_t = tq // radius
    last_halo = n_rows // radius - 1
    qpg = src_cols // qw
    kpg = src_cols // kw

    def q_map(b, r, t):
        return (b, t, r * qpg + q_blk)

    def cur_map(blk):
        return lambda b, r, t: (b, t, r * kpg + blk)

    def prev_map(blk):
        return lambda b, r, t: (b, jnp.maximum(t * per_t - 1, 0), r * kpg + blk)

    def next_map(blk):
        return lambda b, r, t: (b, jnp.minimum((t + 1) * per_t, last_halo), r * kpg + blk)

    out_shape = [jax.ShapeDtypeStruct((B, n_rows, n_res * qw), BF16)]
    out_specs = [pl.BlockSpec((1, tq, qw), lambda b, r, t: (b, t, r))]
    if want_lse:
        out_shape.append(jax.ShapeDtypeStruct((B, n_rows, n_res * HEAD_DIM), F32))
        out_specs.append(pl.BlockSpec((1, tq, HEAD_DIM), lambda b, r, t: (b, t, r)))
    kern = functools.partial(_banded_kernel, tq=tq, radius=radius, n_q=n_q, group=group, n_rows=n_rows,
                             dist_scale=dist_scale, use_sink=sink2 is not None, want_lse=want_lse)
    if sink2 is None:
        sink2 = jnp.zeros((n_q,), F32)
    return pl.pallas_call(
        kern,
        grid=(B, n_res, nt),
        in_specs=[
            pl.BlockSpec(memory_space=pltpu.SMEM),
            pl.BlockSpec(memory_space=pltpu.SMEM),
            pl.BlockSpec((1, tq, qw), q_map),
            pl.BlockSpec((1, radius, kw), prev_map(k_blk)),
            pl.BlockSpec((1, tq, kw), cur_map(k_blk)),
            pl.BlockSpec((1, radius, kw), next_map(k_blk)),
            pl.BlockSpec((1, radius, kw), prev_map(v_blk)),
            pl.BlockSpec((1, tq, kw), cur_map(v_blk)),
            pl.BlockSpec((1, radius, kw), next_map(v_blk)),
        ],
        out_specs=out_specs,
        out_shape=out_shape,
        scratch_shapes=[pltpu.VMEM((tq + 2 * radius, kw), BF16), pltpu.VMEM((tq + 2 * radius, kw), BF16)],
        compiler_params=_params("parallel", "parallel", "parallel"),
        name="banded_attn",
    )(slopes2, sink2, src, src, src, src, src, src, src)


def _post_attention(y, x, g1, lng, lnb, sc2, sh2, rwh_ref, rwl_ref, x1_ref, hp_ref, lg_ref, alpha):
    z = alpha * x + g1 * y
    mu = jnp.mean(z, axis=-1, keepdims=True)
    zc = z - mu
    var = jnp.mean(zc * zc, axis=-1, keepdims=True)
    x1 = zc * lax.rsqrt(var + LN_EPS) * lng + lnb
    x1_ref[0] = x1
    h2 = x1 * (1.0 + sc2) + sh2
    hb = h2.astype(BF16)
    hf = hb.astype(F32)
    bits = lax.bitcast_convert_type(hf, U32)
    half = bits.shape[1] // 2
    packed = bits[:, :half] | (bits[:, half:] >> 16)
    for ck in range(half // HEAD_DIM):
        hp_ref[0, :, ck, :] = packed[:, ck * HEAD_DIM:(ck + 1) * HEAD_DIM]
    lo = (h2 - hf).astype(BF16)
    rwh = rwh_ref[...]
    lg_ref[0] = (jnp.dot(hb, rwh, preferred_element_type=F32)
                 + jnp.dot(hb, rwl_ref[...], preferred_element_type=F32)
                 + jnp.dot(lo, rwh, preferred_element_type=F32))


def _outproj_ab_kernel(ya_ref, o1_ref, o2_ref, o3_ref, l1_ref, l2_ref, l3_ref, w_ref, x_ref, g1_ref,
                       lng_ref, lnb_ref, sh2_ref, sc2_ref, rwh_ref, rwl_ref, x1_ref, hp_ref, lg_ref,
                       yb_ref, *, alpha):
    l1 = l1_ref[0]
    l2 = l2_ref[0]
    l3 = l3_ref[0]
    mx = jnp.maximum(jnp.maximum(l1, l2), l3)
    e1 = jnp.exp2(l1 - mx)
    e2 = jnp.exp2(l2 - mx)
    e3 = jnp.exp2(l3 - mx)
    inv = 1.0 / (e1 + e2 + e3)
    w1 = e1 * inv
    w2 = e2 * inv
    w3 = e3 * inv
    for h in range(B_HEADS):
        sl = slice(h * HEAD_DIM, (h + 1) * HEAD_DIM)
        yb = (w1[:, h:h + 1] * o1_ref[0, :, sl].astype(F32)
              + w2[:, h:h + 1] * o2_ref[0, :, sl].astype(F32)
              + w3[:, h:h + 1] * o3_ref[0, :, sl].astype(F32))
        yb_ref[:, sl] = yb.astype(BF16)
    y = (jnp.dot(ya_ref[0], w_ref[0:A_V, :], preferred_element_type=F32)
         + jnp.dot(yb_ref[...], w_ref[A_V:A_V + B_W, :], preferred_element_type=F32))
    _post_attention(y, x_ref[0], g1_ref[0], lng_ref[...], lnb_ref[...], sc2_ref[0], sh2_ref[0],
                    rwh_ref, rwl_ref, x1_ref, hp_ref, lg_ref, alpha)


def _outproj_c_kernel(o_ref, w_ref, x_ref, g1_ref, lng_ref, lnb_ref, sh2_ref, sc2_ref, rwh_ref, rwl_ref,
                      x1_ref, hp_ref, lg_ref, *, alpha):
    y = jnp.dot(o_ref[0], w_ref[...], preferred_element_type=F32)
    _post_attention(y, x_ref[0], g1_ref[0], lng_ref[...], lnb_ref[...], sc2_ref[0], sh2_ref[0],
                    rwh_ref, rwl_ref, x1_ref, hp_ref, lg_ref, alpha)


def _out_projection(attn_inputs, w_bf16, x, mod, lng, lnb, rw_hi, rw_lo, alpha, mixer):
    B, S, D = x.shape
    E = rw_hi.shape[1]
    tm = min(S, 512)
    row = lambda b, i: (b, i, 0)
    const2 = lambda b, i: (0, 0)
    tail_specs = [
        pl.BlockSpec(w_bf16.shape, const2),
        pl.BlockSpec((1, tm, D), row),
        pl.BlockSpec((1, 1, D), lambda b, i: (b, 0, 2)),
        pl.BlockSpec((1, D), const2),
        pl.BlockSpec((1, D), const2),
        pl.BlockSpec((1, 1, D), lambda b, i: (b, 0, 3)),
        pl.BlockSpec((1, 1, D), lambda b, i: (b, 0, 4)),
        pl.BlockSpec((D, E), const2),
        pl.BlockSpec((D, E), const2),
    ]
    tail_args = (w_bf16, x, mod, lng.reshape(1, D), lnb.reshape(1, D), mod, mod, rw_hi, rw_lo)
    nck = D // 2 // HEAD_DIM
    out_shape = [jax.ShapeDtypeStruct((B, S, D), F32), jax.ShapeDtypeStruct((B, S, nck, HEAD_DIM), U32),
                 jax.ShapeDtypeStruct((B, S, E), F32)]
    out_specs = [pl.BlockSpec((1, tm, D), row), pl.BlockSpec((1, tm, nck, HEAD_DIM), lambda b, i: (b, i, 0, 0)),
                 pl.BlockSpec((1, tm, E), row)]
    if mixer == "ab":
        ya, o1, o2, o3, l1, l2, l3 = attn_inputs
        in_specs = ([pl.BlockSpec((1, tm, A_V), row)] + [pl.BlockSpec((1, tm, B_W), row)] * 3
                    + [pl.BlockSpec((1, tm, HEAD_DIM), row)] * 3 + tail_specs)
        kern = functools.partial(_outproj_ab_kernel, alpha=alpha)
        scratch = [pltpu.VMEM((tm, B_W), BF16)]
        args = (ya, o1, o2, o3, l1, l2, l3) + tail_args
    else:
        (o,) = attn_inputs
        in_specs = [pl.BlockSpec((1, tm, o.shape[2]), row)] + tail_specs
        kern = functools.partial(_outproj_c_kernel, alpha=alpha)
        scratch = []
        args = (o,) + tail_args
    return pl.pallas_call(
        kern,
        grid=(B, S // tm),
        in_specs=in_specs,
        out_specs=out_specs,
        out_shape=out_shape,
        scratch_shapes=scratch,
        compiler_params=_params("parallel", "parallel"),
        name="out_proj_" + mixer,
    )(*args)


def _excl_cumsum_lanes(x, tri):
    n = x.shape[1] // HEAD_DIM
    carry = jnp.zeros((x.shape[0], 1), F32)
    outs = []
    for c in range(n):
        xc = x[:, c * HEAD_DIM:(c + 1) * HEAD_DIM]
        outs.append(jnp.dot(xc.astype(BF16), tri, preferred_element_type=F32) + carry)
        carry = carry + jnp.sum(xc, axis=1, keepdims=True)
    return jnp.concatenate(outs, axis=1)


def _tables_kernel(lg_ref, cnt_ref, dst_ref, gk_ref, *, n_exp, cap):
    lg = lg_ref[0]
    S = lg.shape[1]
    ex = jnp.exp(lg - jnp.max(lg, axis=0, keepdims=True))
    aff = ex / jnp.sum(ex, axis=0, keepdims=True)
    bits = lax.bitcast_convert_type(aff, I32)

    def search(i, cur):
        cand = cur | jnp.left_shift(jnp.int32(1), 30 - i)
        cnt = jnp.sum(jnp.where(bits >= cand, 1.0, 0.0), axis=1, keepdims=True)
        return jnp.where(cnt >= cap, cand, cur)

    thr = lax.fori_loop(0, 31, search, jnp.zeros((n_exp, 1), I32))
    tri = jnp.where(lax.broadcasted_iota(I32, (HEAD_DIM, HEAD_DIM), 0)
                    < lax.broadcasted_iota(I32, (HEAD_DIM, HEAD_DIM), 1), 1.0, 0.0).astype(BF16)
    gt = bits > thr
    eq = jnp.where(bits == thr, 1.0, 0.0)
    need = cap - jnp.sum(jnp.where(gt, 1.0, 0.0), axis=1, keepdims=True)
    sel = jnp.where(gt, 1.0, jnp.where(_excl_cumsum_lanes(eq, tri) < need, eq, 0.0))
    pos = _excl_cumsum_lanes(sel, tri)

    slot = lax.broadcasted_iota(I32, (n_exp, S), 0).astype(F32)
    run = jnp.zeros((1, S), F32)
    dst = jnp.zeros((n_exp, S), F32)
    gk = jnp.zeros((n_exp, S), F32)
    for e in range(n_exp):
        se = sel[e:e + 1, :]
        hit = (slot == run) & (se > 0.0)
        dst = jnp.where(hit, pos[e:e + 1, :] + float(e * cap), dst)
        gk = jnp.where(hit, aff[e:e + 1, :], gk)
        run = run + se
    cnt_ref[0] = run.astype(I32)
    dst_ref[0] = dst.astype(I32)
    gk_ref[0] = gk


def _routing_tables(logits_t, cap):
    B, E, S = logits_t.shape
    kern = functools.partial(_tables_kernel, n_exp=E, cap=cap)
    return pl.pallas_call(
        kern,
        grid=(B,),
        in_specs=[pl.BlockSpec((1, E, S), lambda b: (b, 0, 0))],
        out_specs=[pl.BlockSpec((1, 1, S), lambda b: (b, 0, 0)), pl.BlockSpec((1, E, S), lambda b: (b, 0, 0)),
                   pl.BlockSpec((1, E, S), lambda b: (b, 0, 0))],
        out_shape=[jax.ShapeDtypeStruct((B, 1, S), I32), jax.ShapeDtypeStruct((B, E, S), I32),
                   jax.ShapeDtypeStruct((B, E, S), F32)],
        compiler_params=_params("parallel"),
        name="routing_tables",
    )(logits_t)


def _index_kernel(cnt_ref, dst_ref, gk_ref, off_ref, tok_ref, grow_ref, pos_ref, *, tc, cap, seq):
    tok0 = pl.program_id(0) * seq + pl.program_id(1) * tc
    base = off_ref[0, 0, 0]
    shift = cap.bit_length() - 1 if cap & (cap - 1) == 0 else None

    def per_token(t, n):
        def per_choice(k, n):
            code = dst_ref[0, k, t]
            e = code >> shift if shift is not None else lax.div(code, cap)
            j = code - e * cap
            tok_ref[0, e, j] = tok0 + t
            grow_ref[0, e, j] = gk_ref[0, k, t]
            pos_ref[0, e, j] = base + n
            return n + 1

        return lax.fori_loop(0, cnt_ref[0, 0, t], per_choice, n)

    lax.fori_loop(0, tc, per_token, 0)


def _expert_order_lists(cnt, dst, gk, off, n_exp, cap):
    B, _, S = cnt.shape
    tc = min(S, 512)
    kern = functools.partial(_index_kernel, tc=tc, cap=cap, seq=S)
    smem = functools.partial(pl.BlockSpec, memory_space=pltpu.SMEM)
    out_spec = smem((1, n_exp, cap), lambda b, c: (b, 0, 0))
    return pl.pallas_call(
        kern,
        grid=(B, S // tc),
        in_specs=[smem((1, 1, tc), lambda b, c: (b, 0, c)), smem((1, n_exp, tc), lambda b, c: (b, 0, c)),
                  smem((1, n_exp, tc), lambda b, c: (b, 0, c)), smem((1, 1, tc), lambda b, c: (b, 0, c))],
        out_specs=[out_spec, out_spec, out_spec],
        out_shape=[jax.ShapeDtypeStruct((B, n_exp, cap), I32), jax.ShapeDtypeStruct((B, n_exp, cap), F32),
                   jax.ShapeDtypeStruct((B, n_exp, cap), I32)],
        compiler_params=_params("arbitrary", "arbitrary"),
        name="moe_index",
    )(cnt, dst, gk, off)


def _unpack_rows(xp_ref, xs_ref):
    nck = xp_ref.shape[1]
    half = nck * HEAD_DIM
    for ck in range(nck):
        u = xp_ref[:, ck, :]
        lo = ck * HEAD_DIM
        xs_ref[:, lo:lo + HEAD_DIM] = lax.bitcast_convert_type(u & jnp.uint32(0xFFFF0000), F32).astype(BF16)
        xs_ref[:, half + lo:half + lo + HEAD_DIM] = lax.bitcast_convert_type(u << 16, F32).astype(BF16)


def _pack_pairs(hi, lo):
    hb = lax.bitcast_convert_type(hi.astype(BF16).astype(F32), U32)
    lb = lax.bitcast_convert_type(lo.astype(BF16).astype(F32), U32)
    return hb | (lb >> 16)


ROW_UNROLL = 8


def _ffn_up_kernel(tok_ref, tokn_ref, hp_ref, wg_ref, wu_ref, o_ref, xraw, xs_ref, sem, *, tm, mt, n_tiles):
    tile = pl.program_id(0) * mt + pl.program_id(1)
    par = tile % 2

    def gather(idx_ref, slot):
        def body(i, carry):
            for u in range(ROW_UNROLL):
                r = i * ROW_UNROLL + u
                pltpu.make_async_copy(hp_ref.at[idx_ref[0, 0, r]], xraw.at[slot, r], sem.at[slot]).start()
            return carry

        lax.fori_loop(0, tm // ROW_UNROLL, body, 0)

    @pl.when(pl.program_id(2) == 0)
    def _():
        @pl.when(tile == 0)
        def _():
            gather(tok_ref, 0)

        pltpu.make_async_copy(hp_ref.at[pl.ds(0, tm)], xraw.at[par], sem.at[par]).wait()

        @pl.when(tile + 1 < n_tiles)
        def _():
            gather(tokn_ref, 1 - par)

        _unpack_rows(xraw.at[par], xs_ref)

    x = xs_ref[...]
    g = jnp.dot(x, wg_ref[0].astype(BF16), preferred_element_type=F32)
    u = jnp.dot(x, wu_ref[0].astype(BF16), preferred_element_type=F32)
    o_ref[...] = (g * jax.nn.sigmoid(g) * u).astype(BF16)


def _ffn_up(hp, tok_rows, w_gate, w_up, rows_per_expert):
    E, D, Fh = w_gate.shape
    tm = min(rows_per_expert, 2048)
    mt = rows_per_expert // tm
    n_tiles = E * mt
    tn = min(Fh, 256)
    kern = functools.partial(_ffn_up_kernel, tm=tm, mt=mt, n_tiles=n_tiles)
    smem = functools.partial(pl.BlockSpec, memory_space=pltpu.SMEM)
    tok3 = tok_rows.reshape(n_tiles, 1, tm)
    return pl.pallas_call(
        kern,
        grid=(E, mt, Fh // tn),
        in_specs=[smem((1, 1, tm), lambda e, m, n: (e * mt + m, 0, 0)),
                  smem((1, 1, tm), lambda e, m, n: (jnp.minimum(e * mt + m + 1, n_tiles - 1), 0, 0)),
                  pl.BlockSpec(memory_space=pl.ANY),
                  pl.BlockSpec((1, D, tn), lambda e, m, n: (e, 0, n)),
                  pl.BlockSpec((1, D, tn), lambda e, m, n: (e, 0, n))],
        out_specs=pl.BlockSpec((tm, tn), lambda e, m, n: (e * mt + m, n)),
        out_shape=jax.ShapeDtypeStruct((E * rows_per_expert, Fh), BF16),
        scratch_shapes=[pltpu.VMEM((2, tm, *hp.shape[1:]), U32), pltpu.VMEM((tm, D), BF16),
                        pltpu.SemaphoreType.DMA((2,))],
        compiler_params=_params("arbitrary", "arbitrary", "arbitrary"),
        name="ffn_up",
    )(tok3, tok3, hp, w_gate, w_up)


def _ffn_down_kernel(pos_ref, h_ref, wa_ref, wb_ref, g_ref, yt_ref, ybuf, sem, *, n_steps, cps, tm, mt, n_tiles):
    tile = pl.program_id(0) * mt + pl.program_id(1)
    par = tile % 2
    h = h_ref[...]
    g = g_ref[...]
    ya = jnp.dot(h, wa_ref[0].astype(BF16), preferred_element_type=F32) * g
    yb = jnp.dot(h, wb_ref[0].astype(BF16), preferred_element_type=F32) * g
    packed = _pack_pairs(ya, yb)
    n = pl.program_id(2)
    for k in range(n_steps):
        @pl.when(n == k)
        def _():
            for cc in range(cps):
                ybuf.at[par][:, k * cps + cc, :] = packed[:, cc * HEAD_DIM:(cc + 1) * HEAD_DIM]

    def scatter_done(slot):
        return pltpu.make_async_copy(ybuf.at[slot], yt_ref.at[pl.ds(0, tm)], sem.at[slot])

    @pl.when(n == n_steps - 1)
    def _():
        @pl.when(tile > 0)
        def _():
            scatter_done(1 - par).wait()

        def body(i, carry):
            for u in range(ROW_UNROLL):
                r = i * ROW_UNROLL + u
                pltpu.make_async_copy(ybuf.at[par, r], yt_ref.at[pos_ref[0, 0, r]], sem.at[par]).start()
            return carry

        lax.fori_loop(0, tm // ROW_UNROLL, body, 0)

        @pl.when(tile == n_tiles - 1)
        def _():
            scatter_done(par).wait()


def _ffn_down(hid, w_down, gate_col, pos_rows, rows_per_expert):
    E, Fh, D = w_down.shape
    tm = min(rows_per_expert, 2048)
    mt = rows_per_expert // tm
    n_tiles = E * mt
    half = D // 2
    tnh = min(half, 256)
    n_steps = half // tnh
    nck = half // HEAD_DIM
    kern = functools.partial(_ffn_down_kernel, n_steps=n_steps, cps=tnh // HEAD_DIM, tm=tm, mt=mt, n_tiles=n_tiles)
    return pl.pallas_call(
        kern,
        grid=(E, mt, n_steps),
        in_specs=[pl.BlockSpec((1, 1, tm), lambda e, m, n: (e * mt + m, 0, 0), memory_space=pltpu.SMEM),
                  pl.BlockSpec((tm, Fh), lambda e, m, n: (e * mt + m, 0)),
                  pl.BlockSpec((1, Fh, tnh), lambda e, m, n: (e, 0, n)),
                  pl.BlockSpec((1, Fh, tnh), lambda e, m, n: (e, 0, n + n_steps)),
                  pl.BlockSpec((tm, 1), lambda e, m, n: (e * mt + m, 0))],
        out_specs=pl.BlockSpec(memory_space=pl.ANY),
        out_shape=jax.ShapeDtypeStruct((E * rows_per_expert, nck, HEAD_DIM), U32),
        scratch_shapes=[pltpu.VMEM((2, tm, nck, HEAD_DIM), U32), pltpu.SemaphoreType.DMA((2,))],
        compiler_params=_params("arbitrary", "arbitrary", "arbitrary"),
        name="ffn_down",
    )(pos_rows.reshape(n_tiles, 1, tm), hid, w_down, w_down, gate_col)


def _combine_kernel(offt_ref, x_ref, off_ref, cnt_ref, g2_ref, lng_ref, lnb_ref, yt_ref, o_ref,
                    buf, rows, acc, sem, *, tm, ch, n_total, n_tiles, n_tiles_s, alpha):
    tile = pl.program_id(0) * n_tiles_s + pl.program_id(1)
    o0 = offt_ref[tile]
    o1 = offt_ref[tile + 1]
    n_chunks = jnp.maximum((o1 - o0 + ch - 1) // ch, 1)
    lo_col = off_ref[0]
    hi_col = lo_col + cnt_ref[0]
    lane = lax.broadcasted_iota(I32, (tm, ch), 1)
    acc[...] = jnp.zeros(acc.shape, F32)

    def fetch(want, slot):
        start = jnp.minimum(want, n_total - ch)
        return pltpu.make_async_copy(yt_ref.at[pl.ds(start, ch)], buf.at[slot], sem.at[slot])

    @pl.when(tile == 0)
    def _():
        fetch(o0, 0).start()

    def chunk(c, carry):
        slot = c % 2
        want = o0 + c * ch
        fetch(want, slot).wait()

        @pl.when(c + 1 < n_chunks)
        def _():
            fetch(want + ch, 1 - slot).start()

        _unpack_rows(buf.at[slot], rows)
        row = lane + jnp.minimum(want, n_total - ch)
        own = (row >= lo_col) & (row < hi_col) & (row >= want)
        acc[...] += jnp.dot(jnp.where(own, 1.0, 0.0).astype(BF16), rows[...], preferred_element_type=F32)
        return carry

    lax.fori_loop(0, n_chunks, chunk, 0)

    @pl.when(tile + 1 < n_tiles)
    def _():
        fetch(o1, 0).start()

    z = alpha * x_ref[0] + g2_ref[0] * acc[...]
    mu = jnp.mean(z, axis=-1, keepdims=True)
    zc = z - mu
    var = jnp.mean(zc * zc, axis=-1, keepdims=True)
    o_ref[0] = zc * lax.rsqrt(var + LN_EPS) * lng_ref[...] + lnb_ref[...]


def _combine(x1, mod, lng, lnb, yt, off, cnt, alpha):
    B, S, D = x1.shape
    n_total = yt.shape[0]
    tm = min(S, 256)
    ch = min(n_total, 256)
    n_tiles_s = S // tm
    off_flat = off.reshape(B * S)
    offt = jnp.concatenate([off_flat[::tm], jnp.full((1,), n_total, I32)])
    kern = functools.partial(_combine_kernel, tm=tm, ch=ch, n_total=n_total, n_tiles=B * n_tiles_s,
                             n_tiles_s=n_tiles_s, alpha=alpha)
    grid_spec = pltpu.PrefetchScalarGridSpec(
        num_scalar_prefetch=1,
        grid=(B, n_tiles_s),
        in_specs=[
            pl.BlockSpec((1, tm, D), lambda b, i, o: (b, i, 0)),
            pl.BlockSpec((1, tm, 1), lambda b, i, o: (b, i, 0)),
            pl.BlockSpec((1, tm, 1), lambda b, i, o: (b, i, 0)),
            pl.BlockSpec((1, 1, D), lambda b, i, o: (b, 0, 5)),
            pl.BlockSpec((1, D), lambda b, i, o: (0, 0)),
            pl.BlockSpec((1, D), lambda b, i, o: (0, 0)),
            pl.BlockSpec(memory_space=pl.ANY),
        ],
        out_specs=pl.BlockSpec((1, tm, D), lambda b, i, o: (b, i, 0)),
        scratch_shapes=[pltpu.VMEM((2, ch, *yt.shape[1:]), U32), pltpu.VMEM((ch, D), BF16),
                        pltpu.VMEM((tm, D), F32), pltpu.SemaphoreType.DMA((2,))],
    )
    return pl.pallas_call(
        kern,
        grid_spec=grid_spec,
        out_shape=jax.ShapeDtypeStruct((B, S, D), F32),
        compiler_params=_params("arbitrary", "arbitrary"),
        name="moe_combine",
    )(offt, x1, off.reshape(B, S, 1), cnt.reshape(B, S, 1), mod, lng.reshape(1, D), lnb.reshape(1, D), yt)


def _moe_sublayer(x1, hp, logits, mod, lng, lnb, w_gate, w_up, w_down, alpha):
    B, S, D = x1.shape
    E = logits.shape[-1]
    cap = EC_CAPACITY_FACTOR * S // E
    cnt, dst, gk = _routing_tables(jnp.swapaxes(logits, 1, 2), cap)
    cflat = cnt.reshape(B * S)
    off = (jnp.cumsum(cflat) - cflat).astype(I32).reshape(B, 1, S)
    tok, grow, pos = _expert_order_lists(cnt, dst, gk, off, E, cap)
    expert_major = lambda a: jnp.swapaxes(a, 0, 1).reshape(E * B * cap)
    hid = _ffn_up(hp.reshape(B * S, *hp.shape[2:]), expert_major(tok), w_gate, w_up, B * cap)
    yt = _ffn_down(hid, w_down, expert_major(grow).reshape(E * B * cap, 1), expert_major(pos), B * cap)
    return _combine(x1, mod, lng, lnb, yt, off, cnt, alpha)


def _split_bf16(w):
    hi = w.astype(BF16)
    return hi, (w - hi.astype(F32)).astype(BF16)


def kernel(x, c, ada_w, ada_b, ln_g, ln_b, ab_w_in, ab_w_out, diff_lambda, diff_subln_g, c_w_in, c_w_out,
           c_sink, router_w, w_gate, w_up, w_down):
    B, S, D = x.shape
    depth = ada_w.shape[0]
    alpha = (2.0 * depth) ** 0.25
    qscale = HEAD_DIM ** -0.5 * LOG2E
    mod_all = _modulation(c, ada_w, ada_b)

    ab_scale = np.ones((1, AB_IN), np.float32)
    ab_scale[:, :A_QK] = qscale
    ab_scale[:, 2 * A_QK + A_V:2 * A_QK + A_V + B_W] = qscale
    c_scale = np.ones((1, C_IN), np.float32)
    c_scale[:, :C_QW] = qscale

    for l in range(depth):
        mod = mod_all[l][:, None, :]
        i = l // 2
        rw_hi, rw_lo = _split_bf16(router_w[l])
        if l % 2 == 0:
            proj = _in_projection(x, mod, ab_w_in[i].astype(BF16), jnp.asarray(ab_scale))
            ya = _diff_attention(proj, diff_lambda[i], diff_subln_g[i], l)
            slopes_b = jnp.asarray(_alibi_slopes(B_HEADS) * LOG2E)
            outs = []
            lses = []
            for window, dil in B_BRANCHES:
                n_rows = S // dil
                src = proj.reshape(B, n_rows, dil * AB_IN)
                o, lse = _banded_attention(
                    src, n_rows=n_rows, n_res=dil, src_cols=AB_IN, q_blk=3, k_blk=4, v_blk=5, n_q=B_HEADS,
                    group=1, radius=window // (2 * dil), dist_scale=dil, slopes2=slopes_b, sink2=None,
                    want_lse=True)
                outs.append(o.reshape(B, S, B_W))
                lses.append(lse.reshape(B, S, HEAD_DIM))
            x1, hp, logits = _out_projection((ya, *outs, *lses), ab_w_out[i].astype(BF16), x, mod,
                                             ln_g[l, 0], ln_b[l, 0], rw_hi, rw_lo, alpha, "ab")
        else:
            proj = _in_projection(x, mod, c_w_in[i].astype(BF16), jnp.asarray(c_scale))
            (o,) = _banded_attention(
                proj, n_rows=S, n_res=1, src_cols=C_IN, q_blk=0, k_blk=C_QW // C_KVW, v_blk=C_QW // C_KVW + 1,
                n_q=C_Q_HEADS, group=C_Q_HEADS // C_KV_HEADS, radius=C_RADIUS, dist_scale=1,
                slopes2=jnp.asarray(_alibi_slopes(C_Q_HEADS) * LOG2E), sink2=c_sink[i] * LOG2E, want_lse=False)
            x1, hp, logits = _out_projection((o,), c_w_out[i].astype(BF16), x, mod, ln_g[l, 0], ln_b[l, 0],
                                             rw_hi, rw_lo, alpha, "c")
        x = _moe_sublayer(x1, hp, logits, mod, ln_g[l, 1], ln_b[l, 1], w_gate[l], w_up[l], w_down[l], alpha)
    return x
```

```python
import functools
import math

import numpy as np
import jax
import jax.numpy as jnp
from jax import lax
from jax.experimental import pallas as pl
from jax.experimental.pallas import tpu as pltpu

F32 = jnp.float32
BF16 = jnp.bfloat16
I32 = jnp.int32
U32 = jnp.uint32

HEAD_DIM = 128
A_HEADS = 4
A_VDIM = 2 * HEAD_DIM
B_HEADS = 8
B_BRANCHES = ((128, 1), (512, 4), (2048, 16))
C_Q_HEADS = 16
C_KV_HEADS = 4
C_RADIUS = 128
EC_CAPACITY_FACTOR = 2
LN_EPS = 1e-5
NEG = -1e30
LOG2E = 1.4426950408889634

A_QK = A_HEADS * 2 * HEAD_DIM
A_V = A_HEADS * A_VDIM
B_W = B_HEADS * HEAD_DIM
AB_IN = 2 * A_QK + A_V + 3 * B_W
C_QW = C_Q_HEADS * HEAD_DIM
C_KVW = C_KV_HEADS * HEAD_DIM
C_IN = C_QW + 2 * C_KVW

VMEM_LIMIT_BYTES = 56 * 1024 * 1024


def _params(*sem):
    return pltpu.CompilerParams(dimension_semantics=sem, vmem_limit_bytes=VMEM_LIMIT_BYTES)


def _tile(n, preferred):
    t = min(n, preferred)
    while n % t:
        t //= 2
    return t


def _alibi_slopes(n):
    return np.array([2.0 ** (-8.0 * (i + 1) / n) for i in range(n)], dtype=np.float32)


def _nt_dot(a, b):
    return lax.dot_general(a, b, (((1,), (1,)), ((), ())), preferred_element_type=F32)


def _mod_kernel(c_ref, w_ref, b_ref, o_ref):
    c = c_ref[...]
    cs = (c * jax.nn.sigmoid(c)).astype(BF16)
    o_ref[0] = jnp.dot(cs, w_ref[0].astype(BF16), preferred_element_type=F32) + b_ref[0]


def _modulation(c, ada_w, ada_b):
    L, D, N = ada_w.shape
    B = c.shape[0]
    tn = _tile(N, 1024)
    return pl.pallas_call(
        _mod_kernel,
        grid=(L, N // tn),
        in_specs=[
            pl.BlockSpec((B, D), lambda l, j: (0, 0)),
            pl.BlockSpec((1, D, tn), lambda l, j: (l, 0, j)),
            pl.BlockSpec((1, 1, tn), lambda l, j: (l, 0, j)),
        ],
        out_specs=pl.BlockSpec((1, B, tn), lambda l, j: (l, 0, j)),
        out_shape=jax.ShapeDtypeStruct((L, B, N), F32),
        compiler_params=_params("parallel", "parallel"),
        name="adaln_mod",
    )(c, ada_w, ada_b.reshape(L, 1, N))


def _inproj_kernel(x_ref, sh_ref, sc_ref, w_ref, cs_ref, o_ref, h_ref):
    @pl.when(pl.program_id(2) == 0)
    def _():
        h_ref[...] = (x_ref[0] * (1.0 + sc_ref[0]) + sh_ref[0]).astype(BF16)

    acc = jnp.dot(h_ref[...], w_ref[...], preferred_element_type=F32)
    o_ref[0] = (acc * cs_ref[...]).astype(BF16)


def _in_projection(x, mod, w_bf16, colscale):
    B, S, D = x.shape
    N = w_bf16.shape[1]
    tm = min(S, 1024)
    tn = _tile(N, 1024)
    return pl.pallas_call(
        _inproj_kernel,
        grid=(B, S // tm, N // tn),
        in_specs=[
            pl.BlockSpec((1, tm, D), lambda b, i, j: (b, i, 0)),
            pl.BlockSpec((1, 1, D), lambda b, i, j: (b, 0, 0)),
            pl.BlockSpec((1, 1, D), lambda b, i, j: (b, 0, 1)),
            pl.BlockSpec((D, tn), lambda b, i, j: (0, j)),
            pl.BlockSpec((1, tn), lambda b, i, j: (0, j)),
        ],
        out_specs=pl.BlockSpec((1, tm, tn), lambda b, i, j: (b, i, j)),
        out_shape=jax.ShapeDtypeStruct((B, S, N), BF16),
        scratch_shapes=[pltpu.VMEM((tm, D), BF16)],
        compiler_params=_params("parallel", "parallel", "arbitrary"),
        name="in_proj",
    )(x, mod, mod, w_bf16, colscale)


def _diff_kernel(slope_ref, q_ref, k_ref, v_ref, lam_ref, g_ref, o_ref,
                 m1_ref, l1_ref, a1_ref, m2_ref, l2_ref, a2_ref, *, tq, tk, n_chunks, lam_init):
    h = pl.program_id(1)
    i = pl.program_id(2)
    slope2 = slope_ref[h]
    q = q_ref[0]
    q1 = q[:, :HEAD_DIM]
    q2 = q[:, HEAD_DIM:]
    relf = (lax.broadcasted_iota(I32, (tq, tk), 1) - lax.broadcasted_iota(I32, (tq, tk), 0)).astype(F32)

    m1_ref[...] = jnp.full(m1_ref.shape, NEG, F32)
    m2_ref[...] = jnp.full(m2_ref.shape, NEG, F32)
    l1_ref[...] = jnp.zeros(l1_ref.shape, F32)
    l2_ref[...] = jnp.zeros(l2_ref.shape, F32)
    a1_ref[...] = jnp.zeros(a1_ref.shape, F32)
    a2_ref[...] = jnp.zeros(a2_ref.shape, F32)

    def chunk(c, carry):
        k0 = pl.multiple_of(c * tk, tk)
        kc = k_ref[0, pl.ds(k0, tk), :]
        vc = v_ref[0, pl.ds(k0, tk), :]
        bias = slope2 * jnp.abs(relf + (k0 - i * tq).astype(F32))

        def one(qm, km, m_ref, l_ref, a_ref):
            s = _nt_dot(qm, km) - bias
            m_old = m_ref[...]
            m_new = jnp.maximum(m_old, jnp.max(s, axis=-1, keepdims=True))
            p = jnp.exp2(s - m_new)
            alpha = jnp.exp2(m_old - m_new)
            l_ref[...] = alpha * l_ref[...] + jnp.sum(p, axis=-1, keepdims=True)
            a_ref[...] = alpha * a_ref[...] + jnp.dot(p.astype(BF16), vc, preferred_element_type=F32)
            m_ref[...] = m_new

        one(q1, kc[:, :HEAD_DIM], m1_ref, l1_ref, a1_ref)
        one(q2, kc[:, HEAD_DIM:], m2_ref, l2_ref, a2_ref)
        return carry

    lax.fori_loop(0, n_chunks, chunk, 0)

    lv = lam_ref[...]
    s01 = jnp.sum(lv[0:1, :] * lv[1:2, :], axis=-1, keepdims=True)
    s23 = jnp.sum(lv[2:3, :] * lv[3:4, :], axis=-1, keepdims=True)
    lam = jnp.exp(s01) - jnp.exp(s23) + lam_init
    o = a1_ref[...] / l1_ref[...] - lam * (a2_ref[...] / l2_ref[...])
    ms = jnp.mean(o * o, axis=-1, keepdims=True)
    o = o * lax.rsqrt(ms + LN_EPS) * g_ref[...] * (1.0 - lam_init)
    o_ref[0] = o.astype(BF16)


def _diff_attention(proj, lam_vecs, subln_g, layer_idx):
    B, S, _ = proj.shape
    tq = min(S, 512)
    tk = min(S, 1024)
    lam_init = 0.8 - 0.6 * math.exp(-0.3 * layer_idx)
    slopes2 = jnp.asarray(_alibi_slopes(A_HEADS) * LOG2E)
    nq = A_QK // A_VDIM
    kern = functools.partial(_diff_kernel, tq=tq, tk=tk, n_chunks=S // tk, lam_init=lam_init)
    return pl.pallas_call(
        kern,
        grid=(B, A_HEADS, S // tq),
        in_specs=[
            pl.BlockSpec(memory_space=pltpu.SMEM),
            pl.BlockSpec((1, tq, A_VDIM), lambda b, h, i: (b, i, h)),
            pl.BlockSpec((1, S, A_VDIM), lambda b, h, i: (b, 0, nq + h)),
            pl.BlockSpec((1, S, A_VDIM), lambda b, h, i: (b, 0, 2 * nq + h)),
            pl.BlockSpec((4, HEAD_DIM), lambda b, h, i: (0, 0)),
            pl.BlockSpec((1, A_VDIM), lambda b, h, i: (0, 0)),
        ],
        out_specs=pl.BlockSpec((1, tq, A_VDIM), lambda b, h, i: (b, i, h)),
        out_shape=jax.ShapeDtypeStruct((B, S, A_V), BF16),
        scratch_shapes=[
            pltpu.VMEM((tq, 1), F32), pltpu.VMEM((tq, 1), F32), pltpu.VMEM((tq, A_VDIM), F32),
            pltpu.VMEM((tq, 1), F32), pltpu.VMEM((tq, 1), F32), pltpu.VMEM((tq, A_VDIM), F32),
        ],
        compiler_params=_params("parallel", "parallel", "parallel"),
        name="diff_attn",
    )(slopes2, proj, proj, proj, lam_vecs, subln_g.reshape(1, A_VDIM))


def _banded_kernel(slope_ref, sink_ref, q_ref, kp_ref, kc_ref, kn_ref, vp_ref, vc_ref, vn_ref, *rest,
                   tq, radius, n_q, group, n_rows, dist_scale, use_sink, want_lse):
    if want_lse:
        o_ref, lse_ref, kwin, vwin = rest
    else:
        o_ref, kwin, vwin = rest
        lse_ref = None
    t = pl.program_id(2)
    w = tq + 2 * radius
    kwin[0:radius, :] = kp_ref[0]
    kwin[radius:radius + tq, :] = kc_ref[0]
    kwin[radius + tq:w, :] = kn_ref[0]
    vwin[0:radius, :] = vp_ref[0]
    vwin[radius:radius + tq, :] = vc_ref[0]
    vwin[radius + tq:w, :] = vn_ref[0]

    ii = lax.broadcasted_iota(I32, (tq, w), 0)
    jj = lax.broadcasted_iota(I32, (tq, w), 1)
    rel = jnp.abs(jj - radius - ii)
    kpos = t * tq - radius + jj
    valid = (rel <= radius) & (kpos >= 0) & (kpos < n_rows)
    dist = rel.astype(F32) * float(dist_scale)
    lane = lax.broadcasted_iota(I32, (tq, HEAD_DIM), 1)
    lse_tile = jnp.zeros((tq, HEAD_DIM), F32)

    for h in range(n_q):
        hk = h // group
        qh = q_ref[0, :, h * HEAD_DIM:(h + 1) * HEAD_DIM]
        kh = kwin[:, hk * HEAD_DIM:(hk + 1) * HEAD_DIM]
        vh = vwin[:, hk * HEAD_DIM:(hk + 1) * HEAD_DIM]
        s = jnp.where(valid, _nt_dot(qh, kh) - slope_ref[h] * dist, NEG)
        m = jnp.max(s, axis=-1, keepdims=True)
        if use_sink:
            m = jnp.maximum(m, sink_ref[h])
        p = jnp.exp2(s - m)
        den = jnp.sum(p, axis=-1, keepdims=True)
        if use_sink:
            den = den + jnp.exp2(sink_ref[h] - m)
        o = jnp.dot(p.astype(BF16), vh, preferred_element_type=F32) / den
        o_ref[0, :, h * HEAD_DIM:(h + 1) * HEAD_DIM] = o.astype(BF16)
        if want_lse:
            lse_tile = jnp.where(lane == h, m + jnp.log2(den), lse_tile)
    if want_lse:
        lse_ref[0] = lse_tile


def _banded_attention(src, *, n_rows, n_res, src_cols, q_blk, k_blk, v_blk, n_q, group, radius,
                      dist_scale, slopes2, sink2, want_lse):
    B = src.shape[0]
    qw = n_q * HEAD_DIM
    kw = (n_q // group) * HEAD_DIM
    tq = min(n_rows, 256)
    nt = n_rows // tq
    per_t = tq // radius
    last_halo = n_rows // radius - 1
    qpg = src_cols // qw
    kpg = src_cols // kw

    def q_map(b, r, t):
        return (b, t, r * qpg + q_blk)

    def cur_map(blk):
        return lambda b, r, t: (b, t, r * kpg + blk)

    def prev_map(blk):
        return lambda b, r, t: (b, jnp.maximum(t * per_t - 1, 0), r * kpg + blk)

    def next_map(blk):
        return lambda b, r, t: (b, jnp.minimum((t + 1) * per_t, last_halo), r * kpg + blk)

    out_shape = [jax.ShapeDtypeStruct((B, n_rows, n_res * qw), BF16)]
    out_specs = [pl.BlockSpec((1, tq, qw), lambda b, r, t: (b, t, r))]
    if want_lse:
        out_shape.append(jax.ShapeDtypeStruct((B, n_rows, n_res * HEAD_DIM), F32))
        out_specs.append(pl.BlockSpec((1, tq, HEAD_DIM), lambda b, r, t: (b, t, r)))
    kern = functools.partial(_banded_kernel, tq=tq, radius=radius, n_q=n_q, group=group, n_rows=n_rows,
                             dist_scale=dist_scale, use_sink=sink2 is not None, want_lse=want_lse)
    if sink2 is None:
        sink2 = jnp.zeros((n_q,), F32)
    return pl.pallas_call(
        kern,
        grid=(B, n_res, nt),
        in_specs=[
            pl.BlockSpec(memory_space=pltpu.SMEM),
            pl.BlockSpec(memory_space=pltpu.SMEM),
            pl.BlockSpec((1, tq, qw), q_map),
            pl.BlockSpec((1, radius, kw), prev_map(k_blk)),
            pl.BlockSpec((1, tq, kw), cur_map(k_blk)),
            pl.BlockSpec((1, radius, kw), next_map(k_blk)),
            pl.BlockSpec((1, radius, kw), prev_map(v_blk)),
            pl.BlockSpec((1, tq, kw), cur_map(v_blk)),
            pl.BlockSpec((1, radius, kw), next_map(v_blk)),
        ],
        out_specs=out_specs,
        out_shape=out_shape,
        scratch_shapes=[pltpu.VMEM((tq + 2 * radius, kw), BF16), pltpu.VMEM((tq + 2 * radius, kw), BF16)],
        compiler_params=_params("parallel", "parallel", "parallel"),
        name="banded_attn",
    )(slopes2, sink2, src, src, src, src, src, src, src)


def _post_attention(y, x, g1, lng, lnb, sc2, sh2, rwh_ref, rwl_ref, x1_ref, hp_ref, lg_ref, alpha):
    z = alpha * x + g1 * y
    mu = jnp.mean(z, axis=-1, keepdims=True)
    zc = z - mu
    var = jnp.mean(zc * zc, axis=-1, keepdims=True)
    x1 = zc * lax.rsqrt(var + LN_EPS) * lng + lnb
    x1_ref[0] = x1
    h2 = x1 * (1.0 + sc2) + sh2
    hb = h2.astype(BF16)
    hf = hb.astype(F32)
    bits = lax.bitcast_convert_type(hf, U32)
    half = bits.shape[1] // 2
    packed = bits[:, :half] | (bits[:, half:] >> 16)
    nck = half // HEAD_DIM
    for ck in range(nck):
        hp_ref[0, pl.ds(ck, bits.shape[0], stride=nck), :] = packed[:, ck * HEAD_DIM:(ck + 1) * HEAD_DIM]
    lo = (h2 - hf).astype(BF16)
    rwh = rwh_ref[...]
    lg_ref[0] = (jnp.dot(hb, rwh, preferred_element_type=F32)
                 + jnp.dot(hb, rwl_ref[...], preferred_element_type=F32)
                 + jnp.dot(lo, rwh, preferred_element_type=F32))


def _outproj_ab_kernel(ya_ref, o1_ref, o2_ref, o3_ref, l1_ref, l2_ref, l3_ref, w_ref, x_ref, g1_ref,
                       lng_ref, lnb_ref, sh2_ref, sc2_ref, rwh_ref, rwl_ref, x1_ref, hp_ref, lg_ref,
                       yb_ref, *, alpha):
    l1 = l1_ref[0]
    l2 = l2_ref[0]
    l3 = l3_ref[0]
    mx = jnp.maximum(jnp.maximum(l1, l2), l3)
    e1 = jnp.exp2(l1 - mx)
    e2 = jnp.exp2(l2 - mx)
    e3 = jnp.exp2(l3 - mx)
    inv = 1.0 / (e1 + e2 + e3)
    w1 = e1 * inv
    w2 = e2 * inv
    w3 = e3 * inv
    for h in range(B_HEADS):
        sl = slice(h * HEAD_DIM, (h + 1) * HEAD_DIM)
        yb = (w1[:, h:h + 1] * o1_ref[0, :, sl].astype(F32)
              + w2[:, h:h + 1] * o2_ref[0, :, sl].astype(F32)
              + w3[:, h:h + 1] * o3_ref[0, :, sl].astype(F32))
        yb_ref[:, sl] = yb.astype(BF16)
    y = (jnp.dot(ya_ref[0], w_ref[0:A_V, :], preferred_element_type=F32)
         + jnp.dot(yb_ref[...], w_ref[A_V:A_V + B_W, :], preferred_element_type=F32))
    _post_attention(y, x_ref[0], g1_ref[0], lng_ref[...], lnb_ref[...], sc2_ref[0], sh2_ref[0],
                    rwh_ref, rwl_ref, x1_ref, hp_ref, lg_ref, alpha)


def _outproj_c_kernel(o_ref, w_ref, x_ref, g1_ref, lng_ref, lnb_ref, sh2_ref, sc2_ref, rwh_ref, rwl_ref,
                      x1_ref, hp_ref, lg_ref, *, alpha):
    y = jnp.dot(o_ref[0], w_ref[...], preferred_element_type=F32)
    _post_attention(y, x_ref[0], g1_ref[0], lng_ref[...], lnb_ref[...], sc2_ref[0], sh2_ref[0],
                    rwh_ref, rwl_ref, x1_ref, hp_ref, lg_ref, alpha)


def _out_projection(attn_inputs, w_bf16, x, mod, lng, lnb, rw_hi, rw_lo, alpha, mixer):
    B, S, D = x.shape
    E = rw_hi.shape[1]
    tm = min(S, 512)
    row = lambda b, i: (b, i, 0)
    const2 = lambda b, i: (0, 0)
    tail_specs = [
        pl.BlockSpec(w_bf16.shape, const2),
        pl.BlockSpec((1, tm, D), row),
        pl.BlockSpec((1, 1, D), lambda b, i: (b, 0, 2)),
        pl.BlockSpec((1, D), const2),
        pl.BlockSpec((1, D), const2),
        pl.BlockSpec((1, 1, D), lambda b, i: (b, 0, 3)),
        pl.BlockSpec((1, 1, D), lambda b, i: (b, 0, 4)),
        pl.BlockSpec((D, E), const2),
        pl.BlockSpec((D, E), const2),
    ]
    tail_args = (w_bf16, x, mod, lng.reshape(1, D), lnb.reshape(1, D), mod, mod, rw_hi, rw_lo)
    nck = D // 2 // HEAD_DIM
    out_shape = [jax.ShapeDtypeStruct((B, S, D), F32), jax.ShapeDtypeStruct((B, S * nck, HEAD_DIM), U32),
                 jax.ShapeDtypeStruct((B, S, E), F32)]
    out_specs = [pl.BlockSpec((1, tm, D), row), pl.BlockSpec((1, tm * nck, HEAD_DIM), row),
                 pl.BlockSpec((1, tm, E), row)]
    if mixer == "ab":
        ya, o1, o2, o3, l1, l2, l3 = attn_inputs
        in_specs = ([pl.BlockSpec((1, tm, A_V), row)] + [pl.BlockSpec((1, tm, B_W), row)] * 3
                    + [pl.BlockSpec((1, tm, HEAD_DIM), row)] * 3 + tail_specs)
        kern = functools.partial(_outproj_ab_kernel, alpha=alpha)
        scratch = [pltpu.VMEM((tm, B_W), BF16)]
        args = (ya, o1, o2, o3, l1, l2, l3) + tail_args
    else:
        (o,) = attn_inputs
        in_specs = [pl.BlockSpec((1, tm, o.shape[2]), row)] + tail_specs
        kern = functools.partial(_outproj_c_kernel, alpha=alpha)
        scratch = []
        args = (o,) + tail_args
    return pl.pallas_call(
        kern,
        grid=(B, S // tm),
        in_specs=in_specs,
        out_specs=out_specs,
        out_shape=out_shape,
        scratch_shapes=scratch,
        compiler_params=_params("parallel", "parallel"),
        name="out_proj_" + mixer,
    )(*args)


def _excl_cumsum_lanes(x, tri):
    n = x.shape[1] // HEAD_DIM
    carry = jnp.zeros((x.shape[0], 1), F32)
    outs = []
    for c in range(n):
        xc = x[:, c * HEAD_DIM:(c + 1) * HEAD_DIM]
        outs.append(jnp.dot(xc.astype(BF16), tri, preferred_element_type=F32) + carry)
        carry = carry + jnp.sum(xc, axis=1, keepdims=True)
    return jnp.concatenate(outs, axis=1)


def _tables_kernel(lg_ref, cnt_ref, dst_ref, gk_ref, *, n_exp, cap):
    lg = lg_ref[0]
    S = lg.shape[1]
    ex = jnp.exp(lg - jnp.max(lg, axis=0, keepdims=True))
    aff = ex / jnp.sum(ex, axis=0, keepdims=True)
    bits = lax.bitcast_convert_type(aff, I32)

    def search(i, cur):
        cand = cur | jnp.left_shift(jnp.int32(1), 30 - i)
        cnt = jnp.sum(jnp.where(bits >= cand, 1.0, 0.0), axis=1, keepdims=True)
        return jnp.where(cnt >= cap, cand, cur)

    thr = lax.fori_loop(0, 31, search, jnp.zeros((n_exp, 1), I32))
    tri = jnp.where(lax.broadcasted_iota(I32, (HEAD_DIM, HEAD_DIM), 0)
                    < lax.broadcasted_iota(I32, (HEAD_DIM, HEAD_DIM), 1), 1.0, 0.0).astype(BF16)
    gt = bits > thr
    eq = jnp.where(bits == thr, 1.0, 0.0)
    need = cap - jnp.sum(jnp.where(gt, 1.0, 0.0), axis=1, keepdims=True)
    sel = jnp.where(gt, 1.0, jnp.where(_excl_cumsum_lanes(eq, tri) < need, eq, 0.0))
    pos = _excl_cumsum_lanes(sel, tri)

    slot = lax.broadcasted_iota(I32, (n_exp, S), 0).astype(F32)
    run = jnp.zeros((1, S), F32)
    dst = jnp.zeros((n_exp, S), F32)
    gk = jnp.zeros((n_exp, S), F32)
    for e in range(n_exp):
        se = sel[e:e + 1, :]
        hit = (slot == run) & (se > 0.0)
        dst = jnp.where(hit, pos[e:e + 1, :] + float(e * cap), dst)
        gk = jnp.where(hit, aff[e:e + 1, :], gk)
        run = run + se
    cnt_ref[0] = run.astype(I32)
    dst_ref[0] = dst.astype(I32)
    gk_ref[0] = gk


def _routing_tables(logits_t, cap):
    B, E, S = logits_t.shape
    kern = functools.partial(_tables_kernel, n_exp=E, cap=cap)
    return pl.pallas_call(
        kern,
        grid=(B,),
        in_specs=[pl.BlockSpec((1, E, S), lambda b: (b, 0, 0))],
        out_specs=[pl.BlockSpec((1, 1, S), lambda b: (b, 0, 0)), pl.BlockSpec((1, E, S), lambda b: (b, 0, 0)),
                   pl.BlockSpec((1, E, S), lambda b: (b, 0, 0))],
        out_shape=[jax.ShapeDtypeStruct((B, 1, S), I32), jax.ShapeDtypeStruct((B, E, S), I32),
                   jax.ShapeDtypeStruct((B, E, S), F32)],
        compiler_params=_params("parallel"),
        name="routing_tables",
    )(logits_t)


def _index_kernel(cnt_ref, dst_ref, gk_ref, off_ref, tok_ref, grow_ref, pos_ref, *, tc, n_chunks):
    tok0 = (pl.program_id(0) * n_chunks + pl.program_id(1)) * tc
    base = off_ref[0]

    def per_token(t, n):
        def per_choice(k, n):
            code = dst_ref[k * tc + t]
            tok_ref[code] = tok0 + t
            grow_ref[code] = gk_ref[k * tc + t]
            pos_ref[code] = base + n
            return n + 1

        return lax.fori_loop(0, cnt_ref[t], per_choice, n)

    lax.fori_loop(0, tc, per_token, 0)


def _expert_order_lists(cnt, dst, gk, off, n_exp, cap):
    B, _, S = cnt.shape
    tc = min(S, 1024)
    n_chunks = S // tc
    by_chunk = lambda a: a.reshape(B, n_exp, n_chunks, tc).swapaxes(1, 2).reshape(B * S * n_exp)
    kern = functools.partial(_index_kernel, tc=tc, n_chunks=n_chunks)
    smem = functools.partial(pl.BlockSpec, memory_space=pltpu.SMEM)
    per_chunk = lambda b, c: (b * n_chunks + c,)
    out_spec = smem((n_exp * cap,), lambda b, c: (b,))
    tok, grow, pos = pl.pallas_call(
        kern,
        grid=(B, n_chunks),
        in_specs=[smem((tc,), per_chunk), smem((n_exp * tc,), per_chunk), smem((n_exp * tc,), per_chunk),
                  smem((tc,), per_chunk)],
        out_specs=[out_spec, out_spec, out_spec],
        out_shape=[jax.ShapeDtypeStruct((B * n_exp * cap,), I32), jax.ShapeDtypeStruct((B * n_exp * cap,), F32),
                   jax.ShapeDtypeStruct((B * n_exp * cap,), I32)],
        compiler_params=_params("arbitrary", "arbitrary"),
        name="moe_index",
    )(cnt.reshape(B * S), by_chunk(dst), by_chunk(gk), off.reshape(B * S))
    expert_major = lambda a: a.reshape(B, n_exp, cap).swapaxes(0, 1).reshape(n_exp * B * cap)
    return expert_major(tok), expert_major(grow), expert_major(pos)


def _token_rows(t, nck):
    start = t * nck
    return pl.ds(pl.multiple_of(start, nck) if nck > 1 else start, nck)


def _unpack_rows(xp_ref, first, xs_ref):
    rows, d = xs_ref.shape
    half = d // 2
    nck = half // HEAD_DIM
    for ck in range(nck):
        u = xp_ref[pl.ds(first * nck + ck, rows, stride=nck), :]
        lo = ck * HEAD_DIM
        xs_ref[:, lo:lo + HEAD_DIM] = lax.bitcast_convert_type(u & jnp.uint32(0xFFFF0000), F32).astype(BF16)
        xs_ref[:, half + lo:half + lo + HEAD_DIM] = lax.bitcast_convert_type(u << 16, F32).astype(BF16)


def _pack_pairs(hi, lo):
    hb = lax.bitcast_convert_type(hi.astype(BF16).astype(F32), U32)
    lb = lax.bitcast_convert_type(lo.astype(BF16).astype(F32), U32)
    return hb | (lb >> 16)


ROW_UNROLL = 8


def _ffn_up_kernel(tok_ref, tokn_ref, hp_ref, wg_ref, wu_ref, o_ref, xraw, xs_ref, sem, *, tm, mt, n_tiles, nck):
    tile = pl.program_id(0) * mt + pl.program_id(1)
    par = tile % 2

    def gather(idx_ref, slot):
        def body(i, carry):
            for u in range(ROW_UNROLL):
                r = i * ROW_UNROLL + u
                pltpu.make_async_copy(hp_ref.at[_token_rows(idx_ref[r], nck)],
                                      xraw.at[_token_rows(slot * tm + r, nck)], sem.at[slot]).start()
            return carry

        lax.fori_loop(0, tm // ROW_UNROLL, body, 0)

    @pl.when(pl.program_id(2) == 0)
    def _():
        @pl.when(tile == 0)
        def _():
            gather(tok_ref, 0)

        pltpu.make_async_copy(hp_ref.at[pl.ds(0, tm * nck)], xraw.at[pl.ds(par * tm * nck, tm * nck)],
                              sem.at[par]).wait()

        @pl.when(tile + 1 < n_tiles)
        def _():
            gather(tokn_ref, 1 - par)

        _unpack_rows(xraw, par * tm, xs_ref)

    x = xs_ref[...]
    g = jnp.dot(x, wg_ref[0].astype(BF16), preferred_element_type=F32)
    u = jnp.dot(x, wu_ref[0].astype(BF16), preferred_element_type=F32)
    o_ref[...] = (g * jax.nn.sigmoid(g) * u).astype(BF16)


def _ffn_up(hp, tok_rows, w_gate, w_up, rows_per_expert):
    E, D, Fh = w_gate.shape
    tm = min(rows_per_expert, 2048)
    mt = rows_per_expert // tm
    n_tiles = E * mt
    tn = min(Fh, 256)
    nck = D // 2 // HEAD_DIM
    kern = functools.partial(_ffn_up_kernel, tm=tm, mt=mt, n_tiles=n_tiles, nck=nck)
    smem = functools.partial(pl.BlockSpec, memory_space=pltpu.SMEM)
    return pl.pallas_call(
        kern,
        grid=(E, mt, Fh // tn),
        in_specs=[smem((tm,), lambda e, m, n: (e * mt + m,)),
                  smem((tm,), lambda e, m, n: (jnp.minimum(e * mt + m + 1, n_tiles - 1),)),
                  pl.BlockSpec(memory_space=pl.ANY),
                  pl.BlockSpec((1, D, tn), lambda e, m, n: (e, 0, n)),
                  pl.BlockSpec((1, D, tn), lambda e, m, n: (e, 0, n))],
        out_specs=pl.BlockSpec((tm, tn), lambda e, m, n: (e * mt + m, n)),
        out_shape=jax.ShapeDtypeStruct((E * rows_per_expert, Fh), BF16),
        scratch_shapes=[pltpu.VMEM((2 * tm * nck, HEAD_DIM), U32), pltpu.VMEM((tm, D), BF16),
                        pltpu.SemaphoreType.DMA((2,))],
        compiler_params=_params("arbitrary", "arbitrary", "arbitrary"),
        name="ffn_up",
    )(tok_rows, tok_rows, hp, w_gate, w_up)


def _ffn_down_kernel(pos_ref, h_ref, wa_ref, wb_ref, g_ref, yt_ref, ybuf, sem, *, n_steps, cps, tm, mt, n_tiles,
                     nck):
    tile = pl.program_id(0) * mt + pl.program_id(1)
    par = tile % 2
    h = h_ref[...]
    g = g_ref[...]
    ya = jnp.dot(h, wa_ref[0].astype(BF16), preferred_element_type=F32) * g
    yb = jnp.dot(h, wb_ref[0].astype(BF16), preferred_element_type=F32) * g
    packed = _pack_pairs(ya, yb)
    n = pl.program_id(2)
    for k in range(n_steps):
        @pl.when(n == k)
        def _():
            for cc in range(cps):
                ybuf[pl.ds(par * tm * nck + k * cps + cc, tm, stride=nck), :] = (
                    packed[:, cc * HEAD_DIM:(cc + 1) * HEAD_DIM])

    def scatter_done(slot):
        return pltpu.make_async_copy(ybuf.at[pl.ds(slot * tm * nck, tm * nck)], yt_ref.at[pl.ds(0, tm * nck)],
                                     sem.at[slot])

    @pl.when(n == n_steps - 1)
    def _():
        @pl.when(tile > 0)
        def _():
            scatter_done(1 - par).wait()

        def body(i, carry):
            for u in range(ROW_UNROLL):
                r = i * ROW_UNROLL + u
                pltpu.make_async_copy(ybuf.at[_token_rows(par * tm + r, nck)],
                                      yt_ref.at[_token_rows(pos_ref[r], nck)], sem.at[par]).start()
            return carry

        lax.fori_loop(0, tm // ROW_UNROLL, body, 0)

        @pl.when(tile == n_tiles - 1)
        def _():
            scatter_done(par).wait()


def _ffn_down(hid, w_down, gate_col, pos_rows, rows_per_expert):
    E, Fh, D = w_down.shape
    tm = min(rows_per_expert, 2048)
    mt = rows_per_expert // tm
    n_tiles = E * mt
    half = D // 2
    tnh = min(half, 256)
    n_steps = half // tnh
    nck = half // HEAD_DIM
    kern = functools.partial(_ffn_down_kernel, n_steps=n_steps, cps=tnh // HEAD_DIM, tm=tm, mt=mt, n_tiles=n_tiles,
                             nck=nck)
    return pl.pallas_call(
        kern,
        grid=(E, mt, n_steps),
        in_specs=[pl.BlockSpec((tm,), lambda e, m, n: (e * mt + m,), memory_space=pltpu.SMEM),
                  pl.BlockSpec((tm, Fh), lambda e, m, n: (e * mt + m, 0)),
                  pl.BlockSpec((1, Fh, tnh), lambda e, m, n: (e, 0, n)),
                  pl.BlockSpec((1, Fh, tnh), lambda e, m, n: (e, 0, n + n_steps)),
                  pl.BlockSpec((tm, 1), lambda e, m, n: (e * mt + m, 0))],
        out_specs=pl.BlockSpec(memory_space=pl.ANY),
        out_shape=jax.ShapeDtypeStruct((E * rows_per_expert * nck, HEAD_DIM), U32),
        scratch_shapes=[pltpu.VMEM((2 * tm * nck, HEAD_DIM), U32), pltpu.SemaphoreType.DMA((2,))],
        compiler_params=_params("arbitrary", "arbitrary", "arbitrary"),
        name="ffn_down",
    )(pos_rows, hid, w_down, w_down, gate_col)


def _combine_kernel(offt_ref, x_ref, off_ref, cnt_ref, g2_ref, lng_ref, lnb_ref, yt_ref, o_ref,
                    buf, rows, acc, sem, *, tm, ch, n_total, n_tiles, n_tiles_s, alpha, nck):
    tile = pl.program_id(0) * n_tiles_s + pl.program_id(1)
    o0 = offt_ref[tile]
    o1 = offt_ref[tile + 1]
    n_chunks = jnp.maximum((o1 - o0 + ch - 1) // ch, 1)
    lo_col = off_ref[0]
    hi_col = lo_col + cnt_ref[0]
    lane = lax.broadcasted_iota(I32, (tm, ch), 1)
    acc[...] = jnp.zeros(acc.shape, F32)

    def fetch(want, slot):
        start = jnp.minimum(want, n_total - ch) * nck
        if nck > 1:
            start = pl.multiple_of(start, nck)
        return pltpu.make_async_copy(yt_ref.at[pl.ds(start, ch * nck)], buf.at[pl.ds(slot * ch * nck, ch * nck)],
                                     sem.at[slot])

    @pl.when(tile == 0)
    def _():
        fetch(o0, 0).start()

    def chunk(c, carry):
        slot = c % 2
        want = o0 + c * ch
        fetch(want, slot).wait()

        @pl.when(c + 1 < n_chunks)
        def _():
            fetch(want + ch, 1 - slot).start()

        _unpack_rows(buf, slot * ch, rows)
        row = lane + jnp.minimum(want, n_total - ch)
        own = (row >= lo_col) & (row < hi_col) & (row >= want)
        acc[...] += jnp.dot(jnp.where(own, 1.0, 0.0).astype(BF16), rows[...], preferred_element_type=F32)
        return carry

    lax.fori_loop(0, n_chunks, chunk, 0)

    @pl.when(tile + 1 < n_tiles)
    def _():
        fetch(o1, 0).start()

    z = alpha * x_ref[0] + g2_ref[0] * acc[...]
    mu = jnp.mean(z, axis=-1, keepdims=True)
    zc = z - mu
    var = jnp.mean(zc * zc, axis=-1, keepdims=True)
    o_ref[0] = zc * lax.rsqrt(var + LN_EPS) * lng_ref[...] + lnb_ref[...]


def _combine(x1, mod, lng, lnb, yt, off, cnt, alpha):
    B, S, D = x1.shape
    nck = D // 2 // HEAD_DIM
    n_total = yt.shape[0] // nck
    tm = min(S, 256)
    ch = min(n_total, 256)
    n_tiles_s = S // tm
    off_flat = off.reshape(B * S)
    offt = jnp.concatenate([off_flat[::tm], jnp.full((1,), n_total, I32)])
    kern = functools.partial(_combine_kernel, tm=tm, ch=ch, n_total=n_total, n_tiles=B * n_tiles_s,
                             n_tiles_s=n_tiles_s, alpha=alpha, nck=nck)
    grid_spec = pltpu.PrefetchScalarGridSpec(
        num_scalar_prefetch=1,
        grid=(B, n_tiles_s),
        in_specs=[
            pl.BlockSpec((1, tm, D), lambda b, i, o: (b, i, 0)),
            pl.BlockSpec((1, tm, 1), lambda b, i, o: (b, i, 0)),
            pl.BlockSpec((1, tm, 1), lambda b, i, o: (b, i, 0)),
            pl.BlockSpec((1, 1, D), lambda b, i, o: (b, 0, 5)),
            pl.BlockSpec((1, D), lambda b, i, o: (0, 0)),
            pl.BlockSpec((1, D), lambda b, i, o: (0, 0)),
            pl.BlockSpec(memory_space=pl.ANY),
        ],
        out_specs=pl.BlockSpec((1, tm, D), lambda b, i, o: (b, i, 0)),
        scratch_shapes=[pltpu.VMEM((2 * ch * nck, HEAD_DIM), U32), pltpu.VMEM((ch, D), BF16),
                        pltpu.VMEM((tm, D), F32), pltpu.SemaphoreType.DMA((2,))],
    )
    return pl.pallas_call(
        kern,
        grid_spec=grid_spec,
        out_shape=jax.ShapeDtypeStruct((B, S, D), F32),
        compiler_params=_params("arbitrary", "arbitrary"),
        name="moe_combine",
    )(offt, x1, off.reshape(B, S, 1), cnt.reshape(B, S, 1), mod, lng.reshape(1, D), lnb.reshape(1, D), yt)


def _moe_sublayer(x1, hp, logits, mod, lng, lnb, w_gate, w_up, w_down, alpha):
    B, S, D = x1.shape
    E = logits.shape[-1]
    cap = EC_CAPACITY_FACTOR * S // E
    cnt, dst, gk = _routing_tables(jnp.swapaxes(logits, 1, 2), cap)
    cflat = cnt.reshape(B * S)
    off = (jnp.cumsum(cflat) - cflat).astype(I32).reshape(B, 1, S)
    tok, grow, pos = _expert_order_lists(cnt, dst, gk, off, E, cap)
    hid = _ffn_up(hp.reshape(-1, HEAD_DIM), tok, w_gate, w_up, B * cap)
    yt = _ffn_down(hid, w_down, grow.reshape(E * B * cap, 1), pos, B * cap)
    return _combine(x1, mod, lng, lnb, yt, off, cnt, alpha)


def _split_bf16(w):
    hi = w.astype(BF16)
    return hi, (w - hi.astype(F32)).astype(BF16)


def kernel(x, c, ada_w, ada_b, ln_g, ln_b, ab_w_in, ab_w_out, diff_lambda, diff_subln_g, c_w_in, c_w_out,
           c_sink, router_w, w_gate, w_up, w_down):
    B, S, D = x.shape
    depth = ada_w.shape[0]
    alpha = (2.0 * depth) ** 0.25
    qscale = HEAD_DIM ** -0.5 * LOG2E
    mod_all = _modulation(c, ada_w, ada_b)

    ab_scale = np.ones((1, AB_IN), np.float32)
    ab_scale[:, :A_QK] = qscale
    ab_scale[:, 2 * A_QK + A_V:2 * A_QK + A_V + B_W] = qscale
    c_scale = np.ones((1, C_IN), np.float32)
    c_scale[:, :C_QW] = qscale

    for l in range(depth):
        mod = mod_all[l][:, None, :]
        i = l // 2
        rw_hi, rw_lo = _split_bf16(router_w[l])
        if l % 2 == 0:
            proj = _in_projection(x, mod, ab_w_in[i].astype(BF16), jnp.asarray(ab_scale))
            ya = _diff_attention(proj, diff_lambda[i], diff_subln_g[i], l)
            slopes_b = jnp.asarray(_alibi_slopes(B_HEADS) * LOG2E)
            outs = []
            lses = []
            for window, dil in B_BRANCHES:
                n_rows = S // dil
                src = proj.reshape(B, n_rows, dil * AB_IN)
                o, lse = _banded_attention(
                    src, n_rows=n_rows, n_res=dil, src_cols=AB_IN, q_blk=3, k_blk=4, v_blk=5, n_q=B_HEADS,
                    group=1, radius=window // (2 * dil), dist_scale=dil, slopes2=slopes_b, sink2=None,
                    want_lse=True)
                outs.append(o.reshape(B, S, B_W))
                lses.append(lse.reshape(B, S, HEAD_DIM))
            x1, hp, logits = _out_projection((ya, *outs, *lses), ab_w_out[i].astype(BF16), x, mod,
                                             ln_g[l, 0], ln_b[l, 0], rw_hi, rw_lo, alpha, "ab")
        else:
            proj = _in_projection(x, mod, c_w_in[i].astype(BF16), jnp.asarray(c_scale))
            (o,) = _banded_attention(
                proj, n_rows=S, n_res=1, src_cols=C_IN, q_blk=0, k_blk=C_QW // C_KVW, v_blk=C_QW // C_KVW + 1,
                n_q=C_Q_HEADS, group=C_Q_HEADS // C_KV_HEADS, radius=C_RADIUS, dist_scale=1,
                slopes2=jnp.asarray(_alibi_slopes(C_Q_HEADS) * LOG2E), sink2=c_sink[i] * LOG2E, want_lse=False)
            x1, hp, logits = _out_projection((o,), c_w_out[i].astype(BF16), x, mod, ln_g[l, 0], ln_b[l, 0],
                                             rw_hi, rw_lo, alpha, "c")
        x = _moe_sublayer(x1, hp, logits, mod, ln_g[l, 1], ln_b[l, 1], w_gate[l], w_up[l], w_down[l], alpha)
    return x
```

```python
import functools
import math

import numpy as np
import jax
import jax.numpy as jnp
from jax import lax
from jax.experimental import pallas as pl
from jax.experimental.pallas import tpu as pltpu

F32 = jnp.float32
BF16 = jnp.bfloat16
I32 = jnp.int32
U32 = jnp.uint32

HEAD_DIM = 128
A_HEADS = 4
A_VDIM = 2 * HEAD_DIM
B_HEADS = 8
B_BRANCHES = ((128, 1), (512, 4), (2048, 16))
C_Q_HEADS = 16
C_KV_HEADS = 4
C_RADIUS = 128
EC_CAPACITY_FACTOR = 2
LN_EPS = 1e-5
NEG = -1e30
LOG2E = 1.4426950408889634

A_QK = A_HEADS * 2 * HEAD_DIM
A_V = A_HEADS * A_VDIM
B_W = B_HEADS * HEAD_DIM
AB_IN = 2 * A_QK + A_V + 3 * B_W
C_QW = C_Q_HEADS * HEAD_DIM
C_KVW = C_KV_HEADS * HEAD_DIM
C_IN = C_QW + 2 * C_KVW

VMEM_LIMIT_BYTES = 56 * 1024 * 1024


def _params(*sem):
    return pltpu.CompilerParams(dimension_semantics=sem, vmem_limit_bytes=VMEM_LIMIT_BYTES)


def _tile(n, preferred):
    t = min(n, preferred)
    while n % t:
        t //= 2
    return t


def _alibi_slopes(n):
    return np.array([2.0 ** (-8.0 * (i + 1) / n) for i in range(n)], dtype=np.float32)


def _nt_dot(a, b):
    return lax.dot_general(a, b, (((1,), (1,)), ((), ())), preferred_element_type=F32)


def _mod_kernel(c_ref, w_ref, b_ref, o_ref):
    c = c_ref[...]
    cs = (c * jax.nn.sigmoid(c)).astype(BF16)
    o_ref[0] = jnp.dot(cs, w_ref[0].astype(BF16), preferred_element_type=F32) + b_ref[0]


def _modulation(c, ada_w, ada_b):
    L, D, N = ada_w.shape
    B = c.shape[0]
    tn = _tile(N, 1024)
    return pl.pallas_call(
        _mod_kernel,
        grid=(L, N // tn),
        in_specs=[
            pl.BlockSpec((B, D), lambda l, j: (0, 0)),
            pl.BlockSpec((1, D, tn), lambda l, j: (l, 0, j)),
            pl.BlockSpec((1, 1, tn), lambda l, j: (l, 0, j)),
        ],
        out_specs=pl.BlockSpec((1, B, tn), lambda l, j: (l, 0, j)),
        out_shape=jax.ShapeDtypeStruct((L, B, N), F32),
        compiler_params=_params("parallel", "parallel"),
        name="adaln_mod",
    )(c, ada_w, ada_b.reshape(L, 1, N))


def _inproj_kernel(x_ref, sh_ref, sc_ref, w_ref, cs_ref, o_ref, *rest, n_bf16):
    h_ref = rest[-1]
    j = pl.program_id(2)

    @pl.when(j == 0)
    def _():
        h_ref[...] = (x_ref[0] * (1.0 + sc_ref[0]) + sh_ref[0]).astype(BF16)

    acc = jnp.dot(h_ref[...], w_ref[...], preferred_element_type=F32) * cs_ref[...]
    if n_bf16 is None:
        o_ref[0] = acc.astype(BF16)
    else:
        of_ref = rest[0]

        @pl.when(j < n_bf16)
        def _():
            o_ref[0] = acc.astype(BF16)

        @pl.when(j >= n_bf16)
        def _():
            of_ref[0] = acc


def _in_projection(x, mod, w_bf16, colscale, n_f32_cols=0):
    B, S, D = x.shape
    N = w_bf16.shape[1]
    tm = min(S, 1024)
    tn = _tile(N, 1024)
    in_specs = [
        pl.BlockSpec((1, tm, D), lambda b, i, j: (b, i, 0)),
        pl.BlockSpec((1, 1, D), lambda b, i, j: (b, 0, 0)),
        pl.BlockSpec((1, 1, D), lambda b, i, j: (b, 0, 1)),
        pl.BlockSpec((D, tn), lambda b, i, j: (0, j)),
        pl.BlockSpec((1, tn), lambda b, i, j: (0, j)),
    ]
    if n_f32_cols:
        n_bf16 = (N - n_f32_cols) // tn
        out_specs = [pl.BlockSpec((1, tm, tn), lambda b, i, j: (b, i, jnp.minimum(j, n_bf16 - 1))),
                     pl.BlockSpec((1, tm, tn), lambda b, i, j: (b, i, jnp.maximum(j - n_bf16, 0)))]
        out_shape = [jax.ShapeDtypeStruct((B, S, N - n_f32_cols), BF16),
                     jax.ShapeDtypeStruct((B, S, n_f32_cols), F32)]
    else:
        n_bf16 = None
        out_specs = pl.BlockSpec((1, tm, tn), lambda b, i, j: (b, i, j))
        out_shape = jax.ShapeDtypeStruct((B, S, N), BF16)
    return pl.pallas_call(
        functools.partial(_inproj_kernel, n_bf16=n_bf16),
        grid=(B, S // tm, N // tn),
        in_specs=in_specs,
        out_specs=out_specs,
        out_shape=out_shape,
        scratch_shapes=[pltpu.VMEM((tm, D), BF16)],
        compiler_params=_params("parallel", "parallel", "arbitrary"),
        name="in_proj",
    )(x, mod, mod, w_bf16, colscale)


def _diff_kernel(slope_ref, q_ref, k_ref, v_ref, lam_ref, g_ref, o_ref,
                 m1_ref, l1_ref, a1_ref, m2_ref, l2_ref, a2_ref, *, tq, tk, n_chunks, lam_init):
    h = pl.program_id(1)
    i = pl.program_id(2)
    slope2 = slope_ref[h]
    q = q_ref[0]
    q1 = q[:, :HEAD_DIM]
    q2 = q[:, HEAD_DIM:]
    relf = (lax.broadcasted_iota(I32, (tq, tk), 1) - lax.broadcasted_iota(I32, (tq, tk), 0)).astype(F32)

    m1_ref[...] = jnp.full(m1_ref.shape, NEG, F32)
    m2_ref[...] = jnp.full(m2_ref.shape, NEG, F32)
    l1_ref[...] = jnp.zeros(l1_ref.shape, F32)
    l2_ref[...] = jnp.zeros(l2_ref.shape, F32)
    a1_ref[...] = jnp.zeros(a1_ref.shape, F32)
    a2_ref[...] = jnp.zeros(a2_ref.shape, F32)

    def chunk(c, carry):
        k0 = pl.multiple_of(c * tk, tk)
        kc = k_ref[0, pl.ds(k0, tk), :]
        vc = v_ref[0, pl.ds(k0, tk), :]
        bias = slope2 * jnp.abs(relf + (k0 - i * tq).astype(F32))

        def one(qm, km, m_ref, l_ref, a_ref):
            s = _nt_dot(qm, km) - bias
            m_old = m_ref[...]
            m_new = jnp.maximum(m_old, jnp.max(s, axis=-1, keepdims=True))
            p = jnp.exp2(s - m_new)
            alpha = jnp.exp2(m_old - m_new)
            l_ref[...] = alpha * l_ref[...] + jnp.sum(p, axis=-1, keepdims=True)
            a_ref[...] = alpha * a_ref[...] + jnp.dot(p.astype(BF16), vc, preferred_element_type=F32)
            m_ref[...] = m_new

        one(q1, kc[:, :HEAD_DIM], m1_ref, l1_ref, a1_ref)
        one(q2, kc[:, HEAD_DIM:], m2_ref, l2_ref, a2_ref)
        return carry

    lax.fori_loop(0, n_chunks, chunk, 0)

    lv = lam_ref[...]
    s01 = jnp.sum(lv[0:1, :] * lv[1:2, :], axis=-1, keepdims=True)
    s23 = jnp.sum(lv[2:3, :] * lv[3:4, :], axis=-1, keepdims=True)
    lam = jnp.exp(s01) - jnp.exp(s23) + lam_init
    o = a1_ref[...] / l1_ref[...] - lam * (a2_ref[...] / l2_ref[...])
    ms = jnp.mean(o * o, axis=-1, keepdims=True)
    o = o * lax.rsqrt(ms + LN_EPS) * g_ref[...] * (1.0 - lam_init)
    o_ref[0] = o.astype(BF16)


def _diff_attention(proj, lam_vecs, subln_g, layer_idx):
    B, S, _ = proj.shape
    tq = min(S, 512)
    tk = min(S, 1024)
    lam_init = 0.8 - 0.6 * math.exp(-0.3 * layer_idx)
    slopes2 = jnp.asarray(_alibi_slopes(A_HEADS) * LOG2E)
    nq = A_QK // A_VDIM
    kern = functools.partial(_diff_kernel, tq=tq, tk=tk, n_chunks=S // tk, lam_init=lam_init)
    return pl.pallas_call(
        kern,
        grid=(B, A_HEADS, S // tq),
        in_specs=[
            pl.BlockSpec(memory_space=pltpu.SMEM),
            pl.BlockSpec((1, tq, A_VDIM), lambda b, h, i: (b, i, h)),
            pl.BlockSpec((1, S, A_VDIM), lambda b, h, i: (b, 0, nq + h)),
            pl.BlockSpec((1, S, A_VDIM), lambda b, h, i: (b, 0, 2 * nq + h)),
            pl.BlockSpec((4, HEAD_DIM), lambda b, h, i: (0, 0)),
            pl.BlockSpec((1, A_VDIM), lambda b, h, i: (0, 0)),
        ],
        out_specs=pl.BlockSpec((1, tq, A_VDIM), lambda b, h, i: (b, i, h)),
        out_shape=jax.ShapeDtypeStruct((B, S, A_V), BF16),
        scratch_shapes=[
            pltpu.VMEM((tq, 1), F32), pltpu.VMEM((tq, 1), F32), pltpu.VMEM((tq, A_VDIM), F32),
            pltpu.VMEM((tq, 1), F32), pltpu.VMEM((tq, 1), F32), pltpu.VMEM((tq, A_VDIM), F32),
        ],
        compiler_params=_params("parallel", "parallel", "parallel"),
        name="diff_attn",
    )(slopes2, proj, proj, proj, lam_vecs, subln_g.reshape(1, A_VDIM))


def _banded_kernel(slope_ref, sink_ref, q_ref, kp_ref, kc_ref, kn_ref, vp_ref, vc_ref, vn_ref, *rest,
                   tq, radius, n_q, group, n_rows, dist_scale, use_sink, want_lse):
    if want_lse:
        o_ref, lse_ref, kwin, vwin = rest
    else:
        o_ref, kwin, vwin = rest
        lse_ref = None
    t = pl.program_id(2)
    w = tq + 2 * radius
    kwin[0:radius, :] = kp_ref[0]
    kwin[radius:radius + tq, :] = kc_ref[0]
    kwin[radius + tq:w, :] = kn_ref[0]
    vwin[0:radius, :] = vp_ref[0]
    vwin[radius:radius + tq, :] = vc_ref[0]
    vwin[radius + tq:w, :] = vn_ref[0]

    ii = lax.broadcasted_iota(I32, (tq, w), 0)
    jj = lax.broadcasted_iota(I32, (tq, w), 1)
    rel = jnp.abs(jj - radius - ii)
    kpos = t * tq - radius + jj
    valid = (rel <= radius) & (kpos >= 0) & (kpos < n_rows)
    dist = rel.astype(F32) * float(dist_scale)
    lane = lax.broadcasted_iota(I32, (tq, HEAD_DIM), 1)
    lse_tile = jnp.zeros((tq, HEAD_DIM), F32)

    for h in range(n_q):
        hk = h // group
        qh = q_ref[0, :, h * HEAD_DIM:(h + 1) * HEAD_DIM]
        kh = kwin[:, hk * HEAD_DIM:(hk + 1) * HEAD_DIM]
        vh = vwin[:, hk * HEAD_DIM:(hk + 1) * HEAD_DIM]
        s = jnp.where(valid, _nt_dot(qh, kh) - slope_ref[h] * dist, NEG)
        m = jnp.max(s, axis=-1, keepdims=True)
        if use_sink:
            m = jnp.maximum(m, sink_ref[h])
        p = jnp.exp2(s - m)
        den = jnp.sum(p, axis=-1, keepdims=True)
        if use_sink:
            den = den + jnp.exp2(sink_ref[h] - m)
        o = jnp.dot(p.astype(BF16), vh, preferred_element_type=F32) / den
        o_ref[0, :, h * HEAD_DIM:(h + 1) * HEAD_DIM] = o.astype(BF16)
        if want_lse:
            lse_tile = jnp.where(lane == h, m + jnp.log2(den), lse_tile)
    if want_lse:
        lse_ref[0] = lse_tile


def _banded_attention(src, *, n_rows, n_res, src_cols, q_blk, k_blk, v_blk, n_q, group, radius,
                      dist_scale, slopes2, sink2, want_lse):
    B = src.shape[0]
    qw = n_q * HEAD_DIM
    kw = (n_q // group) * HEAD_DIM
    tq = min(n_rows, 256)
    nt = n_rows // tq
    per_t = tq // radius
    last_halo = n_rows // radius - 1
    qpg = src_cols // qw
    kpg = src_cols // kw

    def q_map(b, r, t):
        return (b, t, r * qpg + q_blk)

    def cur_map(blk):
        return lambda b, r, t: (b, t, r * kpg + blk)

    def prev_map(blk):
        return lambda b, r, t: (b, jnp.maximum(t * per_t - 1, 0), r * kpg + blk)

    def next_map(blk):
        return lambda b, r, t: (b, jnp.minimum((t + 1) * per_t, last_halo), r * kpg + blk)

    out_shape = [jax.ShapeDtypeStruct((B, n_rows, n_res * qw), BF16)]
    out_specs = [pl.BlockSpec((1, tq, qw), lambda b, r, t: (b, t, r))]
    if want_lse:
        out_shape.append(jax.ShapeDtypeStruct((B, n_rows, n_res * HEAD_DIM), F32))
        out_specs.append(pl.BlockSpec((1, tq, HEAD_DIM), lambda b, r, t: (b, t, r)))
    kern = functools.partial(_banded_kernel, tq=tq, radius=radius, n_q=n_q, group=group, n_rows=n_rows,
                             dist_scale=dist_scale, use_sink=sink2 is not None, want_lse=want_lse)
    if sink2 is None:
        sink2 = jnp.zeros((n_q,), F32)
    return pl.pallas_call(
        kern,
        grid=(B, n_res, nt),
        in_specs=[
            pl.BlockSpec(memory_space=pltpu.SMEM),
            pl.BlockSpec(memory_space=pltpu.SMEM),
            pl.BlockSpec((1, tq, qw), q_map),
            pl.BlockSpec((1, radius, kw), prev_map(k_blk)),
            pl.BlockSpec((1, tq, kw), cur_map(k_blk)),
            pl.BlockSpec((1, radius, kw), next_map(k_blk)),
            pl.BlockSpec((1, radius, kw), prev_map(v_blk)),
            pl.BlockSpec((1, tq, kw), cur_map(v_blk)),
            pl.BlockSpec((1, radius, kw), next_map(v_blk)),
        ],
        out_specs=out_specs,
        out_shape=out_shape,
        scratch_shapes=[pltpu.VMEM((tq + 2 * radius, kw), BF16), pltpu.VMEM((tq + 2 * radius, kw), BF16)],
        compiler_params=_params("parallel", "parallel", "parallel"),
        name="banded_attn",
    )(slopes2, sink2, src, src, src, src, src, src, src)


def _band_bias(tq, wlen, offset, radius, slope_dist):
    rel = jnp.abs(offset + lax.broadcasted_iota(I32, (tq, wlen), 1) - lax.broadcasted_iota(I32, (tq, wlen), 0))
    return jnp.where(rel <= radius, -slope_dist * rel.astype(F32), NEG)


def _band_tile(qf, kf, vf, bias):
    s = _nt_dot(qf, kf) + bias
    m = jnp.max(s, axis=-1, keepdims=True)
    p = jnp.exp2(s - m)
    return m, jnp.sum(p, axis=-1, keepdims=True), jnp.dot(p.astype(BF16), vf, preferred_element_type=F32)


TILE_UNROLL = 4


def _loop_tiles(n, tile_fn):
    unroll = TILE_UNROLL if n % TILE_UNROLL == 0 else 1

    def body(i, carry):
        for u in range(unroll):
            tile_fn(i * unroll + u)
        return carry

    lax.fori_loop(0, n // unroll, body, 0)


def _band_geometry(seq, tq, radius, dil):
    n_rows = seq // dil
    tile = min(tq, n_rows)
    nt = n_rows // tile
    wlen = min(tile + 2 * radius, n_rows)
    offsets = (0,) if nt == 1 else (0, -radius, tile - wlen)
    return n_rows, tile, nt, wlen, offsets


def _dilated_kernel(slope_ref, q_ref, k_ref, v_ref, o_ref, acc_s, m_s, l_s, bias_s, *, seq, tq, branches):
    slope = slope_ref[pl.program_id(1)]
    lanes = (tq, HEAD_DIM)

    first_bias = []
    n_bias = 0
    for radius, dil in branches:
        _, tile, _, wlen, offsets = _band_geometry(seq, tq, radius, dil)
        first_bias.append(n_bias)
        for off in offsets:
            bias_s[n_bias, 0:tile, 0:wlen] = _band_bias(tile, wlen, off, radius, slope * dil)
            n_bias += 1

    def tile_window(bi, t):
        radius, dil = branches[bi]
        n_rows, tile, nt, wlen, _ = _band_geometry(seq, tq, radius, dil)
        q0 = t * tile
        k0 = jnp.clip(q0 - radius, 0, n_rows - wlen)
        case = 0 if nt == 1 else jnp.where(t == 0, 0, jnp.where(t == nt - 1, 2, 1))
        return q0, k0, bias_s[first_bias[bi] + case, 0:tile, 0:wlen]

    for bi, (radius, dil) in enumerate(branches[:-1]):
        _, tile, nt, wlen, _ = _band_geometry(seq, tq, radius, dil)

        def one_tile(it, bi=bi, dil=dil, tile=tile, nt=nt, wlen=wlen):
            r = it // nt
            q0, k0, bias = tile_window(bi, it % nt)
            qrows = pl.ds(r + dil * q0, tile, stride=dil)
            krows = pl.ds(r + dil * k0, wlen, stride=dil)
            m, l, acc = _band_tile(q_ref[0, qrows, :].astype(BF16), k_ref[0, krows, :].astype(BF16),
                                   v_ref[0, krows, :].astype(BF16), bias)
            acc_s[bi, qrows, :] = acc
            m_s[bi, qrows, :] = jnp.broadcast_to(m, (tile, HEAD_DIM))
            l_s[bi, qrows, :] = jnp.broadcast_to(l, (tile, HEAD_DIM))

        _loop_tiles(dil * nt, one_tile)

    assert branches[-1][1] == 1
    wlen = _band_geometry(seq, tq, *branches[-1])[3]

    def dense(t):
        q0, k0, bias = tile_window(len(branches) - 1, t)
        qrows = pl.ds(pl.multiple_of(q0, tq), tq)
        krows = pl.ds(pl.multiple_of(k0, 8), wlen)
        m, l, acc = _band_tile(q_ref[0, qrows, :].astype(BF16), k_ref[0, krows, :].astype(BF16),
                               v_ref[0, krows, :].astype(BF16), bias)
        ms = [jnp.broadcast_to(m, lanes)] + [m_s[bi, qrows, :] for bi in range(len(branches) - 1)]
        ls = [jnp.broadcast_to(l, lanes)] + [l_s[bi, qrows, :] for bi in range(len(branches) - 1)]
        accs = [acc] + [acc_s[bi, qrows, :] for bi in range(len(branches) - 1)]
        top = functools.reduce(jnp.maximum, ms)
        es = [jnp.exp2(mi - top) for mi in ms]
        num = sum(e * a for e, a in zip(es, accs))
        den = sum(e * li for e, li in zip(es, ls))
        o_ref[0, qrows, :] = (num / den).astype(BF16)

    _loop_tiles(seq // tq, dense)


def _dilated_mixture(projf, slopes2):
    B, S, _ = projf.shape
    branches = tuple((w // (2 * d), d) for w, d in sorted(B_BRANCHES, key=lambda wd: -wd[1]))
    tq = min(S, 256)
    kern = functools.partial(_dilated_kernel, seq=S, tq=tq, branches=branches)
    nb = len(branches) - 1
    geo = [_band_geometry(S, tq, radius, dil) for radius, dil in branches]
    bias_shape = (sum(len(g[4]) for g in geo), tq, max(g[3] for g in geo))
    col = lambda part: (lambda b, h: (b, 0, part * B_HEADS + h))
    return pl.pallas_call(
        kern,
        grid=(B, B_HEADS),
        in_specs=[pl.BlockSpec(memory_space=pltpu.SMEM)] + [pl.BlockSpec((1, S, HEAD_DIM), col(p)) for p in range(3)],
        out_specs=pl.BlockSpec((1, S, HEAD_DIM), col(0)),
        out_shape=jax.ShapeDtypeStruct((B, S, B_W), BF16),
        scratch_shapes=[pltpu.VMEM((nb, S, HEAD_DIM), F32)] * 3 + [pltpu.VMEM(bias_shape, F32)],
        compiler_params=_params("parallel", "parallel"),
        name="dilated_mix",
    )(slopes2, projf, projf, projf)


def _post_attention(y, x, g1, lng, lnb, sc2, sh2, rwh_ref, rwl_ref, x1_ref, hp_ref, lg_ref, alpha):
    z = alpha * x + g1 * y
    mu = jnp.mean(z, axis=-1, keepdims=True)
    zc = z - mu
    var = jnp.mean(zc * zc, axis=-1, keepdims=True)
    x1 = zc * lax.rsqrt(var + LN_EPS) * lng + lnb
    x1_ref[0] = x1
    h2 = x1 * (1.0 + sc2) + sh2
    hb = h2.astype(BF16)
    hf = hb.astype(F32)
    bits = lax.bitcast_convert_type(hf, U32)
    half = bits.shape[1] // 2
    packed = bits[:, :half] | (bits[:, half:] >> 16)
    nck = half // HEAD_DIM
    for ck in range(nck):
        hp_ref[0, pl.ds(ck, bits.shape[0], stride=nck), :] = packed[:, ck * HEAD_DIM:(ck + 1) * HEAD_DIM]
    lo = (h2 - hf).astype(BF16)
    rwh = rwh_ref[...]
    lg_ref[0] = (jnp.dot(hb, rwh, preferred_element_type=F32)
                 + jnp.dot(hb, rwl_ref[...], preferred_element_type=F32)
                 + jnp.dot(lo, rwh, preferred_element_type=F32))


def _outproj_ab_kernel(ya_ref, yb_ref, w_ref, x_ref, g1_ref, lng_ref, lnb_ref, sh2_ref, sc2_ref, rwh_ref, rwl_ref,
                       x1_ref, hp_ref, lg_ref, *, alpha):
    y = (jnp.dot(ya_ref[0], w_ref[0:A_V, :], preferred_element_type=F32)
         + jnp.dot(yb_ref[0], w_ref[A_V:A_V + B_W, :], preferred_element_type=F32))
    _post_attention(y, x_ref[0], g1_ref[0], lng_ref[...], lnb_ref[...], sc2_ref[0], sh2_ref[0],
                    rwh_ref, rwl_ref, x1_ref, hp_ref, lg_ref, alpha)


def _outproj_c_kernel(o_ref, w_ref, x_ref, g1_ref, lng_ref, lnb_ref, sh2_ref, sc2_ref, rwh_ref, rwl_ref,
                      x1_ref, hp_ref, lg_ref, *, alpha):
    y = jnp.dot(o_ref[0], w_ref[...], preferred_element_type=F32)
    _post_attention(y, x_ref[0], g1_ref[0], lng_ref[...], lnb_ref[...], sc2_ref[0], sh2_ref[0],
                    rwh_ref, rwl_ref, x1_ref, hp_ref, lg_ref, alpha)


def _out_projection(attn_inputs, w_bf16, x, mod, lng, lnb, rw_hi, rw_lo, alpha, mixer):
    B, S, D = x.shape
    E = rw_hi.shape[1]
    tm = min(S, 512)
    row = lambda b, i: (b, i, 0)
    const2 = lambda b, i: (0, 0)
    tail_specs = [
        pl.BlockSpec(w_bf16.shape, const2),
        pl.BlockSpec((1, tm, D), row),
        pl.BlockSpec((1, 1, D), lambda b, i: (b, 0, 2)),
        pl.BlockSpec((1, D), const2),
        pl.BlockSpec((1, D), const2),
        pl.BlockSpec((1, 1, D), lambda b, i: (b, 0, 3)),
        pl.BlockSpec((1, 1, D), lambda b, i: (b, 0, 4)),
        pl.BlockSpec((D, E), const2),
        pl.BlockSpec((D, E), const2),
    ]
    tail_args = (w_bf16, x, mod, lng.reshape(1, D), lnb.reshape(1, D), mod, mod, rw_hi, rw_lo)
    nck = D // 2 // HEAD_DIM
    out_shape = [jax.ShapeDtypeStruct((B, S, D), F32), jax.ShapeDtypeStruct((B, S * nck, HEAD_DIM), U32),
                 jax.ShapeDtypeStruct((B, S, E), F32)]
    out_specs = [pl.BlockSpec((1, tm, D), row), pl.BlockSpec((1, tm * nck, HEAD_DIM), row),
                 pl.BlockSpec((1, tm, E), row)]
    if mixer == "ab":
        ya, yb = attn_inputs
        in_specs = [pl.BlockSpec((1, tm, A_V), row), pl.BlockSpec((1, tm, B_W), row)] + tail_specs
        kern = functools.partial(_outproj_ab_kernel, alpha=alpha)
        scratch = []
        args = (ya, yb) + tail_args
    else:
        (o,) = attn_inputs
        in_specs = [pl.BlockSpec((1, tm, o.shape[2]), row)] + tail_specs
        kern = functools.partial(_outproj_c_kernel, alpha=alpha)
        scratch = []
        args = (o,) + tail_args
    return pl.pallas_call(
        kern,
        grid=(B, S // tm),
        in_specs=in_specs,
        out_specs=out_specs,
        out_shape=out_shape,
        scratch_shapes=scratch,
        compiler_params=_params("parallel", "parallel"),
        name="out_proj_" + mixer,
    )(*args)


def _excl_cumsum_lanes(x, tri):
    n = x.shape[1] // HEAD_DIM
    carry = jnp.zeros((x.shape[0], 1), F32)
    outs = []
    for c in range(n):
        xc = x[:, c * HEAD_DIM:(c + 1) * HEAD_DIM]
        outs.append(jnp.dot(xc.astype(BF16), tri, preferred_element_type=F32) + carry)
        carry = carry + jnp.sum(xc, axis=1, keepdims=True)
    return jnp.concatenate(outs, axis=1)


def _tables_kernel(lg_ref, cnt_ref, dst_ref, gk_ref, *, n_exp, cap):
    lg = lg_ref[0]
    S = lg.shape[1]
    ex = jnp.exp(lg - jnp.max(lg, axis=0, keepdims=True))
    aff = ex / jnp.sum(ex, axis=0, keepdims=True)
    bits = lax.bitcast_convert_type(aff, I32)

    def search(i, cur):
        cand = cur | jnp.left_shift(jnp.int32(1), 30 - i)
        cnt = jnp.sum(jnp.where(bits >= cand, 1.0, 0.0), axis=1, keepdims=True)
        return jnp.where(cnt >= cap, cand, cur)

    thr = lax.fori_loop(0, 31, search, jnp.zeros((n_exp, 1), I32))
    tri = jnp.where(lax.broadcasted_iota(I32, (HEAD_DIM, HEAD_DIM), 0)
                    < lax.broadcasted_iota(I32, (HEAD_DIM, HEAD_DIM), 1), 1.0, 0.0).astype(BF16)
    gt = bits > thr
    eq = jnp.where(bits == thr, 1.0, 0.0)
    need = cap - jnp.sum(jnp.where(gt, 1.0, 0.0), axis=1, keepdims=True)
    sel = jnp.where(gt, 1.0, jnp.where(_excl_cumsum_lanes(eq, tri) < need, eq, 0.0))
    pos = _excl_cumsum_lanes(sel, tri)

    slot = lax.broadcasted_iota(I32, (n_exp, S), 0).astype(F32)
    run = jnp.zeros((1, S), F32)
    dst = jnp.zeros((n_exp, S), F32)
    gk = jnp.zeros((n_exp, S), F32)
    for e in range(n_exp):
        se = sel[e:e + 1, :]
        hit = (slot == run) & (se > 0.0)
        dst = jnp.where(hit, pos[e:e + 1, :] + float(e * cap), dst)
        gk = jnp.where(hit, aff[e:e + 1, :], gk)
        run = run + se
    cnt_ref[0] = run.astype(I32)
    dst_ref[0] = dst.astype(I32)
    gk_ref[0] = gk


def _routing_tables(logits_t, cap):
    B, E, S = logits_t.shape
    kern = functools.partial(_tables_kernel, n_exp=E, cap=cap)
    return pl.pallas_call(
        kern,
        grid=(B,),
        in_specs=[pl.BlockSpec((1, E, S), lambda b: (b, 0, 0))],
        out_specs=[pl.BlockSpec((1, 1, S), lambda b: (b, 0, 0)), pl.BlockSpec((1, E, S), lambda b: (b, 0, 0)),
                   pl.BlockSpec((1, E, S), lambda b: (b, 0, 0))],
        out_shape=[jax.ShapeDtypeStruct((B, 1, S), I32), jax.ShapeDtypeStruct((B, E, S), I32),
                   jax.ShapeDtypeStruct((B, E, S), F32)],
        compiler_params=_params("parallel"),
        name="routing_tables",
    )(logits_t)


def _index_kernel(cnt_ref, dst_ref, gk_ref, off_ref, tok_ref, grow_ref, pos_ref, *, tc, n_chunks):
    tok0 = (pl.program_id(0) * n_chunks + pl.program_id(1)) * tc
    base = off_ref[0]

    def per_token(t, n):
        def per_choice(k, n):
            code = dst_ref[k * tc + t]
            tok_ref[code] = tok0 + t
            grow_ref[code] = gk_ref[k * tc + t]
            pos_ref[code] = base + n
            return n + 1

        return lax.fori_loop(0, cnt_ref[t], per_choice, n)

    lax.fori_loop(0, tc, per_token, 0)


def _expert_order_lists(cnt, dst, gk, off, n_exp, cap):
    B, _, S = cnt.shape
    tc = min(S, 1024)
    n_chunks = S // tc
    by_chunk = lambda a: a.reshape(B, n_exp, n_chunks, tc).swapaxes(1, 2).reshape(B * S * n_exp)
    kern = functools.partial(_index_kernel, tc=tc, n_chunks=n_chunks)
    smem = functools.partial(pl.BlockSpec, memory_space=pltpu.SMEM)
    per_chunk = lambda b, c: (b * n_chunks + c,)
    out_spec = smem((n_exp * cap,), lambda b, c: (b,))
    tok, grow, pos = pl.pallas_call(
        kern,
        grid=(B, n_chunks),
        in_specs=[smem((tc,), per_chunk), smem((n_exp * tc,), per_chunk), smem((n_exp * tc,), per_chunk),
                  smem((tc,), per_chunk)],
        out_specs=[out_spec, out_spec, out_spec],
        out_shape=[jax.ShapeDtypeStruct((B * n_exp * cap,), I32), jax.ShapeDtypeStruct((B * n_exp * cap,), F32),
                   jax.ShapeDtypeStruct((B * n_exp * cap,), I32)],
        compiler_params=_params("arbitrary", "arbitrary"),
        name="moe_index",
    )(cnt.reshape(B * S), by_chunk(dst), by_chunk(gk), off.reshape(B * S))
    expert_major = lambda a: a.reshape(B, n_exp, cap).swapaxes(0, 1).reshape(n_exp * B * cap)
    return expert_major(tok), expert_major(grow), expert_major(pos)


def _token_rows(t, nck):
    start = t * nck
    return pl.ds(pl.multiple_of(start, nck) if nck > 1 else start, nck)


def _unpack_rows(xp_ref, first, xs_ref):
    rows, d = xs_ref.shape
    half = d // 2
    nck = half // HEAD_DIM
    for ck in range(nck):
        u = xp_ref[pl.ds(first * nck + ck, rows, stride=nck), :]
        lo = ck * HEAD_DIM
        xs_ref[:, lo:lo + HEAD_DIM] = lax.bitcast_convert_type(u & jnp.uint32(0xFFFF0000), F32).astype(BF16)
        xs_ref[:, half + lo:half + lo + HEAD_DIM] = lax.bitcast_convert_type(u << 16, F32).astype(BF16)


def _pack_pairs(hi, lo):
    hb = lax.bitcast_convert_type(hi.astype(BF16).astype(F32), U32)
    lb = lax.bitcast_convert_type(lo.astype(BF16).astype(F32), U32)
    return hb | (lb >> 16)


ROW_UNROLL = 8


def _ffn_up_kernel(tok_ref, tokn_ref, hp_ref, wg_ref, wu_ref, o_ref, xraw, xs_ref, sem, *, tm, mt, n_tiles, nck):
    tile = pl.program_id(0) * mt + pl.program_id(1)
    par = tile % 2

    def gather(idx_ref, slot):
        def body(i, carry):
            for u in range(ROW_UNROLL):
                r = i * ROW_UNROLL + u
                pltpu.make_async_copy(hp_ref.at[_token_rows(idx_ref[r], nck)],
                                      xraw.at[_token_rows(slot * tm + r, nck)], sem.at[slot]).start()
            return carry

        lax.fori_loop(0, tm // ROW_UNROLL, body, 0)

    @pl.when(pl.program_id(2) == 0)
    def _():
        @pl.when(tile == 0)
        def _():
            gather(tok_ref, 0)

        pltpu.make_async_copy(hp_ref.at[pl.ds(0, tm * nck)], xraw.at[pl.ds(par * tm * nck, tm * nck)],
                              sem.at[par]).wait()

        @pl.when(tile + 1 < n_tiles)
        def _():
            gather(tokn_ref, 1 - par)

        _unpack_rows(xraw, par * tm, xs_ref)

    x = xs_ref[...]
    g = jnp.dot(x, wg_ref[0].astype(BF16), preferred_element_type=F32)
    u = jnp.dot(x, wu_ref[0].astype(BF16), preferred_element_type=F32)
    o_ref[...] = (g * jax.nn.sigmoid(g) * u).astype(BF16)


def _ffn_up(hp, tok_rows, w_gate, w_up, rows_per_expert):
    E, D, Fh = w_gate.shape
    tm = min(rows_per_expert, 2048)
    mt = rows_per_expert // tm
    n_tiles = E * mt
    tn = min(Fh, 256)
    nck = D // 2 // HEAD_DIM
    kern = functools.partial(_ffn_up_kernel, tm=tm, mt=mt, n_tiles=n_tiles, nck=nck)
    smem = functools.partial(pl.BlockSpec, memory_space=pltpu.SMEM)
    return pl.pallas_call(
        kern,
        grid=(E, mt, Fh // tn),
        in_specs=[smem((tm,), lambda e, m, n: (e * mt + m,)),
                  smem((tm,), lambda e, m, n: (jnp.minimum(e * mt + m + 1, n_tiles - 1),)),
                  pl.BlockSpec(memory_space=pl.ANY),
                  pl.BlockSpec((1, D, tn), lambda e, m, n: (e, 0, n)),
                  pl.BlockSpec((1, D, tn), lambda e, m, n: (e, 0, n))],
        out_specs=pl.BlockSpec((tm, tn), lambda e, m, n: (e * mt + m, n)),
        out_shape=jax.ShapeDtypeStruct((E * rows_per_expert, Fh), BF16),
        scratch_shapes=[pltpu.VMEM((2 * tm * nck, HEAD_DIM), U32), pltpu.VMEM((tm, D), BF16),
                        pltpu.SemaphoreType.DMA((2,))],
        compiler_params=_params("arbitrary", "arbitrary", "arbitrary"),
        name="ffn_up",
    )(tok_rows, tok_rows, hp, w_gate, w_up)


def _ffn_down_kernel(pos_ref, h_ref, wa_ref, wb_ref, g_ref, yt_ref, ybuf, sem, *, n_steps, cps, tm, mt, n_tiles,
                     nck):
    tile = pl.program_id(0) * mt + pl.program_id(1)
    par = tile % 2
    h = h_ref[...]
    g = g_ref[...]
    ya = jnp.dot(h, wa_ref[0].astype(BF16), preferred_element_type=F32) * g
    yb = jnp.dot(h, wb_ref[0].astype(BF16), preferred_element_type=F32) * g
    packed = _pack_pairs(ya, yb)
    n = pl.program_id(2)
    for k in range(n_steps):
        @pl.when(n == k)
        def _():
            for cc in range(cps):
                ybuf[pl.ds(par * tm * nck + k * cps + cc, tm, stride=nck), :] = (
                    packed[:, cc * HEAD_DIM:(cc + 1) * HEAD_DIM])

    def scatter_done(slot):
        return pltpu.make_async_copy(ybuf.at[pl.ds(slot * tm * nck, tm * nck)], yt_ref.at[pl.ds(0, tm * nck)],
                                     sem.at[slot])

    @pl.when(n == n_steps - 1)
    def _():
        @pl.when(tile > 0)
        def _():
            scatter_done(1 - par).wait()

        def body(i, carry):
            for u in range(ROW_UNROLL):
                r = i * ROW_UNROLL + u
                pltpu.make_async_copy(ybuf.at[_token_rows(par * tm + r, nck)],
                                      yt_ref.at[_token_rows(pos_ref[r], nck)], sem.at[par]).start()
            return carry

        lax.fori_loop(0, tm // ROW_UNROLL, body, 0)

        @pl.when(tile == n_tiles - 1)
        def _():
            scatter_done(par).wait()


def _ffn_down(hid, w_down, gate_col, pos_rows, rows_per_expert):
    E, Fh, D = w_down.shape
    tm = min(rows_per_expert, 2048)
    mt = rows_per_expert // tm
    n_tiles = E * mt
    half = D // 2
    tnh = min(half, 256)
    n_steps = half // tnh
    nck = half // HEAD_DIM
    kern = functools.partial(_ffn_down_kernel, n_steps=n_steps, cps=tnh // HEAD_DIM, tm=tm, mt=mt, n_tiles=n_tiles,
                             nck=nck)
    return pl.pallas_call(
        kern,
        grid=(E, mt, n_steps),
        in_specs=[pl.BlockSpec((tm,), lambda e, m, n: (e * mt + m,), memory_space=pltpu.SMEM),
                  pl.BlockSpec((tm, Fh), lambda e, m, n: (e * mt + m, 0)),
                  pl.BlockSpec((1, Fh, tnh), lambda e, m, n: (e, 0, n)),
                  pl.BlockSpec((1, Fh, tnh), lambda e, m, n: (e, 0, n + n_steps)),
                  pl.BlockSpec((tm, 1), lambda e, m, n: (e * mt + m, 0))],
        out_specs=pl.BlockSpec(memory_space=pl.ANY),
        out_shape=jax.ShapeDtypeStruct((E * rows_per_expert * nck, HEAD_DIM), U32),
        scratch_shapes=[pltpu.VMEM((2 * tm * nck, HEAD_DIM), U32), pltpu.SemaphoreType.DMA((2,))],
        compiler_params=_params("arbitrary", "arbitrary", "arbitrary"),
        name="ffn_down",
    )(pos_rows, hid, w_down, w_down, gate_col)


def _combine_kernel(offt_ref, x_ref, off_ref, cnt_ref, g2_ref, lng_ref, lnb_ref, yt_ref, o_ref,
                    buf, rows, acc, sem, *, tm, ch, n_total, n_tiles, n_tiles_s, alpha, nck):
    tile = pl.program_id(0) * n_tiles_s + pl.program_id(1)
    o0 = offt_ref[tile]
    o1 = offt_ref[tile + 1]
    n_chunks = jnp.maximum((o1 - o0 + ch - 1) // ch, 1)
    lo_col = off_ref[0]
    hi_col = lo_col + cnt_ref[0]
    lane = lax.broadcasted_iota(I32, (tm, ch), 1)
    acc[...] = jnp.zeros(acc.shape, F32)

    def fetch(want, slot):
        start = jnp.minimum(want, n_total - ch) * nck
        if nck > 1:
            start = pl.multiple_of(start, nck)
        return pltpu.make_async_copy(yt_ref.at[pl.ds(start, ch * nck)], buf.at[pl.ds(slot * ch * nck, ch * nck)],
                                     sem.at[slot])

    @pl.when(tile == 0)
    def _():
        fetch(o0, 0).start()

    def chunk(c, carry):
        slot = c % 2
        want = o0 + c * ch
        fetch(want, slot).wait()

        @pl.when(c + 1 < n_chunks)
        def _():
            fetch(want + ch, 1 - slot).start()

        _unpack_rows(buf, slot * ch, rows)
        row = lane + jnp.minimum(want, n_total - ch)
        own = (row >= lo_col) & (row < hi_col) & (row >= want)
        acc[...] += jnp.dot(jnp.where(own, 1.0, 0.0).astype(BF16), rows[...], preferred_element_type=F32)
        return carry

    lax.fori_loop(0, n_chunks, chunk, 0)

    @pl.when(tile + 1 < n_tiles)
    def _():
        fetch(o1, 0).start()

    z = alpha * x_ref[0] + g2_ref[0] * acc[...]
    mu = jnp.mean(z, axis=-1, keepdims=True)
    zc = z - mu
    var = jnp.mean(zc * zc, axis=-1, keepdims=True)
    o_ref[0] = zc * lax.rsqrt(var + LN_EPS) * lng_ref[...] + lnb_ref[...]


def _combine(x1, mod, lng, lnb, yt, off, cnt, alpha):
    B, S, D = x1.shape
    nck = D // 2 // HEAD_DIM
    n_total = yt.shape[0] // nck
    tm = min(S, 256)
    ch = min(n_total, 256)
    n_tiles_s = S // tm
    off_flat = off.reshape(B * S)
    offt = jnp.concatenate([off_flat[::tm], jnp.full((1,), n_total, I32)])
    kern = functools.partial(_combine_kernel, tm=tm, ch=ch, n_total=n_total, n_tiles=B * n_tiles_s,
                             n_tiles_s=n_tiles_s, alpha=alpha, nck=nck)
    grid_spec = pltpu.PrefetchScalarGridSpec(
        num_scalar_prefetch=1,
        grid=(B, n_tiles_s),
        in_specs=[
            pl.BlockSpec((1, tm, D), lambda b, i, o: (b, i, 0)),
            pl.BlockSpec((1, tm, 1), lambda b, i, o: (b, i, 0)),
            pl.BlockSpec((1, tm, 1), lambda b, i, o: (b, i, 0)),
            pl.BlockSpec((1, 1, D), lambda b, i, o: (b, 0, 5)),
            pl.BlockSpec((1, D), lambda b, i, o: (0, 0)),
            pl.BlockSpec((1, D), lambda b, i, o: (0, 0)),
            pl.BlockSpec(memory_space=pl.ANY),
        ],
        out_specs=pl.BlockSpec((1, tm, D), lambda b, i, o: (b, i, 0)),
        scratch_shapes=[pltpu.VMEM((2 * ch * nck, HEAD_DIM), U32), pltpu.VMEM((ch, D), BF16),
                        pltpu.VMEM((tm, D), F32), pltpu.SemaphoreType.DMA((2,))],
    )
    return pl.pallas_call(
        kern,
        grid_spec=grid_spec,
        out_shape=jax.ShapeDtypeStruct((B, S, D), F32),
        compiler_params=_params("arbitrary", "arbitrary"),
        name="moe_combine",
    )(offt, x1, off.reshape(B, S, 1), cnt.reshape(B, S, 1), mod, lng.reshape(1, D), lnb.reshape(1, D), yt)


def _moe_sublayer(x1, hp, logits, mod, lng, lnb, w_gate, w_up, w_down, alpha):
    B, S, D = x1.shape
    E = logits.shape[-1]
    cap = EC_CAPACITY_FACTOR * S // E
    cnt, dst, gk = _routing_tables(jnp.swapaxes(logits, 1, 2), cap)
    cflat = cnt.reshape(B * S)
    off = (jnp.cumsum(cflat) - cflat).astype(I32).reshape(B, 1, S)
    tok, grow, pos = _expert_order_lists(cnt, dst, gk, off, E, cap)
    hid = _ffn_up(hp.reshape(-1, HEAD_DIM), tok, w_gate, w_up, B * cap)
    yt = _ffn_down(hid, w_down, grow.reshape(E * B * cap, 1), pos, B * cap)
    return _combine(x1, mod, lng, lnb, yt, off, cnt, alpha)


def _split_bf16(w):
    hi = w.astype(BF16)
    return hi, (w - hi.astype(F32)).astype(BF16)


def kernel(x, c, ada_w, ada_b, ln_g, ln_b, ab_w_in, ab_w_out, diff_lambda, diff_subln_g, c_w_in, c_w_out,
           c_sink, router_w, w_gate, w_up, w_down):
    B, S, D = x.shape
    depth = ada_w.shape[0]
    alpha = (2.0 * depth) ** 0.25
    qscale = HEAD_DIM ** -0.5 * LOG2E
    mod_all = _modulation(c, ada_w, ada_b)

    ab_scale = np.ones((1, AB_IN), np.float32)
    ab_scale[:, :A_QK] = qscale
    ab_scale[:, 2 * A_QK + A_V:2 * A_QK + A_V + B_W] = qscale
    c_scale = np.ones((1, C_IN), np.float32)
    c_scale[:, :C_QW] = qscale

    for l in range(depth):
        mod = mod_all[l][:, None, :]
        i = l // 2
        rw_hi, rw_lo = _split_bf16(router_w[l])
        if l % 2 == 0:
            proj, projf = _in_projection(x, mod, ab_w_in[i].astype(BF16), jnp.asarray(ab_scale), n_f32_cols=3 * B_W)
            ya = _diff_attention(proj, diff_lambda[i], diff_subln_g[i], l)
            yb = _dilated_mixture(projf, jnp.asarray(_alibi_slopes(B_HEADS) * LOG2E))
            x1, hp, logits = _out_projection((ya, yb), ab_w_out[i].astype(BF16), x, mod,
                                             ln_g[l, 0], ln_b[l, 0], rw_hi, rw_lo, alpha, "ab")
        else:
            proj = _in_projection(x, mod, c_w_in[i].astype(BF16), jnp.asarray(c_scale))
            (o,) = _banded_attention(
                proj, n_rows=S, n_res=1, src_cols=C_IN, q_blk=0, k_blk=C_QW // C_KVW, v_blk=C_QW // C_KVW + 1,
                n_q=C_Q_HEADS, group=C_Q_HEADS // C_KV_HEADS, radius=C_RADIUS, dist_scale=1,
                slopes2=jnp.asarray(_alibi_slopes(C_Q_HEADS) * LOG2E), sink2=c_sink[i] * LOG2E, want_lse=False)
            x1, hp, logits = _out_projection((o,), c_w_out[i].astype(BF16), x, mod, ln_g[l, 0], ln_b[l, 0],
                                             rw_hi, rw_lo, alpha, "c")
        x = _moe_sublayer(x1, hp, logits, mod, ln_g[l, 1], ln_b[l, 1], w_gate[l], w_up[l], w_down[l], alpha)
    return x
```

```python
import functools
import math

import numpy as np
import jax
import jax.numpy as jnp
from jax import lax
from jax.experimental import pallas as pl
from jax.experimental.pallas import tpu as pltpu

F32 = jnp.float32
BF16 = jnp.bfloat16
I32 = jnp.int32
U32 = jnp.uint32

HEAD_DIM = 128
A_HEADS = 4
A_VDIM = 2 * HEAD_DIM
B_HEADS = 8
B_BRANCHES = ((128, 1), (512, 4), (2048, 16))
C_Q_HEADS = 16
C_KV_HEADS = 4
C_RADIUS = 128
EC_CAPACITY_FACTOR = 2
LN_EPS = 1e-5
NEG = -1e30
LOG2E = 1.4426950408889634

A_QK = A_HEADS * 2 * HEAD_DIM
A_V = A_HEADS * A_VDIM
B_W = B_HEADS * HEAD_DIM
AB_IN = 2 * A_QK + A_V + 3 * B_W
C_QW = C_Q_HEADS * HEAD_DIM
C_KVW = C_KV_HEADS * HEAD_DIM
C_IN = C_QW + 2 * C_KVW

VMEM_LIMIT_BYTES = 56 * 1024 * 1024


def _params(*sem):
    return pltpu.CompilerParams(dimension_semantics=sem, vmem_limit_bytes=VMEM_LIMIT_BYTES)


def _tile(n, preferred):
    t = min(n, preferred)
    while n % t:
        t //= 2
    return t


def _alibi_slopes(n):
    return np.array([2.0 ** (-8.0 * (i + 1) / n) for i in range(n)], dtype=np.float32)


def _nt_dot(a, b):
    return lax.dot_general(a, b, (((1,), (1,)), ((), ())), preferred_element_type=F32)


def _mod_kernel(c_ref, w_ref, b_ref, o_ref):
    c = c_ref[...]
    cs = (c * jax.nn.sigmoid(c)).astype(BF16)
    o_ref[0] = jnp.dot(cs, w_ref[0].astype(BF16), preferred_element_type=F32) + b_ref[0]


def _modulation(c, ada_w, ada_b):
    L, D, N = ada_w.shape
    B = c.shape[0]
    tn = _tile(N, 1024)
    return pl.pallas_call(
        _mod_kernel,
        grid=(L, N // tn),
        in_specs=[
            pl.BlockSpec((B, D), lambda l, j: (0, 0)),
            pl.BlockSpec((1, D, tn), lambda l, j: (l, 0, j)),
            pl.BlockSpec((1, 1, tn), lambda l, j: (l, 0, j)),
        ],
        out_specs=pl.BlockSpec((1, B, tn), lambda l, j: (l, 0, j)),
        out_shape=jax.ShapeDtypeStruct((L, B, N), F32),
        compiler_params=_params("parallel", "parallel"),
        name="adaln_mod",
    )(c, ada_w, ada_b.reshape(L, 1, N))


def _inproj_kernel(x_ref, sh_ref, sc_ref, w_ref, cs_ref, o_ref, *rest, n_bf16):
    h_ref = rest[-1]
    j = pl.program_id(2)

    @pl.when(j == 0)
    def _():
        h_ref[...] = (x_ref[0] * (1.0 + sc_ref[0]) + sh_ref[0]).astype(BF16)

    acc = jnp.dot(h_ref[...], w_ref[...], preferred_element_type=F32) * cs_ref[...]
    if n_bf16 is None:
        o_ref[0] = acc.astype(BF16)
    else:
        of_ref = rest[0]

        @pl.when(j < n_bf16)
        def _():
            o_ref[0] = acc.astype(BF16)

        @pl.when(j >= n_bf16)
        def _():
            of_ref[0] = acc


def _in_projection(x, mod, w_bf16, colscale, n_f32_cols=0):
    B, S, D = x.shape
    N = w_bf16.shape[1]
    tm = min(S, 1024)
    tn = _tile(N, 1024)
    in_specs = [
        pl.BlockSpec((1, tm, D), lambda b, i, j: (b, i, 0)),
        pl.BlockSpec((1, 1, D), lambda b, i, j: (b, 0, 0)),
        pl.BlockSpec((1, 1, D), lambda b, i, j: (b, 0, 1)),
        pl.BlockSpec((D, tn), lambda b, i, j: (0, j)),
        pl.BlockSpec((1, tn), lambda b, i, j: (0, j)),
    ]
    if n_f32_cols:
        n_bf16 = (N - n_f32_cols) // tn
        out_specs = [pl.BlockSpec((1, tm, tn), lambda b, i, j: (b, i, jnp.minimum(j, n_bf16 - 1))),
                     pl.BlockSpec((1, tm, tn), lambda b, i, j: (b, i, jnp.maximum(j - n_bf16, 0)))]
        out_shape = [jax.ShapeDtypeStruct((B, S, N - n_f32_cols), BF16),
                     jax.ShapeDtypeStruct((B, S, n_f32_cols), F32)]
    else:
        n_bf16 = None
        out_specs = pl.BlockSpec((1, tm, tn), lambda b, i, j: (b, i, j))
        out_shape = jax.ShapeDtypeStruct((B, S, N), BF16)
    return pl.pallas_call(
        functools.partial(_inproj_kernel, n_bf16=n_bf16),
        grid=(B, S // tm, N // tn),
        in_specs=in_specs,
        out_specs=out_specs,
        out_shape=out_shape,
        scratch_shapes=[pltpu.VMEM((tm, D), BF16)],
        compiler_params=_params("parallel", "parallel", "arbitrary"),
        name="in_proj",
    )(x, mod, mod, w_bf16, colscale)


N_POS_PIECES = 3


def _diff_kernel(slope_ref, q_ref, k_ref, v_ref, cx_ref, lam_ref, g_ref, o_ref,
                 kx1, kx2, qx1, qx2, bdiag, m1_ref, l1_ref, a1_ref, m2_ref, l2_ref, a2_ref, *, t, n_chunks, lam_init):
    h = pl.program_id(1)
    i = pl.program_id(2)
    slope2 = slope_ref[h]

    @pl.when(i == 0)
    def _():
        cx = cx_ref[0]
        kx1[:, :HEAD_DIM] = k_ref[0, :, :HEAD_DIM]
        kx1[:, HEAD_DIM:] = cx
        kx2[:, :HEAD_DIM] = k_ref[0, :, HEAD_DIM:]
        kx2[:, HEAD_DIM:] = cx
        d = lax.broadcasted_iota(I32, (t, t), 1) - lax.broadcasted_iota(I32, (t, t), 0)
        bdiag[...] = -slope2 * jnp.abs(d).astype(F32)

    q = q_ref[0]
    ones = jnp.where(lax.broadcasted_iota(I32, (t, HEAD_DIM), 1) < N_POS_PIECES, 1.0, 0.0)
    for side, sign in enumerate((1.0, 0.0, -1.0)):
        e = (sign * ones).astype(BF16)
        qx1[side, :, :HEAD_DIM] = q[:, :HEAD_DIM]
        qx1[side, :, HEAD_DIM:] = e
        qx2[side, :, :HEAD_DIM] = q[:, HEAD_DIM:]
        qx2[side, :, HEAD_DIM:] = e
    row_term = slope2 * (i * t + lax.broadcasted_iota(I32, (t, 1), 0)).astype(F32)

    m1_ref[...] = jnp.full(m1_ref.shape, NEG, F32)
    m2_ref[...] = jnp.full(m2_ref.shape, NEG, F32)
    l1_ref[...] = jnp.zeros(l1_ref.shape, F32)
    l2_ref[...] = jnp.zeros(l2_ref.shape, F32)
    a1_ref[...] = jnp.zeros(a1_ref.shape, F32)
    a2_ref[...] = jnp.zeros(a2_ref.shape, F32)

    def key_tile(c, side, row_sign, on_diagonal):
        k0 = pl.multiple_of(c * t, t)
        vc = v_ref[0, pl.ds(k0, t), :]
        shift = row_sign * row_term

        def one(qx, kx, m_ref, l_ref, a_ref):
            s = _nt_dot(qx[side], kx[pl.ds(k0, t), :])
            if on_diagonal:
                s = s + bdiag[...]
            m_old = m_ref[...]
            m_new = jnp.maximum(m_old, jnp.max(s, axis=-1, keepdims=True) + shift)
            p = jnp.exp2(s - (m_new - shift))
            alpha = jnp.exp2(m_old - m_new)
            l_ref[...] = alpha * l_ref[...] + jnp.sum(p, axis=-1, keepdims=True)
            a_ref[...] = alpha * a_ref[...] + jnp.dot(p.astype(BF16), vc, preferred_element_type=F32)
            m_ref[...] = m_new

        one(qx1, kx1, m1_ref, l1_ref, a1_ref)
        one(qx2, kx2, m2_ref, l2_ref, a2_ref)

    def before(c, carry):
        key_tile(c, 0, -1.0, False)
        return carry

    def after(c, carry):
        key_tile(c, 2, 1.0, False)
        return carry

    lax.fori_loop(0, i, before, 0)
    key_tile(i, 1, 0.0, True)
    lax.fori_loop(i + 1, n_chunks, after, 0)

    lv = lam_ref[...]
    s01 = jnp.sum(lv[0:1, :] * lv[1:2, :], axis=-1, keepdims=True)
    s23 = jnp.sum(lv[2:3, :] * lv[3:4, :], axis=-1, keepdims=True)
    lam = jnp.exp(s01) - jnp.exp(s23) + lam_init
    o = a1_ref[...] / l1_ref[...] - lam * (a2_ref[...] / l2_ref[...])
    ms = jnp.mean(o * o, axis=-1, keepdims=True)
    o = o * lax.rsqrt(ms + LN_EPS) * g_ref[...] * (1.0 - lam_init)
    o_ref[0] = o.astype(BF16)


def _diff_attention(proj, lam_vecs, subln_g, layer_idx):
    B, S, _ = proj.shape
    t = min(S, 1024)
    lam_init = 0.8 - 0.6 * math.exp(-0.3 * layer_idx)
    slopes2 = jnp.asarray(_alibi_slopes(A_HEADS) * LOG2E)
    rest = slopes2[:, None] * jnp.arange(S, dtype=F32)[None, :]
    pieces = []
    for _ in range(N_POS_PIECES):
        piece = rest.astype(BF16)
        pieces.append(piece)
        rest = rest - piece.astype(F32)
    cext = jnp.zeros((A_HEADS, S, HEAD_DIM), BF16).at[:, :, :N_POS_PIECES].set(jnp.stack(pieces, axis=-1))
    nq = A_QK // A_VDIM
    kern = functools.partial(_diff_kernel, t=t, n_chunks=S // t, lam_init=lam_init)
    return pl.pallas_call(
        kern,
        grid=(B, A_HEADS, S // t),
        in_specs=[
            pl.BlockSpec(memory_space=pltpu.SMEM),
            pl.BlockSpec((1, t, A_VDIM), lambda b, h, i: (b, i, h)),
            pl.BlockSpec((1, S, A_VDIM), lambda b, h, i: (b, 0, nq + h)),
            pl.BlockSpec((1, S, A_VDIM), lambda b, h, i: (b, 0, 2 * nq + h)),
            pl.BlockSpec((1, S, HEAD_DIM), lambda b, h, i: (h, 0, 0)),
            pl.BlockSpec((4, HEAD_DIM), lambda b, h, i: (0, 0)),
            pl.BlockSpec((1, A_VDIM), lambda b, h, i: (0, 0)),
        ],
        out_specs=pl.BlockSpec((1, t, A_VDIM), lambda b, h, i: (b, i, h)),
        out_shape=jax.ShapeDtypeStruct((B, S, A_V), BF16),
        scratch_shapes=[
            pltpu.VMEM((S, 2 * HEAD_DIM), BF16), pltpu.VMEM((S, 2 * HEAD_DIM), BF16),
            pltpu.VMEM((3, t, 2 * HEAD_DIM), BF16), pltpu.VMEM((3, t, 2 * HEAD_DIM), BF16),
            pltpu.VMEM((t, t), F32),
            pltpu.VMEM((t, 1), F32), pltpu.VMEM((t, 1), F32), pltpu.VMEM((t, A_VDIM), F32),
            pltpu.VMEM((t, 1), F32), pltpu.VMEM((t, 1), F32), pltpu.VMEM((t, A_VDIM), F32),
        ],
        compiler_params=_params("parallel", "parallel", "arbitrary"),
        name="diff_attn",
    )(slopes2, proj, proj, proj, cext, lam_vecs, subln_g.reshape(1, A_VDIM))


def _banded_kernel(slope_ref, sink_ref, q_ref, kp_ref, kc_ref, kn_ref, vp_ref, vc_ref, vn_ref, *rest,
                   tq, radius, n_q, group, n_rows, dist_scale, use_sink, want_lse):
    if want_lse:
        o_ref, lse_ref, kwin, vwin = rest
    else:
        o_ref, kwin, vwin = rest
        lse_ref = None
    t = pl.program_id(2)
    w = tq + 2 * radius
    kwin[0:radius, :] = kp_ref[0]
    kwin[radius:radius + tq, :] = kc_ref[0]
    kwin[radius + tq:w, :] = kn_ref[0]
    vwin[0:radius, :] = vp_ref[0]
    vwin[radius:radius + tq, :] = vc_ref[0]
    vwin[radius + tq:w, :] = vn_ref[0]

    ii = lax.broadcasted_iota(I32, (tq, w), 0)
    jj = lax.broadcasted_iota(I32, (tq, w), 1)
    rel = jnp.abs(jj - radius - ii)
    kpos = t * tq - radius + jj
    valid = (rel <= radius) & (kpos >= 0) & (kpos < n_rows)
    dist = rel.astype(F32) * float(dist_scale)
    lane = lax.broadcasted_iota(I32, (tq, HEAD_DIM), 1)
    lse_tile = jnp.zeros((tq, HEAD_DIM), F32)

    for h in range(n_q):
        hk = h // group
        qh = q_ref[0, :, h * HEAD_DIM:(h + 1) * HEAD_DIM]
        kh = kwin[:, hk * HEAD_DIM:(hk + 1) * HEAD_DIM]
        vh = vwin[:, hk * HEAD_DIM:(hk + 1) * HEAD_DIM]
        s = jnp.where(valid, _nt_dot(qh, kh) - slope_ref[h] * dist, NEG)
        m = jnp.max(s, axis=-1, keepdims=True)
        if use_sink:
            m = jnp.maximum(m, sink_ref[h])
        p = jnp.exp2(s - m)
        den = jnp.sum(p, axis=-1, keepdims=True)
        if use_sink:
            den = den + jnp.exp2(sink_ref[h] - m)
        o = jnp.dot(p.astype(BF16), vh, preferred_element_type=F32) / den
        o_ref[0, :, h * HEAD_DIM:(h + 1) * HEAD_DIM] = o.astype(BF16)
        if want_lse:
            lse_tile = jnp.where(lane == h, m + jnp.log2(den), lse_tile)
    if want_lse:
        lse_ref[0] = lse_tile


def _banded_attention(src, *, n_rows, n_res, src_cols, q_blk, k_blk, v_blk, n_q, group, radius,
                      dist_scale, slopes2, sink2, want_lse):
    B = src.shape[0]
    qw = n_q * HEAD_DIM
    kw = (n_q // group) * HEAD_DIM
    tq = min(n_rows, 256)
    nt = n_rows // tq
    per_t = tq // radius
    last_halo = n_rows // radius - 1
    qpg = src_cols // qw
    kpg = src_cols // kw

    def q_map(b, r, t):
        return (b, t, r * qpg + q_blk)

    def cur_map(blk):
        return lambda b, r, t: (b, t, r * kpg + blk)

    def prev_map(blk):
        return lambda b, r, t: (b, jnp.maximum(t * per_t - 1, 0), r * kpg + blk)

    def next_map(blk):
        return lambda b, r, t: (b, jnp.minimum((t + 1) * per_t, last_halo), r * kpg + blk)

    out_shape = [jax.ShapeDtypeStruct((B, n_rows, n_res * qw), BF16)]
    out_specs = [pl.BlockSpec((1, tq, qw), lambda b, r, t: (b, t, r))]
    if want_lse:
        out_shape.append(jax.ShapeDtypeStruct((B, n_rows, n_res * HEAD_DIM), F32))
        out_specs.append(pl.BlockSpec((1, tq, HEAD_DIM), lambda b, r, t: (b, t, r)))
    kern = functools.partial(_banded_kernel, tq=tq, radius=radius, n_q=n_q, group=group, n_rows=n_rows,
                             dist_scale=dist_scale, use_sink=sink2 is not None, want_lse=want_lse)
    if sink2 is None:
        sink2 = jnp.zeros((n_q,), F32)
    return pl.pallas_call(
        kern,
        grid=(B, n_res, nt),
        in_specs=[
            pl.BlockSpec(memory_space=pltpu.SMEM),
            pl.BlockSpec(memory_space=pltpu.SMEM),
            pl.BlockSpec((1, tq, qw), q_map),
            pl.BlockSpec((1, radius, kw), prev_map(k_blk)),
            pl.BlockSpec((1, tq, kw), cur_map(k_blk)),
            pl.BlockSpec((1, radius, kw), next_map(k_blk)),
            pl.BlockSpec((1, radius, kw), prev_map(v_blk)),
            pl.BlockSpec((1, tq, kw), cur_map(v_blk)),
            pl.BlockSpec((1, radius, kw), next_map(v_blk)),
        ],
        out_specs=out_specs,
        out_shape=out_shape,
        scratch_shapes=[pltpu.VMEM((tq + 2 * radius, kw), BF16), pltpu.VMEM((tq + 2 * radius, kw), BF16)],
        compiler_params=_params("parallel", "parallel", "parallel"),
        name="banded_attn",
    )(slopes2, sink2, src, src, src, src, src, src, src)


def _band_bias(tq, wlen, offset, radius, slope_dist):
    rel = jnp.abs(offset + lax.broadcasted_iota(I32, (tq, wlen), 1) - lax.broadcasted_iota(I32, (tq, wlen), 0))
    return jnp.where(rel <= radius, -slope_dist * rel.astype(F32), NEG)


def _band_tile(qf, kf, vf, bias):
    s = _nt_dot(qf, kf) + bias
    m = jnp.max(s, axis=-1, keepdims=True)
    p = jnp.exp2(s - m)
    return m, jnp.sum(p, axis=-1, keepdims=True), jnp.dot(p.astype(BF16), vf, preferred_element_type=F32)


TILE_UNROLL = 4


def _loop_tiles(n, tile_fn):
    unroll = TILE_UNROLL if n % TILE_UNROLL == 0 else 1

    def body(i, carry):
        for u in range(unroll):
            tile_fn(i * unroll + u)
        return carry

    lax.fori_loop(0, n // unroll, body, 0)


def _band_geometry(seq, tq, radius, dil):
    n_rows = seq // dil
    tile = min(tq, n_rows)
    nt = n_rows // tile
    wlen = min(tile + 2 * radius, n_rows)
    offsets = (0,) if nt == 1 else (0, -radius, tile - wlen)
    return n_rows, tile, nt, wlen, offsets


def _dilated_kernel(slope_ref, q_ref, k_ref, v_ref, o_ref, acc_s, m_s, l_s, bias_s, *, seq, tq, branches):
    slope = slope_ref[pl.program_id(1)]
    lanes = (tq, HEAD_DIM)

    first_bias = []
    n_bias = 0
    for radius, dil in branches:
        _, tile, _, wlen, offsets = _band_geometry(seq, tq, radius, dil)
        first_bias.append(n_bias)
        for off in offsets:
            bias_s[n_bias, 0:tile, 0:wlen] = _band_bias(tile, wlen, off, radius, slope * dil)
            n_bias += 1

    def tile_window(bi, t):
        radius, dil = branches[bi]
        n_rows, tile, nt, wlen, _ = _band_geometry(seq, tq, radius, dil)
        q0 = t * tile
        k0 = jnp.clip(q0 - radius, 0, n_rows - wlen)
        case = 0 if nt == 1 else jnp.where(t == 0, 0, jnp.where(t == nt - 1, 2, 1))
        return q0, k0, bias_s[first_bias[bi] + case, 0:tile, 0:wlen]

    for bi, (radius, dil) in enumerate(branches[:-1]):
        _, tile, nt, wlen, _ = _band_geometry(seq, tq, radius, dil)

        def one_tile(it, bi=bi, dil=dil, tile=tile, nt=nt, wlen=wlen):
            r = it // nt
            q0, k0, bias = tile_window(bi, it % nt)
            qrows = pl.ds(r + dil * q0, tile, stride=dil)
            krows = pl.ds(r + dil * k0, wlen, stride=dil)
            m, l, acc = _band_tile(q_ref[0, qrows, :].astype(BF16), k_ref[0, krows, :].astype(BF16),
                                   v_ref[0, krows, :].astype(BF16), bias)
            acc_s[bi, qrows, :] = acc
            m_s[bi, qrows, :] = jnp.broadcast_to(m, (tile, HEAD_DIM))
            l_s[bi, qrows, :] = jnp.broadcast_to(l, (tile, HEAD_DIM))

        _loop_tiles(dil * nt, one_tile)

    assert branches[-1][1] == 1
    wlen = _band_geometry(seq, tq, *branches[-1])[3]

    def dense(t):
        q0, k0, bias = tile_window(len(branches) - 1, t)
        qrows = pl.ds(pl.multiple_of(q0, tq), tq)
        krows = pl.ds(pl.multiple_of(k0, 8), wlen)
        m, l, acc = _band_tile(q_ref[0, qrows, :].astype(BF16), k_ref[0, krows, :].astype(BF16),
                               v_ref[0, krows, :].astype(BF16), bias)
        ms = [jnp.broadcast_to(m, lanes)] + [m_s[bi, qrows, :] for bi in range(len(branches) - 1)]
        ls = [jnp.broadcast_to(l, lanes)] + [l_s[bi, qrows, :] for bi in range(len(branches) - 1)]
        accs = [acc] + [acc_s[bi, qrows, :] for bi in range(len(branches) - 1)]
        top = functools.reduce(jnp.maximum, ms)
        es = [jnp.exp2(mi - top) for mi in ms]
        num = sum(e * a for e, a in zip(es, accs))
        den = sum(e * li for e, li in zip(es, ls))
        o_ref[0, qrows, :] = (num / den).astype(BF16)

    _loop_tiles(seq // tq, dense)


def _dilated_mixture(projf, slopes2):
    B, S, _ = projf.shape
    branches = tuple((w // (2 * d), d) for w, d in sorted(B_BRANCHES, key=lambda wd: -wd[1]))
    tq = min(S, 256)
    kern = functools.partial(_dilated_kernel, seq=S, tq=tq, branches=branches)
    nb = len(branches) - 1
    geo = [_band_geometry(S, tq, radius, dil) for radius, dil in branches]
    bias_shape = (sum(len(g[4]) for g in geo), tq, max(g[3] for g in geo))
    col = lambda part: (lambda b, h: (b, 0, part * B_HEADS + h))
    return pl.pallas_call(
        kern,
        grid=(B, B_HEADS),
        in_specs=[pl.BlockSpec(memory_space=pltpu.SMEM)] + [pl.BlockSpec((1, S, HEAD_DIM), col(p)) for p in range(3)],
        out_specs=pl.BlockSpec((1, S, HEAD_DIM), col(0)),
        out_shape=jax.ShapeDtypeStruct((B, S, B_W), BF16),
        scratch_shapes=[pltpu.VMEM((nb, S, HEAD_DIM), F32)] * 3 + [pltpu.VMEM(bias_shape, F32)],
        compiler_params=_params("parallel", "parallel"),
        name="dilated_mix",
    )(slopes2, projf, projf, projf)


def _post_attention(y, x, g1, lng, lnb, sc2, sh2, rwh_ref, rwl_ref, x1_ref, hp_ref, lg_ref, alpha):
    z = alpha * x + g1 * y
    mu = jnp.mean(z, axis=-1, keepdims=True)
    zc = z - mu
    var = jnp.mean(zc * zc, axis=-1, keepdims=True)
    x1 = zc * lax.rsqrt(var + LN_EPS) * lng + lnb
    x1_ref[0] = x1
    h2 = x1 * (1.0 + sc2) + sh2
    hb = h2.astype(BF16)
    hf = hb.astype(F32)
    bits = lax.bitcast_convert_type(hf, U32)
    half = bits.shape[1] // 2
    packed = bits[:, :half] | (bits[:, half:] >> 16)
    nck = half // HEAD_DIM
    for ck in range(nck):
        hp_ref[0, pl.ds(ck, bits.shape[0], stride=nck), :] = packed[:, ck * HEAD_DIM:(ck + 1) * HEAD_DIM]
    lo = (h2 - hf).astype(BF16)
    rwh = rwh_ref[...]
    lg_ref[0] = (jnp.dot(hb, rwh, preferred_element_type=F32)
                 + jnp.dot(hb, rwl_ref[...], preferred_element_type=F32)
                 + jnp.dot(lo, rwh, preferred_element_type=F32))


def _outproj_ab_kernel(ya_ref, yb_ref, w_ref, x_ref, g1_ref, lng_ref, lnb_ref, sh2_ref, sc2_ref, rwh_ref, rwl_ref,
                       x1_ref, hp_ref, lg_ref, *, alpha):
    y = (jnp.dot(ya_ref[0], w_ref[0:A_V, :], preferred_element_type=F32)
         + jnp.dot(yb_ref[0], w_ref[A_V:A_V + B_W, :], preferred_element_type=F32))
    _post_attention(y, x_ref[0], g1_ref[0], lng_ref[...], lnb_ref[...], sc2_ref[0], sh2_ref[0],
                    rwh_ref, rwl_ref, x1_ref, hp_ref, lg_ref, alpha)


def _outproj_c_kernel(o_ref, w_ref, x_ref, g1_ref, lng_ref, lnb_ref, sh2_ref, sc2_ref, rwh_ref, rwl_ref,
                      x1_ref, hp_ref, lg_ref, *, alpha):
    y = jnp.dot(o_ref[0], w_ref[...], preferred_element_type=F32)
    _post_attention(y, x_ref[0], g1_ref[0], lng_ref[...], lnb_ref[...], sc2_ref[0], sh2_ref[0],
                    rwh_ref, rwl_ref, x1_ref, hp_ref, lg_ref, alpha)


def _out_projection(attn_inputs, w_bf16, x, mod, lng, lnb, rw_hi, rw_lo, alpha, mixer):
    B, S, D = x.shape
    E = rw_hi.shape[1]
    tm = min(S, 512)
    row = lambda b, i: (b, i, 0)
    const2 = lambda b, i: (0, 0)
    tail_specs = [
        pl.BlockSpec(w_bf16.shape, const2),
        pl.BlockSpec((1, tm, D), row),
        pl.BlockSpec((1, 1, D), lambda b, i: (b, 0, 2)),
        pl.BlockSpec((1, D), const2),
        pl.BlockSpec((1, D), const2),
        pl.BlockSpec((1, 1, D), lambda b, i: (b, 0, 3)),
        pl.BlockSpec((1, 1, D), lambda b, i: (b, 0, 4)),
        pl.BlockSpec((D, E), const2),
        pl.BlockSpec((D, E), const2),
    ]
    tail_args = (w_bf16, x, mod, lng.reshape(1, D), lnb.reshape(1, D), mod, mod, rw_hi, rw_lo)
    nck = D // 2 // HEAD_DIM
    out_shape = [jax.ShapeDtypeStruct((B, S, D), F32), jax.ShapeDtypeStruct((B, S * nck, HEAD_DIM), U32),
                 jax.ShapeDtypeStruct((B, S, E), F32)]
    out_specs = [pl.BlockSpec((1, tm, D), row), pl.BlockSpec((1, tm * nck, HEAD_DIM), row),
                 pl.BlockSpec((1, tm, E), row)]
    if mixer == "ab":
        ya, yb = attn_inputs
        in_specs = [pl.BlockSpec((1, tm, A_V), row), pl.BlockSpec((1, tm, B_W), row)] + tail_specs
        kern = functools.partial(_outproj_ab_kernel, alpha=alpha)
        scratch = []
        args = (ya, yb) + tail_args
    else:
        (o,) = attn_inputs
        in_specs = [pl.BlockSpec((1, tm, o.shape[2]), row)] + tail_specs
        kern = functools.partial(_outproj_c_kernel, alpha=alpha)
        scratch = []
        args = (o,) + tail_args
    return pl.pallas_call(
        kern,
        grid=(B, S // tm),
        in_specs=in_specs,
        out_specs=out_specs,
        out_shape=out_shape,
        scratch_shapes=scratch,
        compiler_params=_params("parallel", "parallel"),
        name="out_proj_" + mixer,
    )(*args)


def _excl_cumsum_lanes(x, tri):
    n = x.shape[1] // HEAD_DIM
    carry = jnp.zeros((x.shape[0], 1), F32)
    outs = []
    for c in range(n):
        xc = x[:, c * HEAD_DIM:(c + 1) * HEAD_DIM]
        outs.append(jnp.dot(xc.astype(BF16), tri, preferred_element_type=F32) + carry)
        carry = carry + jnp.sum(xc, axis=1, keepdims=True)
    return jnp.concatenate(outs, axis=1)


def _tables_kernel(lg_ref, cnt_ref, dst_ref, gk_ref, *, n_exp, cap):
    lg = lg_ref[0]
    S = lg.shape[1]
    ex = jnp.exp(lg - jnp.max(lg, axis=0, keepdims=True))
    aff = ex / jnp.sum(ex, axis=0, keepdims=True)
    bits = lax.bitcast_convert_type(aff, I32)

    def search(i, cur):
        cand = cur | jnp.left_shift(jnp.int32(1), 30 - i)
        cnt = jnp.sum(jnp.where(bits >= cand, 1.0, 0.0), axis=1, keepdims=True)
        return jnp.where(cnt >= cap, cand, cur)

    thr = lax.fori_loop(0, 31, search, jnp.zeros((n_exp, 1), I32))
    tri = jnp.where(lax.broadcasted_iota(I32, (HEAD_DIM, HEAD_DIM), 0)
                    < lax.broadcasted_iota(I32, (HEAD_DIM, HEAD_DIM), 1), 1.0, 0.0).astype(BF16)
    gt = bits > thr
    eq = jnp.where(bits == thr, 1.0, 0.0)
    need = cap - jnp.sum(jnp.where(gt, 1.0, 0.0), axis=1, keepdims=True)
    sel = jnp.where(gt, 1.0, jnp.where(_excl_cumsum_lanes(eq, tri) < need, eq, 0.0))
    pos = _excl_cumsum_lanes(sel, tri)

    slot = lax.broadcasted_iota(I32, (n_exp, S), 0).astype(F32)
    run = jnp.zeros((1, S), F32)
    dst = jnp.zeros((n_exp, S), F32)
    gk = jnp.zeros((n_exp, S), F32)
    for e in range(n_exp):
        se = sel[e:e + 1, :]
        hit = (slot == run) & (se > 0.0)
        dst = jnp.where(hit, pos[e:e + 1, :] + float(e * cap), dst)
        gk = jnp.where(hit, aff[e:e + 1, :], gk)
        run = run + se
    cnt_ref[0] = run.astype(I32)
    dst_ref[0] = dst.astype(I32)
    gk_ref[0] = gk


def _routing_tables(logits_t, cap):
    B, E, S = logits_t.shape
    kern = functools.partial(_tables_kernel, n_exp=E, cap=cap)
    return pl.pallas_call(
        kern,
        grid=(B,),
        in_specs=[pl.BlockSpec((1, E, S), lambda b: (b, 0, 0))],
        out_specs=[pl.BlockSpec((1, 1, S), lambda b: (b, 0, 0)), pl.BlockSpec((1, E, S), lambda b: (b, 0, 0)),
                   pl.BlockSpec((1, E, S), lambda b: (b, 0, 0))],
        out_shape=[jax.ShapeDtypeStruct((B, 1, S), I32), jax.ShapeDtypeStruct((B, E, S), I32),
                   jax.ShapeDtypeStruct((B, E, S), F32)],
        compiler_params=_params("parallel"),
        name="routing_tables",
    )(logits_t)


def _index_kernel(cnt_ref, dst_ref, gk_ref, off_ref, tok_ref, grow_ref, pos_ref, *, tc, n_chunks):
    tok0 = (pl.program_id(0) * n_chunks + pl.program_id(1)) * tc
    base = off_ref[0]

    def per_token(t, n):
        def per_choice(k, n):
            code = dst_ref[k * tc + t]
            tok_ref[code] = tok0 + t
            grow_ref[code] = gk_ref[k * tc + t]
            pos_ref[code] = base + n
            return n + 1

        return lax.fori_loop(0, cnt_ref[t], per_choice, n)

    lax.fori_loop(0, tc, per_token, 0)


def _expert_order_lists(cnt, dst, gk, off, n_exp, cap):
    B, _, S = cnt.shape
    tc = min(S, 1024)
    n_chunks = S // tc
    by_chunk = lambda a: a.reshape(B, n_exp, n_chunks, tc).swapaxes(1, 2).reshape(B * S * n_exp)
    kern = functools.partial(_index_kernel, tc=tc, n_chunks=n_chunks)
    smem = functools.partial(pl.BlockSpec, memory_space=pltpu.SMEM)
    per_chunk = lambda b, c: (b * n_chunks + c,)
    out_spec = smem((n_exp * cap,), lambda b, c: (b,))
    tok, grow, pos = pl.pallas_call(
        kern,
        grid=(B, n_chunks),
        in_specs=[smem((tc,), per_chunk), smem((n_exp * tc,), per_chunk), smem((n_exp * tc,), per_chunk),
                  smem((tc,), per_chunk)],
        out_specs=[out_spec, out_spec, out_spec],
        out_shape=[jax.ShapeDtypeStruct((B * n_exp * cap,), I32), jax.ShapeDtypeStruct((B * n_exp * cap,), F32),
                   jax.ShapeDtypeStruct((B * n_exp * cap,), I32)],
        compiler_params=_params("arbitrary", "arbitrary"),
        name="moe_index",
    )(cnt.reshape(B * S), by_chunk(dst), by_chunk(gk), off.reshape(B * S))
    expert_major = lambda a: a.reshape(B, n_exp, cap).swapaxes(0, 1).reshape(n_exp * B * cap)
    return expert_major(tok), expert_major(grow), expert_major(pos)


def _token_rows(t, nck):
    start = t * nck
    return pl.ds(pl.multiple_of(start, nck) if nck > 1 else start, nck)


def _unpack_rows(xp_ref, first, xs_ref):
    rows, d = xs_ref.shape
    half = d // 2
    nck = half // HEAD_DIM
    for ck in range(nck):
        u = xp_ref[pl.ds(first * nck + ck, rows, stride=nck), :]
        lo = ck * HEAD_DIM
        xs_ref[:, lo:lo + HEAD_DIM] = lax.bitcast_convert_type(u & jnp.uint32(0xFFFF0000), F32).astype(BF16)
        xs_ref[:, half + lo:half + lo + HEAD_DIM] = lax.bitcast_convert_type(u << 16, F32).astype(BF16)


def _pack_pairs(hi, lo):
    hb = lax.bitcast_convert_type(hi.astype(BF16).astype(F32), U32)
    lb = lax.bitcast_convert_type(lo.astype(BF16).astype(F32), U32)
    return hb | (lb >> 16)


ROW_UNROLL = 8


def _ffn_up_kernel(tok_ref, tokn_ref, hp_ref, wg_ref, wu_ref, o_ref, xraw, xs_ref, sem, *, tm, mt, n_tiles, nck):
    tile = pl.program_id(0) * mt + pl.program_id(1)
    par = tile % 2

    def gather(idx_ref, slot):
        def body(i, carry):
            for u in range(ROW_UNROLL):
                r = i * ROW_UNROLL + u
                pltpu.make_async_copy(hp_ref.at[_token_rows(idx_ref[r], nck)],
                                      xraw.at[_token_rows(slot * tm + r, nck)], sem.at[slot]).start()
            return carry

        lax.fori_loop(0, tm // ROW_UNROLL, body, 0)

    @pl.when(pl.program_id(2) == 0)
    def _():
        @pl.when(tile == 0)
        def _():
            gather(tok_ref, 0)

        pltpu.make_async_copy(hp_ref.at[pl.ds(0, tm * nck)], xraw.at[pl.ds(par * tm * nck, tm * nck)],
                              sem.at[par]).wait()

        @pl.when(tile + 1 < n_tiles)
        def _():
            gather(tokn_ref, 1 - par)

        _unpack_rows(xraw, par * tm, xs_ref)

    x = xs_ref[...]
    g = jnp.dot(x, wg_ref[0, 0].astype(BF16), preferred_element_type=F32)
    u = jnp.dot(x, wu_ref[0, 0].astype(BF16), preferred_element_type=F32)
    o_ref[...] = (g * jax.nn.sigmoid(g) * u).astype(BF16)


def _ffn_up(hp, tok_rows, w_gate, w_up, layer, rows_per_expert):
    _, E, D, Fh = w_gate.shape
    tm = min(rows_per_expert, 2048)
    mt = rows_per_expert // tm
    n_tiles = E * mt
    tn = min(Fh, 256)
    nck = D // 2 // HEAD_DIM
    kern = functools.partial(_ffn_up_kernel, tm=tm, mt=mt, n_tiles=n_tiles, nck=nck)
    smem = functools.partial(pl.BlockSpec, memory_space=pltpu.SMEM)
    return pl.pallas_call(
        kern,
        grid=(E, mt, Fh // tn),
        in_specs=[smem((tm,), lambda e, m, n: (e * mt + m,)),
                  smem((tm,), lambda e, m, n: (jnp.minimum(e * mt + m + 1, n_tiles - 1),)),
                  pl.BlockSpec(memory_space=pl.ANY),
                  pl.BlockSpec((1, 1, D, tn), lambda e, m, n: (layer, e, 0, n)),
                  pl.BlockSpec((1, 1, D, tn), lambda e, m, n: (layer, e, 0, n))],
        out_specs=pl.BlockSpec((tm, tn), lambda e, m, n: (e * mt + m, n)),
        out_shape=jax.ShapeDtypeStruct((E * rows_per_expert, Fh), BF16),
        scratch_shapes=[pltpu.VMEM((2 * tm * nck, HEAD_DIM), U32), pltpu.VMEM((tm, D), BF16),
                        pltpu.SemaphoreType.DMA((2,))],
        compiler_params=_params("arbitrary", "arbitrary", "arbitrary"),
        name="ffn_up",
    )(tok_rows, tok_rows, hp, w_gate, w_up)


def _ffn_down_kernel(pos_ref, h_ref, wa_ref, wb_ref, g_ref, yt_ref, ybuf, sem, *, n_steps, cps, tm, mt, n_tiles,
                     nck):
    tile = pl.program_id(0) * mt + pl.program_id(1)
    par = tile % 2
    h = h_ref[...]
    g = g_ref[...]
    ya = jnp.dot(h, wa_ref[0, 0].astype(BF16), preferred_element_type=F32) * g
    yb = jnp.dot(h, wb_ref[0, 0].astype(BF16), preferred_element_type=F32) * g
    packed = _pack_pairs(ya, yb)
    n = pl.program_id(2)
    for k in range(n_steps):
        @pl.when(n == k)
        def _():
            for cc in range(cps):
                ybuf[pl.ds(par * tm * nck + k * cps + cc, tm, stride=nck), :] = (
                    packed[:, cc * HEAD_DIM:(cc + 1) * HEAD_DIM])

    def scatter_done(slot):
        return pltpu.make_async_copy(ybuf.at[pl.ds(slot * tm * nck, tm * nck)], yt_ref.at[pl.ds(0, tm * nck)],
                                     sem.at[slot])

    @pl.when(n == n_steps - 1)
    def _():
        @pl.when(tile > 0)
        def _():
            scatter_done(1 - par).wait()

        def body(i, carry):
            for u in range(ROW_UNROLL):
                r = i * ROW_UNROLL + u
                pltpu.make_async_copy(ybuf.at[_token_rows(par * tm + r, nck)],
                                      yt_ref.at[_token_rows(pos_ref[r], nck)], sem.at[par]).start()
            return carry

        lax.fori_loop(0, tm // ROW_UNROLL, body, 0)

        @pl.when(tile == n_tiles - 1)
        def _():
            scatter_done(par).wait()


def _ffn_down(hid, w_down, layer, gate_col, pos_rows, rows_per_expert):
    _, E, Fh, D = w_down.shape
    tm = min(rows_per_expert, 2048)
    mt = rows_per_expert // tm
    n_tiles = E * mt
    half = D // 2
    tnh = min(half, 256)
    n_steps = half // tnh
    nck = half // HEAD_DIM
    kern = functools.partial(_ffn_down_kernel, n_steps=n_steps, cps=tnh // HEAD_DIM, tm=tm, mt=mt, n_tiles=n_tiles,
                             nck=nck)
    return pl.pallas_call(
        kern,
        grid=(E, mt, n_steps),
        in_specs=[pl.BlockSpec((tm,), lambda e, m, n: (e * mt + m,), memory_space=pltpu.SMEM),
                  pl.BlockSpec((tm, Fh), lambda e, m, n: (e * mt + m, 0)),
                  pl.BlockSpec((1, 1, Fh, tnh), lambda e, m, n: (layer, e, 0, n)),
                  pl.BlockSpec((1, 1, Fh, tnh), lambda e, m, n: (layer, e, 0, n + n_steps)),
                  pl.BlockSpec((tm, 1), lambda e, m, n: (e * mt + m, 0))],
        out_specs=pl.BlockSpec(memory_space=pl.ANY),
        out_shape=jax.ShapeDtypeStruct((E * rows_per_expert * nck, HEAD_DIM), U32),
        scratch_shapes=[pltpu.VMEM((2 * tm * nck, HEAD_DIM), U32), pltpu.SemaphoreType.DMA((2,))],
        compiler_params=_params("arbitrary", "arbitrary", "arbitrary"),
        name="ffn_down",
    )(pos_rows, hid, w_down, w_down, gate_col)


def _combine_kernel(offt_ref, x_ref, off_ref, cnt_ref, g2_ref, lng_ref, lnb_ref, yt_ref, o_ref,
                    buf, rows, acc, sem, *, tm, ch, n_total, n_tiles, n_tiles_s, alpha, nck):
    tile = pl.program_id(0) * n_tiles_s + pl.program_id(1)
    o0 = offt_ref[tile]
    o1 = offt_ref[tile + 1]
    n_chunks = jnp.maximum((o1 - o0 + ch - 1) // ch, 1)
    lo_col = off_ref[0]
    hi_col = lo_col + cnt_ref[0]
    lane = lax.broadcasted_iota(I32, (tm, ch), 1)
    acc[...] = jnp.zeros(acc.shape, F32)

    def fetch(want, slot):
        start = jnp.minimum(want, n_total - ch) * nck
        if nck > 1:
            start = pl.multiple_of(start, nck)
        return pltpu.make_async_copy(yt_ref.at[pl.ds(start, ch * nck)], buf.at[pl.ds(slot * ch * nck, ch * nck)],
                                     sem.at[slot])

    @pl.when(tile == 0)
    def _():
        fetch(o0, 0).start()

    def chunk(c, carry):
        slot = c % 2
        want = o0 + c * ch
        fetch(want, slot).wait()

        @pl.when(c + 1 < n_chunks)
        def _():
            fetch(want + ch, 1 - slot).start()

        _unpack_rows(buf, slot * ch, rows)
        row = lane + jnp.minimum(want, n_total - ch)
        own = (row >= lo_col) & (row < hi_col) & (row >= want)
        acc[...] += jnp.dot(jnp.where(own, 1.0, 0.0).astype(BF16), rows[...], preferred_element_type=F32)
        return carry

    lax.fori_loop(0, n_chunks, chunk, 0)

    @pl.when(tile + 1 < n_tiles)
    def _():
        fetch(o1, 0).start()

    z = alpha * x_ref[0] + g2_ref[0] * acc[...]
    mu = jnp.mean(z, axis=-1, keepdims=True)
    zc = z - mu
    var = jnp.mean(zc * zc, axis=-1, keepdims=True)
    o_ref[0] = zc * lax.rsqrt(var + LN_EPS) * lng_ref[...] + lnb_ref[...]


def _combine(x1, mod, lng, lnb, yt, off, cnt, alpha):
    B, S, D = x1.shape
    nck = D // 2 // HEAD_DIM
    n_total = yt.shape[0] // nck
    tm = min(S, 256)
    ch = min(n_total, 256)
    n_tiles_s = S // tm
    off_flat = off.reshape(B * S)
    offt = jnp.concatenate([off_flat[::tm], jnp.full((1,), n_total, I32)])
    kern = functools.partial(_combine_kernel, tm=tm, ch=ch, n_total=n_total, n_tiles=B * n_tiles_s,
                             n_tiles_s=n_tiles_s, alpha=alpha, nck=nck)
    grid_spec = pltpu.PrefetchScalarGridSpec(
        num_scalar_prefetch=1,
        grid=(B, n_tiles_s),
        in_specs=[
            pl.BlockSpec((1, tm, D), lambda b, i, o: (b, i, 0)),
            pl.BlockSpec((1, tm, 1), lambda b, i, o: (b, i, 0)),
            pl.BlockSpec((1, tm, 1), lambda b, i, o: (b, i, 0)),
            pl.BlockSpec((1, 1, D), lambda b, i, o: (b, 0, 5)),
            pl.BlockSpec((1, D), lambda b, i, o: (0, 0)),
            pl.BlockSpec((1, D), lambda b, i, o: (0, 0)),
            pl.BlockSpec(memory_space=pl.ANY),
        ],
        out_specs=pl.BlockSpec((1, tm, D), lambda b, i, o: (b, i, 0)),
        scratch_shapes=[pltpu.VMEM((2 * ch * nck, HEAD_DIM), U32), pltpu.VMEM((ch, D), BF16),
                        pltpu.VMEM((tm, D), F32), pltpu.SemaphoreType.DMA((2,))],
    )
    return pl.pallas_call(
        kern,
        grid_spec=grid_spec,
        out_shape=jax.ShapeDtypeStruct((B, S, D), F32),
        compiler_params=_params("arbitrary", "arbitrary"),
        name="moe_combine",
    )(offt, x1, off.reshape(B, S, 1), cnt.reshape(B, S, 1), mod, lng.reshape(1, D), lnb.reshape(1, D), yt)


def _moe_sublayer(x1, hp, logits, mod, lng, lnb, w_gate, w_up, w_down, layer, alpha):
    B, S, D = x1.shape
    E = logits.shape[-1]
    cap = EC_CAPACITY_FACTOR * S // E
    cnt, dst, gk = _routing_tables(jnp.swapaxes(logits, 1, 2), cap)
    cflat = cnt.reshape(B * S)
    off = (jnp.cumsum(cflat) - cflat).astype(I32).reshape(B, 1, S)
    tok, grow, pos = _expert_order_lists(cnt, dst, gk, off, E, cap)
    hid = _ffn_up(hp.reshape(-1, HEAD_DIM), tok, w_gate, w_up, layer, B * cap)
    yt = _ffn_down(hid, w_down, layer, grow.reshape(E * B * cap, 1), pos, B * cap)
    return _combine(x1, mod, lng, lnb, yt, off, cnt, alpha)


def _split_bf16(w):
    hi = w.astype(BF16)
    return hi, (w - hi.astype(F32)).astype(BF16)


def kernel(x, c, ada_w, ada_b, ln_g, ln_b, ab_w_in, ab_w_out, diff_lambda, diff_subln_g, c_w_in, c_w_out,
           c_sink, router_w, w_gate, w_up, w_down):
    B, S, D = x.shape
    depth = ada_w.shape[0]
    alpha = (2.0 * depth) ** 0.25
    qscale = HEAD_DIM ** -0.5 * LOG2E
    mod_all = _modulation(c, ada_w, ada_b)

    ab_scale = np.ones((1, AB_IN), np.float32)
    ab_scale[:, :A_QK] = qscale
    ab_scale[:, 2 * A_QK + A_V:2 * A_QK + A_V + B_W] = qscale
    c_scale = np.ones((1, C_IN), np.float32)
    c_scale[:, :C_QW] = qscale

    for l in range(depth):
        mod = mod_all[l][:, None, :]
        i = l // 2
        rw_hi, rw_lo = _split_bf16(router_w[l])
        if l % 2 == 0:
            proj, projf = _in_projection(x, mod, ab_w_in[i].astype(BF16), jnp.asarray(ab_scale), n_f32_cols=3 * B_W)
            ya = _diff_attention(proj, diff_lambda[i], diff_subln_g[i], l)
            yb = _dilated_mixture(projf, jnp.asarray(_alibi_slopes(B_HEADS) * LOG2E))
            x1, hp, logits = _out_projection((ya, yb), ab_w_out[i].astype(BF16), x, mod,
                                             ln_g[l, 0], ln_b[l, 0], rw_hi, rw_lo, alpha, "ab")
        else:
            proj = _in_projection(x, mod, c_w_in[i].astype(BF16), jnp.asarray(c_scale))
            (o,) = _banded_attention(
                proj, n_rows=S, n_res=1, src_cols=C_IN, q_blk=0, k_blk=C_QW // C_KVW, v_blk=C_QW // C_KVW + 1,
                n_q=C_Q_HEADS, group=C_Q_HEADS // C_KV_HEADS, radius=C_RADIUS, dist_scale=1,
                slopes2=jnp.asarray(_alibi_slopes(C_Q_HEADS) * LOG2E), sink2=c_sink[i] * LOG2E, want_lse=False)
            x1, hp, logits = _out_projection((o,), c_w_out[i].astype(BF16), x, mod, ln_g[l, 0], ln_b[l, 0],
                                             rw_hi, rw_lo, alpha, "c")
        x = _moe_sublayer(x1, hp, logits, mod, ln_g[l, 1], ln_b[l, 1], w_gate, w_up, w_down, l, alpha)
    return x
```

```python
import functools
import math

import numpy as np
import jax
import jax.numpy as jnp
from jax import lax
from jax.experimental import pallas as pl
from jax.experimental.pallas import tpu as pltpu

F32 = jnp.float32
BF16 = jnp.bfloat16
I32 = jnp.int32
U32 = jnp.uint32

HEAD_DIM = 128
A_HEADS = 4
A_VDIM = 2 * HEAD_DIM
B_HEADS = 8
B_BRANCHES = ((128, 1), (512, 4), (2048, 16))
C_Q_HEADS = 16
C_KV_HEADS = 4
C_RADIUS = 128
EC_CAPACITY_FACTOR = 2
LN_EPS = 1e-5
NEG = -1e30
LOG2E = 1.4426950408889634

A_QK = A_HEADS * 2 * HEAD_DIM
A_V = A_HEADS * A_VDIM
B_W = B_HEADS * HEAD_DIM
AB_IN = 2 * A_QK + A_V + 3 * B_W
C_QW = C_Q_HEADS * HEAD_DIM
C_KVW = C_KV_HEADS * HEAD_DIM
C_IN = C_QW + 2 * C_KVW

VMEM_LIMIT_BYTES = 56 * 1024 * 1024


def _params(*sem):
    return pltpu.CompilerParams(dimension_semantics=sem, vmem_limit_bytes=VMEM_LIMIT_BYTES)


def _tile(n, preferred):
    t = min(n, preferred)
    while n % t:
        t //= 2
    return t


def _alibi_slopes(n):
    return np.array([2.0 ** (-8.0 * (i + 1) / n) for i in range(n)], dtype=np.float32)


def _nt_dot(a, b):
    return lax.dot_general(a, b, (((1,), (1,)), ((), ())), preferred_element_type=F32)


def _mod_kernel(c_ref, w_ref, b_ref, o_ref):
    c = c_ref[...]
    cs = (c * jax.nn.sigmoid(c)).astype(BF16)
    o_ref[0] = jnp.dot(cs, w_ref[0].astype(BF16), preferred_element_type=F32) + b_ref[0]


def _modulation(c, ada_w, ada_b):
    L, D, N = ada_w.shape
    B = c.shape[0]
    tn = _tile(N, 1024)
    return pl.pallas_call(
        _mod_kernel,
        grid=(L, N // tn),
        in_specs=[
            pl.BlockSpec((B, D), lambda l, j: (0, 0)),
            pl.BlockSpec((1, D, tn), lambda l, j: (l, 0, j)),
            pl.BlockSpec((1, 1, tn), lambda l, j: (l, 0, j)),
        ],
        out_specs=pl.BlockSpec((1, B, tn), lambda l, j: (l, 0, j)),
        out_shape=jax.ShapeDtypeStruct((L, B, N), F32),
        compiler_params=_params("parallel", "parallel"),
        name="adaln_mod",
    )(c, ada_w, ada_b.reshape(L, 1, N))


def _inproj_kernel(x_ref, sh_ref, sc_ref, w_ref, cs_ref, o_ref, *rest, n_bf16):
    h_ref = rest[-1]
    j = pl.program_id(2)

    @pl.when(j == 0)
    def _():
        h_ref[...] = (x_ref[0] * (1.0 + sc_ref[0]) + sh_ref[0]).astype(BF16)

    acc = jnp.dot(h_ref[...], w_ref[...], preferred_element_type=F32) * cs_ref[...]
    if n_bf16 is None:
        o_ref[0] = acc.astype(BF16)
    else:
        of_ref = rest[0]

        @pl.when(j < n_bf16)
        def _():
            o_ref[0] = acc.astype(BF16)

        @pl.when(j >= n_bf16)
        def _():
            of_ref[0] = acc


def _in_projection(x, mod, w_bf16, colscale, n_f32_cols=0):
    B, S, D = x.shape
    N = w_bf16.shape[1]
    tm = min(S, 1024)
    tn = _tile(N, 1024)
    in_specs = [
        pl.BlockSpec((1, tm, D), lambda b, i, j: (b, i, 0)),
        pl.BlockSpec((1, 1, D), lambda b, i, j: (b, 0, 0)),
        pl.BlockSpec((1, 1, D), lambda b, i, j: (b, 0, 1)),
        pl.BlockSpec((D, tn), lambda b, i, j: (0, j)),
        pl.BlockSpec((1, tn), lambda b, i, j: (0, j)),
    ]
    if n_f32_cols:
        n_bf16 = (N - n_f32_cols) // tn
        out_specs = [pl.BlockSpec((1, tm, tn), lambda b, i, j: (b, i, jnp.minimum(j, n_bf16 - 1))),
                     pl.BlockSpec((1, tm, tn), lambda b, i, j: (b, i, jnp.maximum(j - n_bf16, 0)))]
        out_shape = [jax.ShapeDtypeStruct((B, S, N - n_f32_cols), BF16),
                     jax.ShapeDtypeStruct((B, S, n_f32_cols), F32)]
    else:
        n_bf16 = None
        out_specs = pl.BlockSpec((1, tm, tn), lambda b, i, j: (b, i, j))
        out_shape = jax.ShapeDtypeStruct((B, S, N), BF16)
    return pl.pallas_call(
        functools.partial(_inproj_kernel, n_bf16=n_bf16),
        grid=(B, S // tm, N // tn),
        in_specs=in_specs,
        out_specs=out_specs,
        out_shape=out_shape,
        scratch_shapes=[pltpu.VMEM((tm, D), BF16)],
        compiler_params=_params("parallel", "parallel", "arbitrary"),
        name="in_proj",
    )(x, mod, mod, w_bf16, colscale)


N_POS_PIECES = 3


def _diff_kernel(slope_ref, q_ref, k_ref, v_ref, cx_ref, lam_ref, g_ref, o_ref,
                 kx1, kx2, qx1, qx2, bdiag, m1_ref, l1_ref, a1_ref, m2_ref, l2_ref, a2_ref, *, t, n_chunks, lam_init):
    h = pl.program_id(1)
    i = pl.program_id(2)
    slope2 = slope_ref[h]

    @pl.when(i == 0)
    def _():
        cx = cx_ref[0]
        kx1[:, :HEAD_DIM] = k_ref[0, :, :HEAD_DIM]
        kx2[:, :HEAD_DIM] = k_ref[0, :, HEAD_DIM:]
        for c in range(n_chunks):
            kx1[c * t:(c + 1) * t, HEAD_DIM:] = cx
            kx2[c * t:(c + 1) * t, HEAD_DIM:] = cx
        d = lax.broadcasted_iota(I32, (t, t), 1) - lax.broadcasted_iota(I32, (t, t), 0)
        bdiag[...] = -slope2 * jnp.abs(d).astype(F32)

    q = q_ref[0]
    ones = jnp.where(lax.broadcasted_iota(I32, (t, HEAD_DIM), 1) < N_POS_PIECES, 1.0, 0.0)
    for side, sign in enumerate((1.0, 0.0, -1.0)):
        e = (sign * ones).astype(BF16)
        qx1[side, :, :HEAD_DIM] = q[:, :HEAD_DIM]
        qx1[side, :, HEAD_DIM:] = e
        qx2[side, :, :HEAD_DIM] = q[:, HEAD_DIM:]
        qx2[side, :, HEAD_DIM:] = e
    row_term = slope2 * (i * t + lax.broadcasted_iota(I32, (t, 1), 0)).astype(F32)

    m1_ref[...] = jnp.full(m1_ref.shape, NEG, F32)
    m2_ref[...] = jnp.full(m2_ref.shape, NEG, F32)
    l1_ref[...] = jnp.zeros(l1_ref.shape, F32)
    l2_ref[...] = jnp.zeros(l2_ref.shape, F32)
    a1_ref[...] = jnp.zeros(a1_ref.shape, F32)
    a2_ref[...] = jnp.zeros(a2_ref.shape, F32)

    def key_tile(c, side, row_sign, on_diagonal):
        k0 = pl.multiple_of(c * t, t)
        vc = v_ref[0, pl.ds(k0, t), :]
        shift = row_sign * (row_term - slope2 * k0.astype(F32))

        def one(qx, kx, m_ref, l_ref, a_ref):
            s = _nt_dot(qx[side], kx[pl.ds(k0, t), :])
            if on_diagonal:
                s = s + bdiag[...]
            m_old = m_ref[...]
            m_new = jnp.maximum(m_old, jnp.max(s, axis=-1, keepdims=True) + shift)
            p = jnp.exp2(s - (m_new - shift))
            alpha = jnp.exp2(m_old - m_new)
            l_ref[...] = alpha * l_ref[...] + jnp.sum(p, axis=-1, keepdims=True)
            a_ref[...] = alpha * a_ref[...] + jnp.dot(p.astype(BF16), vc, preferred_element_type=F32)
            m_ref[...] = m_new

        one(qx1, kx1, m1_ref, l1_ref, a1_ref)
        one(qx2, kx2, m2_ref, l2_ref, a2_ref)

    def before(c, carry):
        key_tile(c, 0, -1.0, False)
        return carry

    def after(c, carry):
        key_tile(c, 2, 1.0, False)
        return carry

    lax.fori_loop(0, i, before, 0)
    key_tile(i, 1, 0.0, True)
    lax.fori_loop(i + 1, n_chunks, after, 0)

    lv = lam_ref[...]
    s01 = jnp.sum(lv[0:1, :] * lv[1:2, :], axis=-1, keepdims=True)
    s23 = jnp.sum(lv[2:3, :] * lv[3:4, :], axis=-1, keepdims=True)
    lam = jnp.exp(s01) - jnp.exp(s23) + lam_init
    o = a1_ref[...] / l1_ref[...] - lam * (a2_ref[...] / l2_ref[...])
    ms = jnp.mean(o * o, axis=-1, keepdims=True)
    o = o * lax.rsqrt(ms + LN_EPS) * g_ref[...] * (1.0 - lam_init)
    o_ref[0] = o.astype(BF16)


def _diff_attention(proj, lam_vecs, subln_g, layer_idx):
    B, S, _ = proj.shape
    t = min(S, 1024)
    lam_init = 0.8 - 0.6 * math.exp(-0.3 * layer_idx)
    slopes2 = jnp.asarray(_alibi_slopes(A_HEADS) * LOG2E)
    rest = slopes2[:, None] * jnp.arange(t, dtype=F32)[None, :]
    pieces = []
    for _ in range(N_POS_PIECES):
        piece = rest.astype(BF16)
        pieces.append(piece)
        rest = rest - piece.astype(F32)
    cext = jnp.zeros((A_HEADS, t, HEAD_DIM), BF16).at[:, :, :N_POS_PIECES].set(jnp.stack(pieces, axis=-1))
    nq = A_QK // A_VDIM
    kern = functools.partial(_diff_kernel, t=t, n_chunks=S // t, lam_init=lam_init)
    return pl.pallas_call(
        kern,
        grid=(B, A_HEADS, S // t),
        in_specs=[
            pl.BlockSpec(memory_space=pltpu.SMEM),
            pl.BlockSpec((1, t, A_VDIM), lambda b, h, i: (b, i, h)),
            pl.BlockSpec((1, S, A_VDIM), lambda b, h, i: (b, 0, nq + h)),
            pl.BlockSpec((1, S, A_VDIM), lambda b, h, i: (b, 0, 2 * nq + h)),
            pl.BlockSpec((1, t, HEAD_DIM), lambda b, h, i: (h, 0, 0)),
            pl.BlockSpec((4, HEAD_DIM), lambda b, h, i: (0, 0)),
            pl.BlockSpec((1, A_VDIM), lambda b, h, i: (0, 0)),
        ],
        out_specs=pl.BlockSpec((1, t, A_VDIM), lambda b, h, i: (b, i, h)),
        out_shape=jax.ShapeDtypeStruct((B, S, A_V), BF16),
        scratch_shapes=[
            pltpu.VMEM((S, 2 * HEAD_DIM), BF16), pltpu.VMEM((S, 2 * HEAD_DIM), BF16),
            pltpu.VMEM((3, t, 2 * HEAD_DIM), BF16), pltpu.VMEM((3, t, 2 * HEAD_DIM), BF16),
            pltpu.VMEM((t, t), F32),
            pltpu.VMEM((t, 1), F32), pltpu.VMEM((t, 1), F32), pltpu.VMEM((t, A_VDIM), F32),
            pltpu.VMEM((t, 1), F32), pltpu.VMEM((t, 1), F32), pltpu.VMEM((t, A_VDIM), F32),
        ],
        compiler_params=_params("parallel", "parallel", "arbitrary"),
        name="diff_attn",
    )(slopes2, proj, proj, proj, cext, lam_vecs, subln_g.reshape(1, A_VDIM))


def _banded_kernel(slope_ref, sink_ref, q_ref, kp_ref, kc_ref, kn_ref, vp_ref, vc_ref, vn_ref, *rest,
                   tq, radius, n_q, group, n_rows, dist_scale, use_sink, want_lse):
    if want_lse:
        o_ref, lse_ref, kwin, vwin = rest
    else:
        o_ref, kwin, vwin = rest
        lse_ref = None
    t = pl.program_id(2)
    w = tq + 2 * radius
    kwin[0:radius, :] = kp_ref[0]
    kwin[radius:radius + tq, :] = kc_ref[0]
    kwin[radius + tq:w, :] = kn_ref[0]
    vwin[0:radius, :] = vp_ref[0]
    vwin[radius:radius + tq, :] = vc_ref[0]
    vwin[radius + tq:w, :] = vn_ref[0]

    ii = lax.broadcasted_iota(I32, (tq, w), 0)
    jj = lax.broadcasted_iota(I32, (tq, w), 1)
    rel = jnp.abs(jj - radius - ii)
    kpos = t * tq - radius + jj
    valid = (rel <= radius) & (kpos >= 0) & (kpos < n_rows)
    dist = rel.astype(F32) * float(dist_scale)
    lane = lax.broadcasted_iota(I32, (tq, HEAD_DIM), 1)
    lse_tile = jnp.zeros((tq, HEAD_DIM), F32)

    for h in range(n_q):
        hk = h // group
        qh = q_ref[0, :, h * HEAD_DIM:(h + 1) * HEAD_DIM]
        kh = kwin[:, hk * HEAD_DIM:(hk + 1) * HEAD_DIM]
        vh = vwin[:, hk * HEAD_DIM:(hk + 1) * HEAD_DIM]
        s = jnp.where(valid, _nt_dot(qh, kh) - slope_ref[h] * dist, NEG)
        m = jnp.max(s, axis=-1, keepdims=True)
        if use_sink:
            m = jnp.maximum(m, sink_ref[h])
        p = jnp.exp2(s - m)
        den = jnp.sum(p, axis=-1, keepdims=True)
        if use_sink:
            den = den + jnp.exp2(sink_ref[h] - m)
        o = jnp.dot(p.astype(BF16), vh, preferred_element_type=F32) / den
        o_ref[0, :, h * HEAD_DIM:(h + 1) * HEAD_DIM] = o.astype(BF16)
        if want_lse:
            lse_tile = jnp.where(lane == h, m + jnp.log2(den), lse_tile)
    if want_lse:
        lse_ref[0] = lse_tile


def _banded_attention(src, *, n_rows, n_res, src_cols, q_blk, k_blk, v_blk, n_q, group, radius,
                      dist_scale, slopes2, sink2, want_lse):
    B = src.shape[0]
    qw = n_q * HEAD_DIM
    kw = (n_q // group) * HEAD_DIM
    tq = min(n_rows, 256)
    nt = n_rows // tq
    per_t = tq // radius
    last_halo = n_rows // radius - 1
    qpg = src_cols // qw
    kpg = src_cols // kw

    def q_map(b, r, t):
        return (b, t, r * qpg + q_blk)

    def cur_map(blk):
        return lambda b, r, t: (b, t, r * kpg + blk)

    def prev_map(blk):
        return lambda b, r, t: (b, jnp.maximum(t * per_t - 1, 0), r * kpg + blk)

    def next_map(blk):
        return lambda b, r, t: (b, jnp.minimum((t + 1) * per_t, last_halo), r * kpg + blk)

    out_shape = [jax.ShapeDtypeStruct((B, n_rows, n_res * qw), BF16)]
    out_specs = [pl.BlockSpec((1, tq, qw), lambda b, r, t: (b, t, r))]
    if want_lse:
        out_shape.append(jax.ShapeDtypeStruct((B, n_rows, n_res * HEAD_DIM), F32))
        out_specs.append(pl.BlockSpec((1, tq, HEAD_DIM), lambda b, r, t: (b, t, r)))
    kern = functools.partial(_banded_kernel, tq=tq, radius=radius, n_q=n_q, group=group, n_rows=n_rows,
                             dist_scale=dist_scale, use_sink=sink2 is not None, want_lse=want_lse)
    if sink2 is None:
        sink2 = jnp.zeros((n_q,), F32)
    return pl.pallas_call(
        kern,
        grid=(B, n_res, nt),
        in_specs=[
            pl.BlockSpec(memory_space=pltpu.SMEM),
            pl.BlockSpec(memory_space=pltpu.SMEM),
            pl.BlockSpec((1, tq, qw), q_map),
            pl.BlockSpec((1, radius, kw), prev_map(k_blk)),
            pl.BlockSpec((1, tq, kw), cur_map(k_blk)),
            pl.BlockSpec((1, radius, kw), next_map(k_blk)),
            pl.BlockSpec((1, radius, kw), prev_map(v_blk)),
            pl.BlockSpec((1, tq, kw), cur_map(v_blk)),
            pl.BlockSpec((1, radius, kw), next_map(v_blk)),
        ],
        out_specs=out_specs,
        out_shape=out_shape,
        scratch_shapes=[pltpu.VMEM((tq + 2 * radius, kw), BF16), pltpu.VMEM((tq + 2 * radius, kw), BF16)],
        compiler_params=_params("parallel", "parallel", "parallel"),
        name="banded_attn",
    )(slopes2, sink2, src, src, src, src, src, src, src)


def _band_bias(tq, wlen, offset, radius, slope_dist):
    rel = jnp.abs(offset + lax.broadcasted_iota(I32, (tq, wlen), 1) - lax.broadcasted_iota(I32, (tq, wlen), 0))
    return jnp.where(rel <= radius, -slope_dist * rel.astype(F32), NEG)


def _band_tile(qf, kf, vf, bias):
    s = _nt_dot(qf, kf) + bias
    m = jnp.max(s, axis=-1, keepdims=True)
    p = jnp.exp2(s - m)
    return m, jnp.sum(p, axis=-1, keepdims=True), jnp.dot(p.astype(BF16), vf, preferred_element_type=F32)


TILE_UNROLL = 4


def _loop_tiles(n, tile_fn):
    unroll = TILE_UNROLL if n % TILE_UNROLL == 0 else 1

    def body(i, carry):
        for u in range(unroll):
            tile_fn(i * unroll + u)
        return carry

    lax.fori_loop(0, n // unroll, body, 0)


def _band_geometry(seq, tq, radius, dil):
    n_rows = seq // dil
    tile = min(tq, n_rows)
    nt = n_rows // tile
    wlen = min(tile + 2 * radius, n_rows)
    offsets = (0,) if nt == 1 else (0, -radius, tile - wlen)
    return n_rows, tile, nt, wlen, offsets


def _dilated_kernel(slope_ref, q_ref, k_ref, v_ref, o_ref, acc_s, m_s, l_s, bias_s, *, seq, tq, branches):
    slope = slope_ref[pl.program_id(1)]
    lanes = (tq, HEAD_DIM)

    first_bias = []
    n_bias = 0
    for radius, dil in branches:
        _, tile, _, wlen, offsets = _band_geometry(seq, tq, radius, dil)
        first_bias.append(n_bias)
        for off in offsets:
            bias_s[n_bias, 0:tile, 0:wlen] = _band_bias(tile, wlen, off, radius, slope * dil)
            n_bias += 1

    def tile_window(bi, t):
        radius, dil = branches[bi]
        n_rows, tile, nt, wlen, _ = _band_geometry(seq, tq, radius, dil)
        q0 = t * tile
        k0 = jnp.clip(q0 - radius, 0, n_rows - wlen)
        case = 0 if nt == 1 else jnp.where(t == 0, 0, jnp.where(t == nt - 1, 2, 1))
        return q0, k0, bias_s[first_bias[bi] + case, 0:tile, 0:wlen]

    for bi, (radius, dil) in enumerate(branches[:-1]):
        _, tile, nt, wlen, _ = _band_geometry(seq, tq, radius, dil)

        def one_tile(it, bi=bi, dil=dil, tile=tile, nt=nt, wlen=wlen):
            r = it // nt
            q0, k0, bias = tile_window(bi, it % nt)
            qrows = pl.ds(r + dil * q0, tile, stride=dil)
            krows = pl.ds(r + dil * k0, wlen, stride=dil)
            m, l, acc = _band_tile(q_ref[0, qrows, :].astype(BF16), k_ref[0, krows, :].astype(BF16),
                                   v_ref[0, krows, :].astype(BF16), bias)
            acc_s[bi, qrows, :] = acc
            m_s[bi, qrows, :] = jnp.broadcast_to(m, (tile, HEAD_DIM))
            l_s[bi, qrows, :] = jnp.broadcast_to(l, (tile, HEAD_DIM))

        _loop_tiles(dil * nt, one_tile)

    assert branches[-1][1] == 1
    wlen = _band_geometry(seq, tq, *branches[-1])[3]

    def dense(t):
        q0, k0, bias = tile_window(len(branches) - 1, t)
        qrows = pl.ds(pl.multiple_of(q0, tq), tq)
        krows = pl.ds(pl.multiple_of(k0, 8), wlen)
        m, l, acc = _band_tile(q_ref[0, qrows, :].astype(BF16), k_ref[0, krows, :].astype(BF16),
                               v_ref[0, krows, :].astype(BF16), bias)
        ms = [jnp.broadcast_to(m, lanes)] + [m_s[bi, qrows, :] for bi in range(len(branches) - 1)]
        ls = [jnp.broadcast_to(l, lanes)] + [l_s[bi, qrows, :] for bi in range(len(branches) - 1)]
        accs = [acc] + [acc_s[bi, qrows, :] for bi in range(len(branches) - 1)]
        top = functools.reduce(jnp.maximum, ms)
        es = [jnp.exp2(mi - top) for mi in ms]
        num = sum(e * a for e, a in zip(es, accs))
        den = sum(e * li for e, li in zip(es, ls))
        o_ref[0, qrows, :] = (num / den).astype(BF16)

    _loop_tiles(seq // tq, dense)


def _dilated_mixture(projf, slopes2):
    B, S, _ = projf.shape
    branches = tuple((w // (2 * d), d) for w, d in sorted(B_BRANCHES, key=lambda wd: -wd[1]))
    tq = min(S, 256)
    kern = functools.partial(_dilated_kernel, seq=S, tq=tq, branches=branches)
    nb = len(branches) - 1
    geo = [_band_geometry(S, tq, radius, dil) for radius, dil in branches]
    bias_shape = (sum(len(g[4]) for g in geo), tq, max(g[3] for g in geo))
    col = lambda part: (lambda b, h: (b, 0, part * B_HEADS + h))
    return pl.pallas_call(
        kern,
        grid=(B, B_HEADS),
        in_specs=[pl.BlockSpec(memory_space=pltpu.SMEM)] + [pl.BlockSpec((1, S, HEAD_DIM), col(p)) for p in range(3)],
        out_specs=pl.BlockSpec((1, S, HEAD_DIM), col(0)),
        out_shape=jax.ShapeDtypeStruct((B, S, B_W), BF16),
        scratch_shapes=[pltpu.VMEM((nb, S, HEAD_DIM), F32)] * 3 + [pltpu.VMEM(bias_shape, F32)],
        compiler_params=_params("parallel", "parallel"),
        name="dilated_mix",
    )(slopes2, projf, projf, projf)


def _post_attention(y, x, g1, lng, lnb, sc2, sh2, rwh_ref, rwl_ref, x1_ref, hp_ref, lg_ref, alpha):
    z = alpha * x + g1 * y
    mu = jnp.mean(z, axis=-1, keepdims=True)
    zc = z - mu
    var = jnp.mean(zc * zc, axis=-1, keepdims=True)
    x1 = zc * lax.rsqrt(var + LN_EPS) * lng + lnb
    x1_ref[0] = x1
    h2 = x1 * (1.0 + sc2) + sh2
    hb = h2.astype(BF16)
    hf = hb.astype(F32)
    bits = lax.bitcast_convert_type(hf, U32)
    half = bits.shape[1] // 2
    packed = bits[:, :half] | (bits[:, half:] >> 16)
    nck = half // HEAD_DIM
    for ck in range(nck):
        hp_ref[0, pl.ds(ck, bits.shape[0], stride=nck), :] = packed[:, ck * HEAD_DIM:(ck + 1) * HEAD_DIM]
    lo = (h2 - hf).astype(BF16)
    rwh = rwh_ref[...]
    lg_ref[0] = (jnp.dot(hb, rwh, preferred_element_type=F32)
                 + jnp.dot(hb, rwl_ref[...], preferred_element_type=F32)
                 + jnp.dot(lo, rwh, preferred_element_type=F32))


def _outproj_ab_kernel(ya_ref, yb_ref, w_ref, x_ref, g1_ref, lng_ref, lnb_ref, sh2_ref, sc2_ref, rwh_ref, rwl_ref,
                       x1_ref, hp_ref, lg_ref, *, alpha):
    y = (jnp.dot(ya_ref[0], w_ref[0:A_V, :], preferred_element_type=F32)
         + jnp.dot(yb_ref[0], w_ref[A_V:A_V + B_W, :], preferred_element_type=F32))
    _post_attention(y, x_ref[0], g1_ref[0], lng_ref[...], lnb_ref[...], sc2_ref[0], sh2_ref[0],
                    rwh_ref, rwl_ref, x1_ref, hp_ref, lg_ref, alpha)


def _outproj_c_kernel(o_ref, w_ref, x_ref, g1_ref, lng_ref, lnb_ref, sh2_ref, sc2_ref, rwh_ref, rwl_ref,
                      x1_ref, hp_ref, lg_ref, *, alpha):
    y = jnp.dot(o_ref[0], w_ref[...], preferred_element_type=F32)
    _post_attention(y, x_ref[0], g1_ref[0], lng_ref[...], lnb_ref[...], sc2_ref[0], sh2_ref[0],
                    rwh_ref, rwl_ref, x1_ref, hp_ref, lg_ref, alpha)


def _out_projection(attn_inputs, w_bf16, x, mod, lng, lnb, rw_hi, rw_lo, alpha, mixer):
    B, S, D = x.shape
    E = rw_hi.shape[1]
    tm = min(S, 512)
    row = lambda b, i: (b, i, 0)
    const2 = lambda b, i: (0, 0)
    tail_specs = [
        pl.BlockSpec(w_bf16.shape, const2),
        pl.BlockSpec((1, tm, D), row),
        pl.BlockSpec((1, 1, D), lambda b, i: (b, 0, 2)),
        pl.BlockSpec((1, D), const2),
        pl.BlockSpec((1, D), const2),
        pl.BlockSpec((1, 1, D), lambda b, i: (b, 0, 3)),
        pl.BlockSpec((1, 1, D), lambda b, i: (b, 0, 4)),
        pl.BlockSpec((D, E), const2),
        pl.BlockSpec((D, E), const2),
    ]
    tail_args = (w_bf16, x, mod, lng.reshape(1, D), lnb.reshape(1, D), mod, mod, rw_hi, rw_lo)
    nck = D // 2 // HEAD_DIM
    out_shape = [jax.ShapeDtypeStruct((B, S, D), F32), jax.ShapeDtypeStruct((B, S * nck, HEAD_DIM), U32),
                 jax.ShapeDtypeStruct((B, S, E), F32)]
    out_specs = [pl.BlockSpec((1, tm, D), row), pl.BlockSpec((1, tm * nck, HEAD_DIM), row),
                 pl.BlockSpec((1, tm, E), row)]
    if mixer == "ab":
        ya, yb = attn_inputs
        in_specs = [pl.BlockSpec((1, tm, A_V), row), pl.BlockSpec((1, tm, B_W), row)] + tail_specs
        kern = functools.partial(_outproj_ab_kernel, alpha=alpha)
        scratch = []
        args = (ya, yb) + tail_args
    else:
        (o,) = attn_inputs
        in_specs = [pl.BlockSpec((1, tm, o.shape[2]), row)] + tail_specs
        kern = functools.partial(_outproj_c_kernel, alpha=alpha)
        scratch = []
        args = (o,) + tail_args
    return pl.pallas_call(
        kern,
        grid=(B, S // tm),
        in_specs=in_specs,
        out_specs=out_specs,
        out_shape=out_shape,
        scratch_shapes=scratch,
        compiler_params=_params("parallel", "parallel"),
        name="out_proj_" + mixer,
    )(*args)


def _excl_cumsum_lanes(x, tri):
    n = x.shape[1] // HEAD_DIM
    carry = jnp.zeros((x.shape[0], 1), F32)
    outs = []
    for c in range(n):
        xc = x[:, c * HEAD_DIM:(c + 1) * HEAD_DIM]
        outs.append(jnp.dot(xc.astype(BF16), tri, preferred_element_type=F32) + carry)
        carry = carry + jnp.sum(xc, axis=1, keepdims=True)
    return jnp.concatenate(outs, axis=1)


def _tables_kernel(lg_ref, cnt_ref, off_ref, tok_ref, gate_ref, place_ref, *, n_exp, cap):
    lg = lg_ref[0]
    S = lg.shape[1]
    ex = jnp.exp(lg - jnp.max(lg, axis=0, keepdims=True))
    aff = ex / jnp.sum(ex, axis=0, keepdims=True)
    bits = lax.bitcast_convert_type(aff, I32)

    def search(i, cur):
        cand = cur | jnp.left_shift(jnp.int32(1), 30 - i)
        cnt = jnp.sum(jnp.where(bits >= cand, 1.0, 0.0), axis=1, keepdims=True)
        return jnp.where(cnt >= cap, cand, cur)

    thr = lax.fori_loop(0, 31, search, jnp.zeros((n_exp, 1), I32))
    tri = jnp.where(lax.broadcasted_iota(I32, (HEAD_DIM, HEAD_DIM), 0)
                    < lax.broadcasted_iota(I32, (HEAD_DIM, HEAD_DIM), 1), 1.0, 0.0).astype(BF16)
    gt = bits > thr
    eq = jnp.where(bits == thr, 1.0, 0.0)
    need = cap - jnp.sum(jnp.where(gt, 1.0, 0.0), axis=1, keepdims=True)
    sel = jnp.where(gt, 1.0, jnp.where(_excl_cumsum_lanes(eq, tri) < need, eq, 0.0))
    pos = _excl_cumsum_lanes(sel, tri)

    ranks = []
    run = jnp.zeros((1, S), F32)
    for e in range(n_exp):
        ranks.append(run)
        run = run + sel[e:e + 1, :]
    cnt_ref[0] = run.astype(I32)
    first = _excl_cumsum_lanes(run, tri) + (pl.program_id(0) * (n_exp * cap)).astype(F32)
    off_ref[0] = first.astype(I32)

    jcol = lax.broadcasted_iota(I32, (cap, S), 0).astype(F32)
    token = (lax.broadcasted_iota(I32, (1, S), 1) + pl.program_id(0) * S).astype(F32)
    toks, gates, places = [], [], []
    for e in range(n_exp):
        mine = jnp.where(sel[e:e + 1, :] > 0.0, pos[e:e + 1, :], -1.0) == jcol
        pick = lambda row: jnp.sum(jnp.where(mine, row, 0.0), axis=1, keepdims=True)
        toks.append(pick(token))
        gates.append(pick(aff[e:e + 1, :]))
        places.append(pick(first + ranks[e]))
    tok_ref[0] = jnp.concatenate(toks, axis=1).astype(I32)
    gate_ref[0] = jnp.concatenate(gates, axis=1)
    place_ref[0] = jnp.concatenate(places, axis=1).astype(I32)


def _routing_tables(logits_t, cap):
    B, E, S = logits_t.shape
    kern = functools.partial(_tables_kernel, n_exp=E, cap=cap)
    per_token = pl.BlockSpec((1, 1, S), lambda b: (b, 0, 0))
    per_slot = pl.BlockSpec((1, cap, E), lambda b: (b, 0, 0))
    cnt, off, tok, gate, place = pl.pallas_call(
        kern,
        grid=(B,),
        in_specs=[pl.BlockSpec((1, E, S), lambda b: (b, 0, 0))],
        out_specs=[per_token, per_token, per_slot, per_slot, per_slot],
        out_shape=[jax.ShapeDtypeStruct((B, 1, S), I32), jax.ShapeDtypeStruct((B, 1, S), I32),
                   jax.ShapeDtypeStruct((B, cap, E), I32), jax.ShapeDtypeStruct((B, cap, E), F32),
                   jax.ShapeDtypeStruct((B, cap, E), I32)],
        compiler_params=_params("parallel"),
        name="routing_tables",
    )(logits_t)
    expert_major = lambda a: jnp.transpose(a, (2, 0, 1)).reshape(E * B * cap)
    return cnt, off, expert_major(tok), expert_major(gate), expert_major(place)


def _token_rows(t, nck):
    start = t * nck
    return pl.ds(pl.multiple_of(start, nck) if nck > 1 else start, nck)


def _unpack_rows(xp_ref, first, xs_ref):
    rows, d = xs_ref.shape
    half = d // 2
    nck = half // HEAD_DIM
    for ck in range(nck):
        u = xp_ref[pl.ds(first * nck + ck, rows, stride=nck), :]
        lo = ck * HEAD_DIM
        xs_ref[:, lo:lo + HEAD_DIM] = lax.bitcast_convert_type(u & jnp.uint32(0xFFFF0000), F32).astype(BF16)
        xs_ref[:, half + lo:half + lo + HEAD_DIM] = lax.bitcast_convert_type(u << 16, F32).astype(BF16)


def _pack_pairs(hi, lo):
    hb = lax.bitcast_convert_type(hi.astype(BF16).astype(F32), U32)
    lb = lax.bitcast_convert_type(lo.astype(BF16).astype(F32), U32)
    return hb | (lb >> 16)


ROW_UNROLL = 8


def _ffn_up_kernel(tok_ref, tokn_ref, hp_ref, wg_ref, wu_ref, o_ref, xraw, xs_ref, sem, *, tm, mt, n_tiles, nck):
    tile = pl.program_id(0) * mt + pl.program_id(1)
    par = tile % 2

    def gather(idx_ref, slot):
        def body(i, carry):
            for u in range(ROW_UNROLL):
                r = i * ROW_UNROLL + u
                pltpu.make_async_copy(hp_ref.at[_token_rows(idx_ref[r], nck)],
                                      xraw.at[_token_rows(slot * tm + r, nck)], sem.at[slot]).start()
            return carry

        lax.fori_loop(0, tm // ROW_UNROLL, body, 0)

    @pl.when(pl.program_id(2) == 0)
    def _():
        @pl.when(tile == 0)
        def _():
            gather(tok_ref, 0)

        pltpu.make_async_copy(hp_ref.at[pl.ds(0, tm * nck)], xraw.at[pl.ds(par * tm * nck, tm * nck)],
                              sem.at[par]).wait()

        @pl.when(tile + 1 < n_tiles)
        def _():
            gather(tokn_ref, 1 - par)

        _unpack_rows(xraw, par * tm, xs_ref)

    x = xs_ref[...]
    g = jnp.dot(x, wg_ref[0, 0].astype(BF16), preferred_element_type=F32)
    u = jnp.dot(x, wu_ref[0, 0].astype(BF16), preferred_element_type=F32)
    o_ref[...] = (g * jax.nn.sigmoid(g) * u).astype(BF16)


def _ffn_up(hp, tok_rows, w_gate, w_up, layer, rows_per_expert):
    _, E, D, Fh = w_gate.shape
    tm = min(rows_per_expert, 2048)
    mt = rows_per_expert // tm
    n_tiles = E * mt
    tn = min(Fh, 256)
    nck = D // 2 // HEAD_DIM
    kern = functools.partial(_ffn_up_kernel, tm=tm, mt=mt, n_tiles=n_tiles, nck=nck)
    smem = functools.partial(pl.BlockSpec, memory_space=pltpu.SMEM)
    return pl.pallas_call(
        kern,
        grid=(E, mt, Fh // tn),
        in_specs=[smem((tm,), lambda e, m, n: (e * mt + m,)),
                  smem((tm,), lambda e, m, n: (jnp.minimum(e * mt + m + 1, n_tiles - 1),)),
                  pl.BlockSpec(memory_space=pl.ANY),
                  pl.BlockSpec((1, 1, D, tn), lambda e, m, n: (layer, e, 0, n)),
                  pl.BlockSpec((1, 1, D, tn), lambda e, m, n: (layer, e, 0, n))],
        out_specs=pl.BlockSpec((tm, tn), lambda e, m, n: (e * mt + m, n)),
        out_shape=jax.ShapeDtypeStruct((E * rows_per_expert, Fh), BF16),
        scratch_shapes=[pltpu.VMEM((2 * tm * nck, HEAD_DIM), U32), pltpu.VMEM((tm, D), BF16),
                        pltpu.SemaphoreType.DMA((2,))],
        compiler_params=_params("arbitrary", "arbitrary", "arbitrary"),
        name="ffn_up",
    )(tok_rows, tok_rows, hp, w_gate, w_up)


def _ffn_down_kernel(pos_ref, h_ref, wa_ref, wb_ref, g_ref, yt_ref, ybuf, sem, *, n_steps, cps, tm, mt, n_tiles,
                     nck):
    tile = pl.program_id(0) * mt + pl.program_id(1)
    par = tile % 2
    h = h_ref[...]
    g = g_ref[...]
    ya = jnp.dot(h, wa_ref[0, 0].astype(BF16), preferred_element_type=F32) * g
    yb = jnp.dot(h, wb_ref[0, 0].astype(BF16), preferred_element_type=F32) * g
    packed = _pack_pairs(ya, yb)
    n = pl.program_id(2)
    for k in range(n_steps):
        @pl.when(n == k)
        def _():
            for cc in range(cps):
                ybuf[pl.ds(par * tm * nck + k * cps + cc, tm, stride=nck), :] = (
                    packed[:, cc * HEAD_DIM:(cc + 1) * HEAD_DIM])

    def scatter_done(slot):
        return pltpu.make_async_copy(ybuf.at[pl.ds(slot * tm * nck, tm * nck)], yt_ref.at[pl.ds(0, tm * nck)],
                                     sem.at[slot])

    @pl.when(n == n_steps - 1)
    def _():
        @pl.when(tile > 0)
        def _():
            scatter_done(1 - par).wait()

        def body(i, carry):
            for u in range(ROW_UNROLL):
                r = i * ROW_UNROLL + u
                pltpu.make_async_copy(ybuf.at[_token_rows(par * tm + r, nck)],
                                      yt_ref.at[_token_rows(pos_ref[r], nck)], sem.at[par]).start()
            return carry

        lax.fori_loop(0, tm // ROW_UNROLL, body, 0)

        @pl.when(tile == n_tiles - 1)
        def _():
            scatter_done(par).wait()


def _ffn_down(hid, w_down, layer, gate_col, pos_rows, rows_per_expert):
    _, E, Fh, D = w_down.shape
    tm = min(rows_per_expert, 2048)
    mt = rows_per_expert // tm
    n_tiles = E * mt
    half = D // 2
    tnh = min(half, 256)
    n_steps = half // tnh
    nck = half // HEAD_DIM
    kern = functools.partial(_ffn_down_kernel, n_steps=n_steps, cps=tnh // HEAD_DIM, tm=tm, mt=mt, n_tiles=n_tiles,
                             nck=nck)
    return pl.pallas_call(
        kern,
        grid=(E, mt, n_steps),
        in_specs=[pl.BlockSpec((tm,), lambda e, m, n: (e * mt + m,), memory_space=pltpu.SMEM),
                  pl.BlockSpec((tm, Fh), lambda e, m, n: (e * mt + m, 0)),
                  pl.BlockSpec((1, 1, Fh, tnh), lambda e, m, n: (layer, e, 0, n)),
                  pl.BlockSpec((1, 1, Fh, tnh), lambda e, m, n: (layer, e, 0, n + n_steps)),
                  pl.BlockSpec((tm, 1), lambda e, m, n: (e * mt + m, 0))],
        out_specs=pl.BlockSpec(memory_space=pl.ANY),
        out_shape=jax.ShapeDtypeStruct((E * rows_per_expert * nck, HEAD_DIM), U32),
        scratch_shapes=[pltpu.VMEM((2 * tm * nck, HEAD_DIM), U32), pltpu.SemaphoreType.DMA((2,))],
        compiler_params=_params("arbitrary", "arbitrary", "arbitrary"),
        name="ffn_down",
    )(pos_rows, hid, w_down, w_down, gate_col)


def _combine_kernel(offt_ref, x_ref, off_ref, cnt_ref, g2_ref, lng_ref, lnb_ref, yt_ref, o_ref,
                    buf, rows, acc, sem, *, tm, ch, n_total, n_tiles, n_tiles_s, alpha, nck):
    tile = pl.program_id(0) * n_tiles_s + pl.program_id(1)
    o0 = offt_ref[tile]
    o1 = offt_ref[tile + 1]
    n_chunks = jnp.maximum((o1 - o0 + ch - 1) // ch, 1)
    lo_col = off_ref[0]
    hi_col = lo_col + cnt_ref[0]
    lane = lax.broadcasted_iota(I32, (tm, ch), 1)
    acc[...] = jnp.zeros(acc.shape, F32)

    def fetch(want, slot):
        start = jnp.minimum(want, n_total - ch) * nck
        if nck > 1:
            start = pl.multiple_of(start, nck)
        return pltpu.make_async_copy(yt_ref.at[pl.ds(start, ch * nck)], buf.at[pl.ds(slot * ch * nck, ch * nck)],
                                     sem.at[slot])

    @pl.when(tile == 0)
    def _():
        fetch(o0, 0).start()

    def chunk(c, carry):
        slot = c % 2
        want = o0 + c * ch
        fetch(want, slot).wait()

        @pl.when(c + 1 < n_chunks)
        def _():
            fetch(want + ch, 1 - slot).start()

        _unpack_rows(buf, slot * ch, rows)
        row = lane + jnp.minimum(want, n_total - ch)
        own = (row >= lo_col) & (row < hi_col) & (row >= want)
        acc[...] += jnp.dot(jnp.where(own, 1.0, 0.0).astype(BF16), rows[...], preferred_element_type=F32)
        return carry

    lax.fori_loop(0, n_chunks, chunk, 0)

    @pl.when(tile + 1 < n_tiles)
    def _():
        fetch(o1, 0).start()

    z = alpha * x_ref[0] + g2_ref[0] * acc[...]
    mu = jnp.mean(z, axis=-1, keepdims=True)
    zc = z - mu
    var = jnp.mean(zc * zc, axis=-1, keepdims=True)
    o_ref[0] = zc * lax.rsqrt(var + LN_EPS) * lng_ref[...] + lnb_ref[...]


def _combine(x1, mod, lng, lnb, yt, off, cnt, alpha):
    B, S, D = x1.shape
    nck = D // 2 // HEAD_DIM
    n_total = yt.shape[0] // nck
    tm = min(S, 256)
    ch = min(n_total, 256)
    n_tiles_s = S // tm
    off_flat = off.reshape(B * S)
    offt = jnp.concatenate([off_flat[::tm], jnp.full((1,), n_total, I32)])
    kern = functools.partial(_combine_kernel, tm=tm, ch=ch, n_total=n_total, n_tiles=B * n_tiles_s,
                             n_tiles_s=n_tiles_s, alpha=alpha, nck=nck)
    grid_spec = pltpu.PrefetchScalarGridSpec(
        num_scalar_prefetch=1,
        grid=(B, n_tiles_s),
        in_specs=[
            pl.BlockSpec((1, tm, D), lambda b, i, o: (b, i, 0)),
            pl.BlockSpec((1, tm, 1), lambda b, i, o: (b, i, 0)),
            pl.BlockSpec((1, tm, 1), lambda b, i, o: (b, i, 0)),
            pl.BlockSpec((1, 1, D), lambda b, i, o: (b, 0, 5)),
            pl.BlockSpec((1, D), lambda b, i, o: (0, 0)),
            pl.BlockSpec((1, D), lambda b, i, o: (0, 0)),
            pl.BlockSpec(memory_space=pl.ANY),
        ],
        out_specs=pl.BlockSpec((1, tm, D), lambda b, i, o: (b, i, 0)),
        scratch_shapes=[pltpu.VMEM((2 * ch * nck, HEAD_DIM), U32), pltpu.VMEM((ch, D), BF16),
                        pltpu.VMEM((tm, D), F32), pltpu.SemaphoreType.DMA((2,))],
    )
    return pl.pallas_call(
        kern,
        grid_spec=grid_spec,
        out_shape=jax.ShapeDtypeStruct((B, S, D), F32),
        compiler_params=_params("arbitrary", "arbitrary"),
        name="moe_combine",
    )(offt, x1, off.reshape(B, S, 1), cnt.reshape(B, S, 1), mod, lng.reshape(1, D), lnb.reshape(1, D), yt)


def _moe_sublayer(x1, hp, logits, mod, lng, lnb, w_gate, w_up, w_down, layer, alpha):
    B, S, D = x1.shape
    E = logits.shape[-1]
    cap = EC_CAPACITY_FACTOR * S // E
    cnt, off, tok, grow, pos = _routing_tables(jnp.swapaxes(logits, 1, 2), cap)
    hid = _ffn_up(hp.reshape(-1, HEAD_DIM), tok, w_gate, w_up, layer, B * cap)
    yt = _ffn_down(hid, w_down, layer, grow.reshape(E * B * cap, 1), pos, B * cap)
    return _combine(x1, mod, lng, lnb, yt, off, cnt, alpha)


def _split_bf16(w):
    hi = w.astype(BF16)
    return hi, (w - hi.astype(F32)).astype(BF16)


def kernel(x, c, ada_w, ada_b, ln_g, ln_b, ab_w_in, ab_w_out, diff_lambda, diff_subln_g, c_w_in, c_w_out,
           c_sink, router_w, w_gate, w_up, w_down):
    B, S, D = x.shape
    depth = ada_w.shape[0]
    alpha = (2.0 * depth) ** 0.25
    qscale = HEAD_DIM ** -0.5 * LOG2E
    mod_all = _modulation(c, ada_w, ada_b)

    ab_scale = np.ones((1, AB_IN), np.float32)
    ab_scale[:, :A_QK] = qscale
    ab_scale[:, 2 * A_QK + A_V:2 * A_QK + A_V + B_W] = qscale
    c_scale = np.ones((1, C_IN), np.float32)
    c_scale[:, :C_QW] = qscale

    for l in range(depth):
        mod = mod_all[l][:, None, :]
        i = l // 2
        rw_hi, rw_lo = _split_bf16(router_w[l])
        if l % 2 == 0:
            proj, projf = _in_projection(x, mod, ab_w_in[i].astype(BF16), jnp.asarray(ab_scale), n_f32_cols=3 * B_W)
            ya = _diff_attention(proj, diff_lambda[i], diff_subln_g[i], l)
            yb = _dilated_mixture(projf, jnp.asarray(_alibi_slopes(B_HEADS) * LOG2E))
            x1, hp, logits = _out_projection((ya, yb), ab_w_out[i].astype(BF16), x, mod,
                                             ln_g[l, 0], ln_b[l, 0], rw_hi, rw_lo, alpha, "ab")
        else:
            proj = _in_projection(x, mod, c_w_in[i].astype(BF16), jnp.asarray(c_scale))
            (o,) = _banded_attention(
                proj, n_rows=S, n_res=1, src_cols=C_IN, q_blk=0, k_blk=C_QW // C_KVW, v_blk=C_QW // C_KVW + 1,
                n_q=C_Q_HEADS, group=C_Q_HEADS // C_KV_HEADS, radius=C_RADIUS, dist_scale=1,
                slopes2=jnp.asarray(_alibi_slopes(C_Q_HEADS) * LOG2E), sink2=c_sink[i] * LOG2E, want_lse=False)
            x1, hp, logits = _out_projection((o,), c_w_out[i].astype(BF16), x, mod, ln_g[l, 0], ln_b[l, 0],
                                             rw_hi, rw_lo, alpha, "c")
        x = _moe_sublayer(x1, hp, logits, mod, ln_g[l, 1], ln_b[l, 1], w_gate, w_up, w_down, l, alpha)
    return x
```

```python
import functools
import math

import numpy as np
import jax
import jax.numpy as jnp
from jax import lax
from jax.experimental import pallas as pl
from jax.experimental.pallas import tpu as pltpu

F32 = jnp.float32
BF16 = jnp.bfloat16
I32 = jnp.int32
U32 = jnp.uint32

HEAD_DIM = 128
A_HEADS = 4
A_VDIM = 2 * HEAD_DIM
B_HEADS = 8
B_BRANCHES = ((128, 1), (512, 4), (2048, 16))
C_Q_HEADS = 16
C_KV_HEADS = 4
C_RADIUS = 128
EC_CAPACITY_FACTOR = 2
LN_EPS = 1e-5
NEG = -1e30
LOG2E = 1.4426950408889634

A_QK = A_HEADS * 2 * HEAD_DIM
A_V = A_HEADS * A_VDIM
B_W = B_HEADS * HEAD_DIM
AB_IN = 2 * A_QK + A_V + 3 * B_W
C_QW = C_Q_HEADS * HEAD_DIM
C_KVW = C_KV_HEADS * HEAD_DIM
C_IN = C_QW + 2 * C_KVW

VMEM_LIMIT_BYTES = 56 * 1024 * 1024


def _params(*sem):
    return pltpu.CompilerParams(dimension_semantics=sem, vmem_limit_bytes=VMEM_LIMIT_BYTES)


def _tile(n, preferred):
    t = min(n, preferred)
    while n % t:
        t //= 2
    return t


def _alibi_slopes(n):
    return np.array([2.0 ** (-8.0 * (i + 1) / n) for i in range(n)], dtype=np.float32)


def _nt_dot(a, b):
    return lax.dot_general(a, b, (((1,), (1,)), ((), ())), preferred_element_type=F32)


def _mod_kernel(c_ref, w_ref, b_ref, o_ref):
    c = c_ref[...]
    cs = (c * jax.nn.sigmoid(c)).astype(BF16)
    o_ref[0] = jnp.dot(cs, w_ref[0].astype(BF16), preferred_element_type=F32) + b_ref[0]


def _modulation(c, ada_w, ada_b):
    L, D, N = ada_w.shape
    B = c.shape[0]
    tn = _tile(N, 1024)
    return pl.pallas_call(
        _mod_kernel,
        grid=(L, N // tn),
        in_specs=[
            pl.BlockSpec((B, D), lambda l, j: (0, 0)),
            pl.BlockSpec((1, D, tn), lambda l, j: (l, 0, j)),
            pl.BlockSpec((1, 1, tn), lambda l, j: (l, 0, j)),
        ],
        out_specs=pl.BlockSpec((1, B, tn), lambda l, j: (l, 0, j)),
        out_shape=jax.ShapeDtypeStruct((L, B, N), F32),
        compiler_params=_params("parallel", "parallel"),
        name="adaln_mod",
    )(c, ada_w, ada_b.reshape(L, 1, N))


def _inproj_kernel(x_ref, sh_ref, sc_ref, w_ref, cs_ref, o_ref, *rest, n_bf16):
    h_ref = rest[-1]
    j = pl.program_id(2)

    @pl.when(j == 0)
    def _():
        h_ref[...] = (x_ref[0] * (1.0 + sc_ref[0]) + sh_ref[0]).astype(BF16)

    acc = jnp.dot(h_ref[...], w_ref[...], preferred_element_type=F32) * cs_ref[...]
    if n_bf16 is None:
        o_ref[0] = acc.astype(BF16)
    else:
        of_ref = rest[0]

        @pl.when(j < n_bf16)
        def _():
            o_ref[0] = acc.astype(BF16)

        @pl.when(j >= n_bf16)
        def _():
            of_ref[0] = acc


def _in_projection(x, mod, w_bf16, colscale, n_f32_cols=0):
    B, S, D = x.shape
    N = w_bf16.shape[1]
    tm = min(S, 1024)
    tn = _tile(N, 1024)
    in_specs = [
        pl.BlockSpec((1, tm, D), lambda b, i, j: (b, i, 0)),
        pl.BlockSpec((1, 1, D), lambda b, i, j: (b, 0, 0)),
        pl.BlockSpec((1, 1, D), lambda b, i, j: (b, 0, 1)),
        pl.BlockSpec((D, tn), lambda b, i, j: (0, j)),
        pl.BlockSpec((1, tn), lambda b, i, j: (0, j)),
    ]
    if n_f32_cols:
        n_bf16 = (N - n_f32_cols) // tn
        out_specs = [pl.BlockSpec((1, tm, tn), lambda b, i, j: (b, i, jnp.minimum(j, n_bf16 - 1))),
                     pl.BlockSpec((1, tm, tn), lambda b, i, j: (b, i, jnp.maximum(j - n_bf16, 0)))]
        out_shape = [jax.ShapeDtypeStruct((B, S, N - n_f32_cols), BF16),
                     jax.ShapeDtypeStruct((B, S, n_f32_cols), F32)]
    else:
        n_bf16 = None
        out_specs = pl.BlockSpec((1, tm, tn), lambda b, i, j: (b, i, j))
        out_shape = jax.ShapeDtypeStruct((B, S, N), BF16)
    return pl.pallas_call(
        functools.partial(_inproj_kernel, n_bf16=n_bf16),
        grid=(B, S // tm, N // tn),
        in_specs=in_specs,
        out_specs=out_specs,
        out_shape=out_shape,
        scratch_shapes=[pltpu.VMEM((tm, D), BF16)],
        compiler_params=_params("parallel", "parallel", "arbitrary"),
        name="in_proj",
    )(x, mod, mod, w_bf16, colscale)


N_POS_PIECES = 3


def _diff_kernel(slope_ref, q_ref, k_ref, v_ref, cx_ref, lam_ref, g_ref, o_ref,
                 kx1, kx2, qx1, qx2, bdiag, m1_ref, l1_ref, a1_ref, m2_ref, l2_ref, a2_ref, *, t, n_chunks, lam_init):
    h = pl.program_id(1)
    i = pl.program_id(2)
    slope2 = slope_ref[h]

    @pl.when(i == 0)
    def _():
        cx = cx_ref[0]
        kx1[:, :HEAD_DIM] = k_ref[0, :, :HEAD_DIM]
        kx2[:, :HEAD_DIM] = k_ref[0, :, HEAD_DIM:]
        for c in range(n_chunks):
            kx1[c * t:(c + 1) * t, HEAD_DIM:] = cx
            kx2[c * t:(c + 1) * t, HEAD_DIM:] = cx
        d = lax.broadcasted_iota(I32, (t, t), 1) - lax.broadcasted_iota(I32, (t, t), 0)
        bdiag[...] = -slope2 * jnp.abs(d).astype(F32)

    q = q_ref[0]
    ones = jnp.where(lax.broadcasted_iota(I32, (t, HEAD_DIM), 1) < N_POS_PIECES, 1.0, 0.0)
    for side, sign in enumerate((1.0, 0.0, -1.0)):
        e = (sign * ones).astype(BF16)
        qx1[side, :, :HEAD_DIM] = q[:, :HEAD_DIM]
        qx1[side, :, HEAD_DIM:] = e
        qx2[side, :, :HEAD_DIM] = q[:, HEAD_DIM:]
        qx2[side, :, HEAD_DIM:] = e
    row_term = slope2 * (i * t + lax.broadcasted_iota(I32, (t, 1), 0)).astype(F32)

    m1_ref[...] = jnp.full(m1_ref.shape, NEG, F32)
    m2_ref[...] = jnp.full(m2_ref.shape, NEG, F32)
    l1_ref[...] = jnp.zeros(l1_ref.shape, F32)
    l2_ref[...] = jnp.zeros(l2_ref.shape, F32)
    a1_ref[...] = jnp.zeros(a1_ref.shape, F32)
    a2_ref[...] = jnp.zeros(a2_ref.shape, F32)

    def key_tile(c, side, row_sign, on_diagonal):
        k0 = pl.multiple_of(c * t, t)
        vc = v_ref[0, pl.ds(k0, t), :]
        shift = row_sign * (row_term - slope2 * k0.astype(F32))

        def softmax_update(s, m_ref, l_ref):
            if on_diagonal:
                s = s + bdiag[...]
            m_old = m_ref[...]
            m_new = jnp.maximum(m_old, jnp.max(s, axis=-1, keepdims=True) + shift)
            p = jnp.exp2(s - (m_new - shift))
            alpha = jnp.exp2(m_old - m_new)
            l_ref[...] = alpha * l_ref[...] + jnp.sum(p, axis=-1, keepdims=True)
            m_ref[...] = m_new
            return p.astype(BF16), alpha

        s1 = _nt_dot(qx1[side], kx1[pl.ds(k0, t), :])
        s2 = _nt_dot(qx2[side], kx2[pl.ds(k0, t), :])
        p1, alpha1 = softmax_update(s1, m1_ref, l1_ref)
        p2, alpha2 = softmax_update(s2, m2_ref, l2_ref)
        a1_ref[...] = alpha1 * a1_ref[...] + jnp.dot(p1, vc, preferred_element_type=F32)
        a2_ref[...] = alpha2 * a2_ref[...] + jnp.dot(p2, vc, preferred_element_type=F32)

    def before(c, carry):
        key_tile(c, 0, -1.0, False)
        return carry

    def after(c, carry):
        key_tile(c, 2, 1.0, False)
        return carry

    lax.fori_loop(0, i, before, 0)
    key_tile(i, 1, 0.0, True)
    lax.fori_loop(i + 1, n_chunks, after, 0)

    lv = lam_ref[...]
    s01 = jnp.sum(lv[0:1, :] * lv[1:2, :], axis=-1, keepdims=True)
    s23 = jnp.sum(lv[2:3, :] * lv[3:4, :], axis=-1, keepdims=True)
    lam = jnp.exp(s01) - jnp.exp(s23) + lam_init
    o = a1_ref[...] / l1_ref[...] - lam * (a2_ref[...] / l2_ref[...])
    ms = jnp.mean(o * o, axis=-1, keepdims=True)
    o = o * lax.rsqrt(ms + LN_EPS) * g_ref[...] * (1.0 - lam_init)
    o_ref[0] = o.astype(BF16)


def _diff_attention(proj, lam_vecs, subln_g, layer_idx):
    B, S, _ = proj.shape
    t = min(S, 1024)
    lam_init = 0.8 - 0.6 * math.exp(-0.3 * layer_idx)
    slopes2 = jnp.asarray(_alibi_slopes(A_HEADS) * LOG2E)
    rest = slopes2[:, None] * jnp.arange(t, dtype=F32)[None, :]
    pieces = []
    for _ in range(N_POS_PIECES):
        piece = rest.astype(BF16)
        pieces.append(piece)
        rest = rest - piece.astype(F32)
    cext = jnp.zeros((A_HEADS, t, HEAD_DIM), BF16).at[:, :, :N_POS_PIECES].set(jnp.stack(pieces, axis=-1))
    nq = A_QK // A_VDIM
    kern = functools.partial(_diff_kernel, t=t, n_chunks=S // t, lam_init=lam_init)
    return pl.pallas_call(
        kern,
        grid=(B, A_HEADS, S // t),
        in_specs=[
            pl.BlockSpec(memory_space=pltpu.SMEM),
            pl.BlockSpec((1, t, A_VDIM), lambda b, h, i: (b, i, h)),
            pl.BlockSpec((1, S, A_VDIM), lambda b, h, i: (b, 0, nq + h)),
            pl.BlockSpec((1, S, A_VDIM), lambda b, h, i: (b, 0, 2 * nq + h)),
            pl.BlockSpec((1, t, HEAD_DIM), lambda b, h, i: (h, 0, 0)),
            pl.BlockSpec((4, HEAD_DIM), lambda b, h, i: (0, 0)),
            pl.BlockSpec((1, A_VDIM), lambda b, h, i: (0, 0)),
        ],
        out_specs=pl.BlockSpec((1, t, A_VDIM), lambda b, h, i: (b, i, h)),
        out_shape=jax.ShapeDtypeStruct((B, S, A_V), BF16),
        scratch_shapes=[
            pltpu.VMEM((S, 2 * HEAD_DIM), BF16), pltpu.VMEM((S, 2 * HEAD_DIM), BF16),
            pltpu.VMEM((3, t, 2 * HEAD_DIM), BF16), pltpu.VMEM((3, t, 2 * HEAD_DIM), BF16),
            pltpu.VMEM((t, t), F32),
            pltpu.VMEM((t, 1), F32), pltpu.VMEM((t, 1), F32), pltpu.VMEM((t, A_VDIM), F32),
            pltpu.VMEM((t, 1), F32), pltpu.VMEM((t, 1), F32), pltpu.VMEM((t, A_VDIM), F32),
        ],
        compiler_params=_params("parallel", "parallel", "arbitrary"),
        name="diff_attn",
    )(slopes2, proj, proj, proj, cext, lam_vecs, subln_g.reshape(1, A_VDIM))


def _banded_kernel(slope_ref, sink_ref, q_ref, kp_ref, kc_ref, kn_ref, vp_ref, vc_ref, vn_ref, *rest,
                   tq, radius, n_q, group, n_rows, dist_scale, use_sink, want_lse):
    if want_lse:
        o_ref, lse_ref, kwin, vwin = rest
    else:
        o_ref, kwin, vwin = rest
        lse_ref = None
    t = pl.program_id(2)
    w = tq + 2 * radius
    kwin[0:radius, :] = kp_ref[0]
    kwin[radius:radius + tq, :] = kc_ref[0]
    kwin[radius + tq:w, :] = kn_ref[0]
    vwin[0:radius, :] = vp_ref[0]
    vwin[radius:radius + tq, :] = vc_ref[0]
    vwin[radius + tq:w, :] = vn_ref[0]

    ii = lax.broadcasted_iota(I32, (tq, w), 0)
    jj = lax.broadcasted_iota(I32, (tq, w), 1)
    rel = jnp.abs(jj - radius - ii)
    kpos = t * tq - radius + jj
    valid = (rel <= radius) & (kpos >= 0) & (kpos < n_rows)
    dist = rel.astype(F32) * float(dist_scale)
    lane = lax.broadcasted_iota(I32, (tq, HEAD_DIM), 1)
    lse_tile = jnp.zeros((tq, HEAD_DIM), F32)

    for h in range(n_q):
        hk = h // group
        qh = q_ref[0, :, h * HEAD_DIM:(h + 1) * HEAD_DIM]
        kh = kwin[:, hk * HEAD_DIM:(hk + 1) * HEAD_DIM]
        vh = vwin[:, hk * HEAD_DIM:(hk + 1) * HEAD_DIM]
        s = jnp.where(valid, _nt_dot(qh, kh) - slope_ref[h] * dist, NEG)
        m = jnp.max(s, axis=-1, keepdims=True)
        if use_sink:
            m = jnp.maximum(m, sink_ref[h])
        p = jnp.exp2(s - m)
        den = jnp.sum(p, axis=-1, keepdims=True)
        if use_sink:
            den = den + jnp.exp2(sink_ref[h] - m)
        o = jnp.dot(p.astype(BF16), vh, preferred_element_type=F32) / den
        o_ref[0, :, h * HEAD_DIM:(h + 1) * HEAD_DIM] = o.astype(BF16)
        if want_lse:
            lse_tile = jnp.where(lane == h, m + jnp.log2(den), lse_tile)
    if want_lse:
        lse_ref[0] = lse_tile


def _banded_attention(src, *, n_rows, n_res, src_cols, q_blk, k_blk, v_blk, n_q, group, radius,
                      dist_scale, slopes2, sink2, want_lse):
    B = src.shape[0]
    qw = n_q * HEAD_DIM
    kw = (n_q // group) * HEAD_DIM
    tq = min(n_rows, 256)
    nt = n_rows // tq
    per_t = tq // radius
    last_halo = n_rows // radius - 1
    qpg = src_cols // qw
    kpg = src_cols // kw

    def q_map(b, r, t):
        return (b, t, r * qpg + q_blk)

    def cur_map(blk):
        return lambda b, r, t: (b, t, r * kpg + blk)

    def prev_map(blk):
        return lambda b, r, t: (b, jnp.maximum(t * per_t - 1, 0), r * kpg + blk)

    def next_map(blk):
        return lambda b, r, t: (b, jnp.minimum((t + 1) * per_t, last_halo), r * kpg + blk)

    out_shape = [jax.ShapeDtypeStruct((B, n_rows, n_res * qw), BF16)]
    out_specs = [pl.BlockSpec((1, tq, qw), lambda b, r, t: (b, t, r))]
    if want_lse:
        out_shape.append(jax.ShapeDtypeStruct((B, n_rows, n_res * HEAD_DIM), F32))
        out_specs.append(pl.BlockSpec((1, tq, HEAD_DIM), lambda b, r, t: (b, t, r)))
    kern = functools.partial(_banded_kernel, tq=tq, radius=radius, n_q=n_q, group=group, n_rows=n_rows,
                             dist_scale=dist_scale, use_sink=sink2 is not None, want_lse=want_lse)
    if sink2 is None:
        sink2 = jnp.zeros((n_q,), F32)
    return pl.pallas_call(
        kern,
        grid=(B, n_res, nt),
        in_specs=[
            pl.BlockSpec(memory_space=pltpu.SMEM),
            pl.BlockSpec(memory_space=pltpu.SMEM),
            pl.BlockSpec((1, tq, qw), q_map),
            pl.BlockSpec((1, radius, kw), prev_map(k_blk)),
            pl.BlockSpec((1, tq, kw), cur_map(k_blk)),
            pl.BlockSpec((1, radius, kw), next_map(k_blk)),
            pl.BlockSpec((1, radius, kw), prev_map(v_blk)),
            pl.BlockSpec((1, tq, kw), cur_map(v_blk)),
            pl.BlockSpec((1, radius, kw), next_map(v_blk)),
        ],
        out_specs=out_specs,
        out_shape=out_shape,
        scratch_shapes=[pltpu.VMEM((tq + 2 * radius, kw), BF16), pltpu.VMEM((tq + 2 * radius, kw), BF16)],
        compiler_params=_params("parallel", "parallel", "parallel"),
        name="banded_attn",
    )(slopes2, sink2, src, src, src, src, src, src, src)


def _band_bias(tq, wlen, offset, radius, slope_dist):
    rel = jnp.abs(offset + lax.broadcasted_iota(I32, (tq, wlen), 1) - lax.broadcasted_iota(I32, (tq, wlen), 0))
    return jnp.where(rel <= radius, -slope_dist * rel.astype(F32), NEG)


def _band_tile(qf, kf, vf, bias):
    s = _nt_dot(qf, kf) + bias
    m = jnp.max(s, axis=-1, keepdims=True)
    p = jnp.exp2(s - m)
    return m, jnp.sum(p, axis=-1, keepdims=True), jnp.dot(p.astype(BF16), vf, preferred_element_type=F32)


TILE_UNROLL = 4


def _loop_tiles(n, tile_fn):
    unroll = TILE_UNROLL if n % TILE_UNROLL == 0 else 1

    def body(i, carry):
        for u in range(unroll):
            tile_fn(i * unroll + u)
        return carry

    lax.fori_loop(0, n // unroll, body, 0)


def _band_geometry(seq, tq, radius, dil):
    n_rows = seq // dil
    tile = min(tq, n_rows)
    nt = n_rows // tile
    wlen = min(tile + 2 * radius, n_rows)
    offsets = (0,) if nt == 1 else (0, -radius, tile - wlen)
    return n_rows, tile, nt, wlen, offsets


def _dilated_kernel(slope_ref, q_ref, k_ref, v_ref, o_ref, acc_s, m_s, l_s, bias_s, *, seq, tq, branches):
    slope = slope_ref[pl.program_id(1)]
    lanes = (tq, HEAD_DIM)

    first_bias = []
    n_bias = 0
    for radius, dil in branches:
        _, tile, _, wlen, offsets = _band_geometry(seq, tq, radius, dil)
        first_bias.append(n_bias)
        for off in offsets:
            bias_s[n_bias, 0:tile, 0:wlen] = _band_bias(tile, wlen, off, radius, slope * dil)
            n_bias += 1

    def tile_window(bi, t):
        radius, dil = branches[bi]
        n_rows, tile, nt, wlen, _ = _band_geometry(seq, tq, radius, dil)
        q0 = t * tile
        k0 = jnp.clip(q0 - radius, 0, n_rows - wlen)
        case = 0 if nt == 1 else jnp.where(t == 0, 0, jnp.where(t == nt - 1, 2, 1))
        return q0, k0, bias_s[first_bias[bi] + case, 0:tile, 0:wlen]

    for bi, (radius, dil) in enumerate(branches[:-1]):
        _, tile, nt, wlen, _ = _band_geometry(seq, tq, radius, dil)

        def one_tile(it, bi=bi, dil=dil, tile=tile, nt=nt, wlen=wlen):
            r = it // nt
            q0, k0, bias = tile_window(bi, it % nt)
            qrows = pl.ds(r + dil * q0, tile, stride=dil)
            krows = pl.ds(r + dil * k0, wlen, stride=dil)
            m, l, acc = _band_tile(q_ref[0, qrows, :].astype(BF16), k_ref[0, krows, :].astype(BF16),
                                   v_ref[0, krows, :].astype(BF16), bias)
            acc_s[bi, qrows, :] = acc
            m_s[bi, qrows, :] = jnp.broadcast_to(m, (tile, HEAD_DIM))
            l_s[bi, qrows, :] = jnp.broadcast_to(l, (tile, HEAD_DIM))

        _loop_tiles(dil * nt, one_tile)

    assert branches[-1][1] == 1
    wlen = _band_geometry(seq, tq, *branches[-1])[3]

    def dense(t):
        q0, k0, bias = tile_window(len(branches) - 1, t)
        qrows = pl.ds(pl.multiple_of(q0, tq), tq)
        krows = pl.ds(pl.multiple_of(k0, 8), wlen)
        m, l, acc = _band_tile(q_ref[0, qrows, :].astype(BF16), k_ref[0, krows, :].astype(BF16),
                               v_ref[0, krows, :].astype(BF16), bias)
        ms = [jnp.broadcast_to(m, lanes)] + [m_s[bi, qrows, :] for bi in range(len(branches) - 1)]
        ls = [jnp.broadcast_to(l, lanes)] + [l_s[bi, qrows, :] for bi in range(len(branches) - 1)]
        accs = [acc] + [acc_s[bi, qrows, :] for bi in range(len(branches) - 1)]
        top = functools.reduce(jnp.maximum, ms)
        es = [jnp.exp2(mi - top) for mi in ms]
        num = sum(e * a for e, a in zip(es, accs))
        den = sum(e * li for e, li in zip(es, ls))
        o_ref[0, qrows, :] = (num / den).astype(BF16)

    _loop_tiles(seq // tq, dense)


def _dilated_mixture(projf, slopes2):
    B, S, _ = projf.shape
    branches = tuple((w // (2 * d), d) for w, d in sorted(B_BRANCHES, key=lambda wd: -wd[1]))
    tq = min(S, 256)
    kern = functools.partial(_dilated_kernel, seq=S, tq=tq, branches=branches)
    nb = len(branches) - 1
    geo = [_band_geometry(S, tq, radius, dil) for radius, dil in branches]
    bias_shape = (sum(len(g[4]) for g in geo), tq, max(g[3] for g in geo))
    col = lambda part: (lambda b, h: (b, 0, part * B_HEADS + h))
    return pl.pallas_call(
        kern,
        grid=(B, B_HEADS),
        in_specs=[pl.BlockSpec(memory_space=pltpu.SMEM)] + [pl.BlockSpec((1, S, HEAD_DIM), col(p)) for p in range(3)],
        out_specs=pl.BlockSpec((1, S, HEAD_DIM), col(0)),
        out_shape=jax.ShapeDtypeStruct((B, S, B_W), BF16),
        scratch_shapes=[pltpu.VMEM((nb, S, HEAD_DIM), F32)] * 3 + [pltpu.VMEM(bias_shape, F32)],
        compiler_params=_params("parallel", "parallel"),
        name="dilated_mix",
    )(slopes2, projf, projf, projf)


def _post_attention(y, x, g1, lng, lnb, sc2, sh2, rwh_ref, rwl_ref, x1_ref, hp_ref, lg_ref, alpha):
    z = alpha * x + g1 * y
    mu = jnp.mean(z, axis=-1, keepdims=True)
    zc = z - mu
    var = jnp.mean(zc * zc, axis=-1, keepdims=True)
    x1 = zc * lax.rsqrt(var + LN_EPS) * lng + lnb
    x1_ref[0] = x1
    h2 = x1 * (1.0 + sc2) + sh2
    hb = h2.astype(BF16)
    hf = hb.astype(F32)
    bits = lax.bitcast_convert_type(hf, U32)
    half = bits.shape[1] // 2
    packed = bits[:, :half] | (bits[:, half:] >> 16)
    nck = half // HEAD_DIM
    for ck in range(nck):
        hp_ref[0, pl.ds(ck, bits.shape[0], stride=nck), :] = packed[:, ck * HEAD_DIM:(ck + 1) * HEAD_DIM]
    lo = (h2 - hf).astype(BF16)
    rwh = rwh_ref[...]
    lg_ref[0] = (jnp.dot(hb, rwh, preferred_element_type=F32)
                 + jnp.dot(hb, rwl_ref[...], preferred_element_type=F32)
                 + jnp.dot(lo, rwh, preferred_element_type=F32))


def _outproj_ab_kernel(ya_ref, yb_ref, w_ref, x_ref, g1_ref, lng_ref, lnb_ref, sh2_ref, sc2_ref, rwh_ref, rwl_ref,
                       x1_ref, hp_ref, lg_ref, *, alpha):
    y = (jnp.dot(ya_ref[0], w_ref[0:A_V, :], preferred_element_type=F32)
         + jnp.dot(yb_ref[0], w_ref[A_V:A_V + B_W, :], preferred_element_type=F32))
    _post_attention(y, x_ref[0], g1_ref[0], lng_ref[...], lnb_ref[...], sc2_ref[0], sh2_ref[0],
                    rwh_ref, rwl_ref, x1_ref, hp_ref, lg_ref, alpha)


def _outproj_c_kernel(o_ref, w_ref, x_ref, g1_ref, lng_ref, lnb_ref, sh2_ref, sc2_ref, rwh_ref, rwl_ref,
                      x1_ref, hp_ref, lg_ref, *, alpha):
    y = jnp.dot(o_ref[0], w_ref[...], preferred_element_type=F32)
    _post_attention(y, x_ref[0], g1_ref[0], lng_ref[...], lnb_ref[...], sc2_ref[0], sh2_ref[0],
                    rwh_ref, rwl_ref, x1_ref, hp_ref, lg_ref, alpha)


def _out_projection(attn_inputs, w_bf16, x, mod, lng, lnb, rw_hi, rw_lo, alpha, mixer):
    B, S, D = x.shape
    E = rw_hi.shape[1]
    tm = min(S, 512)
    row = lambda b, i: (b, i, 0)
    const2 = lambda b, i: (0, 0)
    tail_specs = [
        pl.BlockSpec(w_bf16.shape, const2),
        pl.BlockSpec((1, tm, D), row),
        pl.BlockSpec((1, 1, D), lambda b, i: (b, 0, 2)),
        pl.BlockSpec((1, D), const2),
        pl.BlockSpec((1, D), const2),
        pl.BlockSpec((1, 1, D), lambda b, i: (b, 0, 3)),
        pl.BlockSpec((1, 1, D), lambda b, i: (b, 0, 4)),
        pl.BlockSpec((D, E), const2),
        pl.BlockSpec((D, E), const2),
    ]
    tail_args = (w_bf16, x, mod, lng.reshape(1, D), lnb.reshape(1, D), mod, mod, rw_hi, rw_lo)
    nck = D // 2 // HEAD_DIM
    out_shape = [jax.ShapeDtypeStruct((B, S, D), F32), jax.ShapeDtypeStruct((B, S * nck, HEAD_DIM), U32),
                 jax.ShapeDtypeStruct((B, S, E), F32)]
    out_specs = [pl.BlockSpec((1, tm, D), row), pl.BlockSpec((1, tm * nck, HEAD_DIM), row),
                 pl.BlockSpec((1, tm, E), row)]
    if mixer == "ab":
        ya, yb = attn_inputs
        in_specs = [pl.BlockSpec((1, tm, A_V), row), pl.BlockSpec((1, tm, B_W), row)] + tail_specs
        kern = functools.partial(_outproj_ab_kernel, alpha=alpha)
        scratch = []
        args = (ya, yb) + tail_args
    else:
        (o,) = attn_inputs
        in_specs = [pl.BlockSpec((1, tm, o.shape[2]), row)] + tail_specs
        kern = functools.partial(_outproj_c_kernel, alpha=alpha)
        scratch = []
        args = (o,) + tail_args
    return pl.pallas_call(
        kern,
        grid=(B, S // tm),
        in_specs=in_specs,
        out_specs=out_specs,
        out_shape=out_shape,
        scratch_shapes=scratch,
        compiler_params=_params("parallel", "parallel"),
        name="out_proj_" + mixer,
    )(*args)


def _excl_cumsum_lanes(x, tri):
    n = x.shape[1] // HEAD_DIM
    carry = jnp.zeros((x.shape[0], 1), F32)
    outs = []
    for c in range(n):
        xc = x[:, c * HEAD_DIM:(c + 1) * HEAD_DIM]
        outs.append(jnp.dot(xc.astype(BF16), tri, preferred_element_type=F32) + carry)
        carry = carry + jnp.sum(xc, axis=1, keepdims=True)
    return jnp.concatenate(outs, axis=1)


def _tables_kernel(lg_ref, cnt_ref, off_ref, tok_ref, gate_ref, place_ref, *, n_exp, cap):
    lg = lg_ref[0]
    S = lg.shape[1]
    ex = jnp.exp(lg - jnp.max(lg, axis=0, keepdims=True))
    aff = ex / jnp.sum(ex, axis=0, keepdims=True)
    bits = lax.bitcast_convert_type(aff, I32)

    def search(i, cur):
        cand = cur | jnp.left_shift(jnp.int32(1), 30 - i)
        cnt = jnp.sum(jnp.where(bits >= cand, 1.0, 0.0), axis=1, keepdims=True)
        return jnp.where(cnt >= cap, cand, cur)

    thr = lax.fori_loop(0, 31, search, jnp.zeros((n_exp, 1), I32))
    tri = jnp.where(lax.broadcasted_iota(I32, (HEAD_DIM, HEAD_DIM), 0)
                    < lax.broadcasted_iota(I32, (HEAD_DIM, HEAD_DIM), 1), 1.0, 0.0).astype(BF16)
    gt = bits > thr
    eq = jnp.where(bits == thr, 1.0, 0.0)
    need = cap - jnp.sum(jnp.where(gt, 1.0, 0.0), axis=1, keepdims=True)
    sel = jnp.where(gt, 1.0, jnp.where(_excl_cumsum_lanes(eq, tri) < need, eq, 0.0))
    pos = _excl_cumsum_lanes(sel, tri)

    ranks = []
    run = jnp.zeros((1, S), F32)
    for e in range(n_exp):
        ranks.append(run)
        run = run + sel[e:e + 1, :]
    cnt_ref[0] = run.astype(I32)
    first = _excl_cumsum_lanes(run, tri) + (pl.program_id(0) * (n_exp * cap)).astype(F32)
    off_ref[0] = first.astype(I32)

    jcol = lax.broadcasted_iota(I32, (cap, S), 0).astype(F32)
    token = (lax.broadcasted_iota(I32, (1, S), 1) + pl.program_id(0) * S).astype(F32)
    toks, gates, places = [], [], []
    for e in range(n_exp):
        mine = jnp.where(sel[e:e + 1, :] > 0.0, pos[e:e + 1, :], -1.0) == jcol
        pick = lambda row: jnp.sum(jnp.where(mine, row, 0.0), axis=1, keepdims=True)
        toks.append(pick(token))
        gates.append(pick(aff[e:e + 1, :]))
        places.append(pick(first + ranks[e]))
    tok_ref[0] = jnp.concatenate(toks, axis=1).astype(I32)
    gate_ref[0] = jnp.concatenate(gates, axis=1)
    place_ref[0] = jnp.concatenate(places, axis=1).astype(I32)


def _routing_tables(logits_t, cap):
    B, E, S = logits_t.shape
    kern = functools.partial(_tables_kernel, n_exp=E, cap=cap)
    per_token = pl.BlockSpec((1, 1, S), lambda b: (b, 0, 0))
    per_slot = pl.BlockSpec((1, cap, E), lambda b: (b, 0, 0))
    cnt, off, tok, gate, place = pl.pallas_call(
        kern,
        grid=(B,),
        in_specs=[pl.BlockSpec((1, E, S), lambda b: (b, 0, 0))],
        out_specs=[per_token, per_token, per_slot, per_slot, per_slot],
        out_shape=[jax.ShapeDtypeStruct((B, 1, S), I32), jax.ShapeDtypeStruct((B, 1, S), I32),
                   jax.ShapeDtypeStruct((B, cap, E), I32), jax.ShapeDtypeStruct((B, cap, E), F32),
                   jax.ShapeDtypeStruct((B, cap, E), I32)],
        compiler_params=_params("parallel"),
        name="routing_tables",
    )(logits_t)
    expert_major = lambda a: jnp.transpose(a, (2, 0, 1)).reshape(E * B * cap)
    return cnt, off, expert_major(tok), expert_major(gate), expert_major(place)


def _token_rows(t, nck):
    start = t * nck
    return pl.ds(pl.multiple_of(start, nck) if nck > 1 else start, nck)


def _unpack_rows(xp_ref, first, xs_ref):
    rows, d = xs_ref.shape
    half = d // 2
    nck = half // HEAD_DIM
    for ck in range(nck):
        u = xp_ref[pl.ds(first * nck + ck, rows, stride=nck), :]
        lo = ck * HEAD_DIM
        xs_ref[:, lo:lo + HEAD_DIM] = lax.bitcast_convert_type(u & jnp.uint32(0xFFFF0000), F32).astype(BF16)
        xs_ref[:, half + lo:half + lo + HEAD_DIM] = lax.bitcast_convert_type(u << 16, F32).astype(BF16)


def _pack_pairs(hi, lo):
    hb = lax.bitcast_convert_type(hi.astype(BF16).astype(F32), U32)
    lb = lax.bitcast_convert_type(lo.astype(BF16).astype(F32), U32)
    return hb | (lb >> 16)


ROW_UNROLL = 8


def _ffn_up_kernel(tok_ref, tokn_ref, hp_ref, wg_ref, wu_ref, o_ref, xraw, xs_ref, sem, *, tm, mt, n_tiles, nck,
                   n_steps):
    tile = pl.program_id(0) * mt + pl.program_id(1)
    step = pl.program_id(2)
    par = tile % 2

    def request(idx_ref, slot, r):
        pltpu.make_async_copy(hp_ref.at[_token_rows(idx_ref[r], nck)],
                              xraw.at[_token_rows(slot * tm + r, nck)], sem.at[slot]).start()

    def arrived(slot):
        return pltpu.make_async_copy(hp_ref.at[pl.ds(0, tm * nck)], xraw.at[pl.ds(slot * tm * nck, tm * nck)],
                                     sem.at[slot])

    @pl.when(step == 0)
    def _():
        @pl.when(tile == 0)
        def _():
            def body(i, carry):
                for u in range(ROW_UNROLL):
                    request(tok_ref, 0, i * ROW_UNROLL + u)
                return carry

            lax.fori_loop(0, tm // ROW_UNROLL, body, 0)

        arrived(par).wait()
        _unpack_rows(xraw, par * tm, xs_ref)

    per_step = tm // n_steps
    for u in range(per_step):
        request(tokn_ref, 1 - par, step * per_step + u)

    x = xs_ref[...]
    g = jnp.dot(x, wg_ref[0, 0].astype(BF16), preferred_element_type=F32)
    u = jnp.dot(x, wu_ref[0, 0].astype(BF16), preferred_element_type=F32)
    o_ref[...] = (g * jax.nn.sigmoid(g) * u).astype(BF16)

    @pl.when((tile == n_tiles - 1) & (step == n_steps - 1))
    def _():
        arrived(1 - par).wait()


def _ffn_up(hp, tok_rows, w_gate, w_up, layer, rows_per_expert):
    _, E, D, Fh = w_gate.shape
    tm = min(rows_per_expert, 2048)
    mt = rows_per_expert // tm
    n_tiles = E * mt
    tn = min(Fh, 256)
    nck = D // 2 // HEAD_DIM
    kern = functools.partial(_ffn_up_kernel, tm=tm, mt=mt, n_tiles=n_tiles, nck=nck, n_steps=Fh // tn)
    smem = functools.partial(pl.BlockSpec, memory_space=pltpu.SMEM)
    return pl.pallas_call(
        kern,
        grid=(E, mt, Fh // tn),
        in_specs=[smem((tm,), lambda e, m, n: (e * mt + m,)),
                  smem((tm,), lambda e, m, n: (jnp.minimum(e * mt + m + 1, n_tiles - 1),)),
                  pl.BlockSpec(memory_space=pl.ANY),
                  pl.BlockSpec((1, 1, D, tn), lambda e, m, n: (layer, e, 0, n)),
                  pl.BlockSpec((1, 1, D, tn), lambda e, m, n: (layer, e, 0, n))],
        out_specs=pl.BlockSpec((tm, tn), lambda e, m, n: (e * mt + m, n)),
        out_shape=jax.ShapeDtypeStruct((E * rows_per_expert, Fh), BF16),
        scratch_shapes=[pltpu.VMEM((2 * tm * nck, HEAD_DIM), U32), pltpu.VMEM((tm, D), BF16),
                        pltpu.SemaphoreType.DMA((2,))],
        compiler_params=_params("arbitrary", "arbitrary", "arbitrary"),
        name="ffn_up",
    )(tok_rows, tok_rows, hp, w_gate, w_up)


def _ffn_down_kernel(pos_ref, h_ref, wa_ref, wb_ref, g_ref, yt_ref, ybuf, sem, *, n_steps, cps, tm, mt, n_tiles,
                     nck):
    tile = pl.program_id(0) * mt + pl.program_id(1)
    par = tile % 2
    h = h_ref[...]
    g = g_ref[...]
    ya = jnp.dot(h, wa_ref[0, 0].astype(BF16), preferred_element_type=F32) * g
    yb = jnp.dot(h, wb_ref[0, 0].astype(BF16), preferred_element_type=F32) * g
    packed = _pack_pairs(ya, yb)
    n = pl.program_id(2)
    for k in range(n_steps):
        @pl.when(n == k)
        def _():
            for cc in range(cps):
                ybuf[pl.ds(par * tm * nck + k * cps + cc, tm, stride=nck), :] = (
                    packed[:, cc * HEAD_DIM:(cc + 1) * HEAD_DIM])

    def scatter_done(slot):
        return pltpu.make_async_copy(ybuf.at[pl.ds(slot * tm * nck, tm * nck)], yt_ref.at[pl.ds(0, tm * nck)],
                                     sem.at[slot])

    @pl.when(n == n_steps - 1)
    def _():
        @pl.when(tile > 0)
        def _():
            scatter_done(1 - par).wait()

        def body(i, carry):
            for u in range(ROW_UNROLL):
                r = i * ROW_UNROLL + u
                pltpu.make_async_copy(ybuf.at[_token_rows(par * tm + r, nck)],
                                      yt_ref.at[_token_rows(pos_ref[r], nck)], sem.at[par]).start()
            return carry

        lax.fori_loop(0, tm // ROW_UNROLL, body, 0)

        @pl.when(tile == n_tiles - 1)
        def _():
            scatter_done(par).wait()


def _ffn_down(hid, w_down, layer, gate_col, pos_rows, rows_per_expert):
    _, E, Fh, D = w_down.shape
    tm = min(rows_per_expert, 2048)
    mt = rows_per_expert // tm
    n_tiles = E * mt
    half = D // 2
    tnh = min(half, 256)
    n_steps = half // tnh
    nck = half // HEAD_DIM
    kern = functools.partial(_ffn_down_kernel, n_steps=n_steps, cps=tnh // HEAD_DIM, tm=tm, mt=mt, n_tiles=n_tiles,
                             nck=nck)
    return pl.pallas_call(
        kern,
        grid=(E, mt, n_steps),
        in_specs=[pl.BlockSpec((tm,), lambda e, m, n: (e * mt + m,), memory_space=pltpu.SMEM),
                  pl.BlockSpec((tm, Fh), lambda e, m, n: (e * mt + m, 0)),
                  pl.BlockSpec((1, 1, Fh, tnh), lambda e, m, n: (layer, e, 0, n)),
                  pl.BlockSpec((1, 1, Fh, tnh), lambda e, m, n: (layer, e, 0, n + n_steps)),
                  pl.BlockSpec((tm, 1), lambda e, m, n: (e * mt + m, 0))],
        out_specs=pl.BlockSpec(memory_space=pl.ANY),
        out_shape=jax.ShapeDtypeStruct((E * rows_per_expert * nck, HEAD_DIM), U32),
        scratch_shapes=[pltpu.VMEM((2 * tm * nck, HEAD_DIM), U32), pltpu.SemaphoreType.DMA((2,))],
        compiler_params=_params("arbitrary", "arbitrary", "arbitrary"),
        name="ffn_down",
    )(pos_rows, hid, w_down, w_down, gate_col)


def _combine_kernel(offt_ref, x_ref, off_ref, cnt_ref, g2_ref, lng_ref, lnb_ref, yt_ref, o_ref,
                    buf, rows, acc, sem, *, tm, ch, n_total, n_tiles, n_tiles_s, alpha, nck):
    tile = pl.program_id(0) * n_tiles_s + pl.program_id(1)
    o0 = offt_ref[tile]
    o1 = offt_ref[tile + 1]
    n_chunks = jnp.maximum((o1 - o0 + ch - 1) // ch, 1)
    lo_col = off_ref[0]
    hi_col = lo_col + cnt_ref[0]
    lane = lax.broadcasted_iota(I32, (tm, ch), 1)
    acc[...] = jnp.zeros(acc.shape, F32)

    def fetch(want, slot):
        start = jnp.minimum(want, n_total - ch) * nck
        if nck > 1:
            start = pl.multiple_of(start, nck)
        return pltpu.make_async_copy(yt_ref.at[pl.ds(start, ch * nck)], buf.at[pl.ds(slot * ch * nck, ch * nck)],
                                     sem.at[slot])

    @pl.when(tile == 0)
    def _():
        fetch(o0, 0).start()

    def chunk(c, carry):
        slot = c % 2
        want = o0 + c * ch
        fetch(want, slot).wait()

        @pl.when(c + 1 < n_chunks)
        def _():
            fetch(want + ch, 1 - slot).start()

        _unpack_rows(buf, slot * ch, rows)
        row = lane + jnp.minimum(want, n_total - ch)
        own = (row >= lo_col) & (row < hi_col) & (row >= want)
        acc[...] += jnp.dot(jnp.where(own, 1.0, 0.0).astype(BF16), rows[...], preferred_element_type=F32)
        return carry

    lax.fori_loop(0, n_chunks, chunk, 0)

    @pl.when(tile + 1 < n_tiles)
    def _():
        fetch(o1, 0).start()

    z = alpha * x_ref[0] + g2_ref[0] * acc[...]
    mu = jnp.mean(z, axis=-1, keepdims=True)
    zc = z - mu
    var = jnp.mean(zc * zc, axis=-1, keepdims=True)
    o_ref[0] = zc * lax.rsqrt(var + LN_EPS) * lng_ref[...] + lnb_ref[...]


def _combine(x1, mod, lng, lnb, yt, off, cnt, alpha):
    B, S, D = x1.shape
    nck = D // 2 // HEAD_DIM
    n_total = yt.shape[0] // nck
    tm = min(S, 256)
    ch = min(n_total, 512)
    n_tiles_s = S // tm
    off_flat = off.reshape(B * S)
    offt = jnp.concatenate([off_flat[::tm], jnp.full((1,), n_total, I32)])
    kern = functools.partial(_combine_kernel, tm=tm, ch=ch, n_total=n_total, n_tiles=B * n_tiles_s,
                             n_tiles_s=n_tiles_s, alpha=alpha, nck=nck)
    grid_spec = pltpu.PrefetchScalarGridSpec(
        num_scalar_prefetch=1,
        grid=(B, n_tiles_s),
        in_specs=[
            pl.BlockSpec((1, tm, D), lambda b, i, o: (b, i, 0)),
            pl.BlockSpec((1, tm, 1), lambda b, i, o: (b, i, 0)),
            pl.BlockSpec((1, tm, 1), lambda b, i, o: (b, i, 0)),
            pl.BlockSpec((1, 1, D), lambda b, i, o: (b, 0, 5)),
            pl.BlockSpec((1, D), lambda b, i, o: (0, 0)),
            pl.BlockSpec((1, D), lambda b, i, o: (0, 0)),
            pl.BlockSpec(memory_space=pl.ANY),
        ],
        out_specs=pl.BlockSpec((1, tm, D), lambda b, i, o: (b, i, 0)),
        scratch_shapes=[pltpu.VMEM((2 * ch * nck, HEAD_DIM), U32), pltpu.VMEM((ch, D), BF16),
                        pltpu.VMEM((tm, D), F32), pltpu.SemaphoreType.DMA((2,))],
    )
    return pl.pallas_call(
        kern,
        grid_spec=grid_spec,
        out_shape=jax.ShapeDtypeStruct((B, S, D), F32),
        compiler_params=_params("arbitrary", "arbitrary"),
        name="moe_combine",
    )(offt, x1, off.reshape(B, S, 1), cnt.reshape(B, S, 1), mod, lng.reshape(1, D), lnb.reshape(1, D), yt)


def _moe_sublayer(x1, hp, logits, mod, lng, lnb, w_gate, w_up, w_down, layer, alpha):
    B, S, D = x1.shape
    E = logits.shape[-1]
    cap = EC_CAPACITY_FACTOR * S // E
    cnt, off, tok, grow, pos = _routing_tables(jnp.swapaxes(logits, 1, 2), cap)
    hid = _ffn_up(hp.reshape(-1, HEAD_DIM), tok, w_gate, w_up, layer, B * cap)
    yt = _ffn_down(hid, w_down, layer, grow.reshape(E * B * cap, 1), pos, B * cap)
    return _combine(x1, mod, lng, lnb, yt, off, cnt, alpha)


def _split_bf16(w):
    hi = w.astype(BF16)
    return hi, (w - hi.astype(F32)).astype(BF16)


def kernel(x, c, ada_w, ada_b, ln_g, ln_b, ab_w_in, ab_w_out, diff_lambda, diff_subln_g, c_w_in, c_w_out,
           c_sink, router_w, w_gate, w_up, w_down):
    B, S, D = x.shape
    depth = ada_w.shape[0]
    alpha = (2.0 * depth) ** 0.25
    qscale = HEAD_DIM ** -0.5 * LOG2E
    mod_all = _modulation(c, ada_w, ada_b)

    ab_scale = np.ones((1, AB_IN), np.float32)
    ab_scale[:, :A_QK] = qscale
    ab_scale[:, 2 * A_QK + A_V:2 * A_QK + A_V + B_W] = qscale
    c_scale = np.ones((1, C_IN), np.float32)
    c_scale[:, :C_QW] = qscale

    for l in range(depth):
        mod = mod_all[l][:, None, :]
        i = l // 2
        rw_hi, rw_lo = _split_bf16(router_w[l])
        if l % 2 == 0:
            proj, projf = _in_projection(x, mod, ab_w_in[i].astype(BF16), jnp.asarray(ab_scale), n_f32_cols=3 * B_W)
            ya = _diff_attention(proj, diff_lambda[i], diff_subln_g[i], l)
            yb = _dilated_mixture(projf, jnp.asarray(_alibi_slopes(B_HEADS) * LOG2E))
            x1, hp, logits = _out_projection((ya, yb), ab_w_out[i].astype(BF16), x, mod,
                                             ln_g[l, 0], ln_b[l, 0], rw_hi, rw_lo, alpha, "ab")
        else:
            proj = _in_projection(x, mod, c_w_in[i].astype(BF16), jnp.asarray(c_scale))
            (o,) = _banded_attention(
                proj, n_rows=S, n_res=1, src_cols=C_IN, q_blk=0, k_blk=C_QW // C_KVW, v_blk=C_QW // C_KVW + 1,
                n_q=C_Q_HEADS, group=C_Q_HEADS // C_KV_HEADS, radius=C_RADIUS, dist_scale=1,
                slopes2=jnp.asarray(_alibi_slopes(C_Q_HEADS) * LOG2E), sink2=c_sink[i] * LOG2E, want_lse=False)
            x1, hp, logits = _out_projection((o,), c_w_out[i].astype(BF16), x, mod, ln_g[l, 0], ln_b[l, 0],
                                             rw_hi, rw_lo, alpha, "c")
        x = _moe_sublayer(x1, hp, logits, mod, ln_g[l, 1], ln_b[l, 1], w_gate, w_up, w_down, l, alpha)
    return x
```

```python
import functools
import math

import numpy as np
import jax
import jax.numpy as jnp
from jax import lax
from jax.experimental import pallas as pl
from jax.experimental.pallas import tpu as pltpu

F32 = jnp.float32
BF16 = jnp.bfloat16
I32 = jnp.int32
U32 = jnp.uint32

HEAD_DIM = 128
A_HEADS = 4
A_VDIM = 2 * HEAD_DIM
B_HEADS = 8
B_BRANCHES = ((128, 1), (512, 4), (2048, 16))
C_Q_HEADS = 16
C_KV_HEADS = 4
C_RADIUS = 128
EC_CAPACITY_FACTOR = 2
LN_EPS = 1e-5
NEG = -1e30
LOG2E = 1.4426950408889634

A_QK = A_HEADS * 2 * HEAD_DIM
A_V = A_HEADS * A_VDIM
B_W = B_HEADS * HEAD_DIM
AB_IN = 2 * A_QK + A_V + 3 * B_W
C_QW = C_Q_HEADS * HEAD_DIM
C_KVW = C_KV_HEADS * HEAD_DIM
C_IN = C_QW + 2 * C_KVW

VMEM_LIMIT_BYTES = 56 * 1024 * 1024


def _params(*sem):
    return pltpu.CompilerParams(dimension_semantics=sem, vmem_limit_bytes=VMEM_LIMIT_BYTES)


def _tile(n, preferred):
    t = min(n, preferred)
    while n % t:
        t //= 2
    return t


def _alibi_slopes(n):
    return np.array([2.0 ** (-8.0 * (i + 1) / n) for i in range(n)], dtype=np.float32)


def _nt_dot(a, b):
    return lax.dot_general(a, b, (((1,), (1,)), ((), ())), preferred_element_type=F32)


def _mod_kernel(c_ref, w_ref, b_ref, o_ref):
    c = c_ref[...]
    cs = (c * jax.nn.sigmoid(c)).astype(BF16)
    o_ref[0] = jnp.dot(cs, w_ref[0].astype(BF16), preferred_element_type=F32) + b_ref[0]


def _modulation(c, ada_w, ada_b):
    L, D, N = ada_w.shape
    B = c.shape[0]
    tn = _tile(N, 1024)
    return pl.pallas_call(
        _mod_kernel,
        grid=(L, N // tn),
        in_specs=[
            pl.BlockSpec((B, D), lambda l, j: (0, 0)),
            pl.BlockSpec((1, D, tn), lambda l, j: (l, 0, j)),
            pl.BlockSpec((1, 1, tn), lambda l, j: (l, 0, j)),
        ],
        out_specs=pl.BlockSpec((1, B, tn), lambda l, j: (l, 0, j)),
        out_shape=jax.ShapeDtypeStruct((L, B, N), F32),
        compiler_params=_params("parallel", "parallel"),
        name="adaln_mod",
    )(c, ada_w, ada_b.reshape(L, 1, N))


def _inproj_kernel(x_ref, sh_ref, sc_ref, w_ref, cs_ref, o_ref, *rest, n_bf16):
    h_ref = rest[-1]
    j = pl.program_id(2)

    @pl.when(j == 0)
    def _():
        h_ref[...] = (x_ref[0] * (1.0 + sc_ref[0]) + sh_ref[0]).astype(BF16)

    acc = jnp.dot(h_ref[...], w_ref[...], preferred_element_type=F32) * cs_ref[...]
    if n_bf16 is None:
        o_ref[0] = acc.astype(BF16)
    else:
        of_ref = rest[0]

        @pl.when(j < n_bf16)
        def _():
            o_ref[0] = acc.astype(BF16)

        @pl.when(j >= n_bf16)
        def _():
            of_ref[0] = acc


def _in_projection(x, mod, w_bf16, colscale, n_f32_cols=0):
    B, S, D = x.shape
    N = w_bf16.shape[1]
    tm = min(S, 1024)
    tn = _tile(N, 1024)
    in_specs = [
        pl.BlockSpec((1, tm, D), lambda b, i, j: (b, i, 0)),
        pl.BlockSpec((1, 1, D), lambda b, i, j: (b, 0, 0)),
        pl.BlockSpec((1, 1, D), lambda b, i, j: (b, 0, 1)),
        pl.BlockSpec((D, tn), lambda b, i, j: (0, j)),
        pl.BlockSpec((1, tn), lambda b, i, j: (0, j)),
    ]
    if n_f32_cols:
        n_bf16 = (N - n_f32_cols) // tn
        out_specs = [pl.BlockSpec((1, tm, tn), lambda b, i, j: (b, i, jnp.minimum(j, n_bf16 - 1))),
                     pl.BlockSpec((1, tm, tn), lambda b, i, j: (b, i, jnp.maximum(j - n_bf16, 0)))]
        out_shape = [jax.ShapeDtypeStruct((B, S, N - n_f32_cols), BF16),
                     jax.ShapeDtypeStruct((B, S, n_f32_cols), F32)]
    else:
        n_bf16 = None
        out_specs = pl.BlockSpec((1, tm, tn), lambda b, i, j: (b, i, j))
        out_shape = jax.ShapeDtypeStruct((B, S, N), BF16)
    return pl.pallas_call(
        functools.partial(_inproj_kernel, n_bf16=n_bf16),
        grid=(B, S // tm, N // tn),
        in_specs=in_specs,
        out_specs=out_specs,
        out_shape=out_shape,
        scratch_shapes=[pltpu.VMEM((tm, D), BF16)],
        compiler_params=_params("parallel", "parallel", "arbitrary"),
        name="in_proj",
    )(x, mod, mod, w_bf16, colscale)


N_POS_PIECES = 3


def _diff_kernel(slope_ref, q_ref, k_ref, v_ref, cx_ref, lam_ref, g_ref, o_ref,
                 kx1, kx2, qx1, qx2, bdiag, m1_ref, l1_ref, a1_ref, m2_ref, l2_ref, a2_ref, *, t, n_chunks, lam_init):
    h = pl.program_id(1)
    i = pl.program_id(2)
    slope2 = slope_ref[h]

    @pl.when(i == 0)
    def _():
        cx = cx_ref[0]
        kx1[:, :HEAD_DIM] = k_ref[0, :, :HEAD_DIM]
        kx2[:, :HEAD_DIM] = k_ref[0, :, HEAD_DIM:]
        for c in range(n_chunks):
            kx1[c * t:(c + 1) * t, HEAD_DIM:] = cx
            kx2[c * t:(c + 1) * t, HEAD_DIM:] = cx
        d = lax.broadcasted_iota(I32, (t, t), 1) - lax.broadcasted_iota(I32, (t, t), 0)
        bdiag[...] = -slope2 * jnp.abs(d).astype(F32)

    q = q_ref[0]
    ones = jnp.where(lax.broadcasted_iota(I32, (t, HEAD_DIM), 1) < N_POS_PIECES, 1.0, 0.0)
    for side, sign in enumerate((1.0, 0.0, -1.0)):
        e = (sign * ones).astype(BF16)
        qx1[side, :, :HEAD_DIM] = q[:, :HEAD_DIM]
        qx1[side, :, HEAD_DIM:] = e
        qx2[side, :, :HEAD_DIM] = q[:, HEAD_DIM:]
        qx2[side, :, HEAD_DIM:] = e
    row_term = slope2 * (i * t + lax.broadcasted_iota(I32, (t, 1), 0)).astype(F32)

    m1_ref[...] = jnp.full(m1_ref.shape, NEG, F32)
    m2_ref[...] = jnp.full(m2_ref.shape, NEG, F32)
    l1_ref[...] = jnp.zeros(l1_ref.shape, F32)
    l2_ref[...] = jnp.zeros(l2_ref.shape, F32)
    a1_ref[...] = jnp.zeros(a1_ref.shape, F32)
    a2_ref[...] = jnp.zeros(a2_ref.shape, F32)

    def key_tile(c, side, row_sign, on_diagonal):
        k0 = pl.multiple_of(c * t, t)
        vc = v_ref[0, pl.ds(k0, t), :]
        shift = row_sign * (row_term - slope2 * k0.astype(F32))

        def softmax_update(s, m_ref, l_ref):
            if on_diagonal:
                s = s + bdiag[...]
            m_old = m_ref[...]
            m_new = jnp.maximum(m_old, jnp.max(s, axis=-1, keepdims=True) + shift)
            p = jnp.exp2(s - (m_new - shift))
            alpha = jnp.exp2(m_old - m_new)
            l_ref[...] = alpha * l_ref[...] + jnp.sum(p, axis=-1, keepdims=True)
            m_ref[...] = m_new
            return p.astype(BF16), alpha

        s1 = _nt_dot(qx1[side], kx1[pl.ds(k0, t), :])
        s2 = _nt_dot(qx2[side], kx2[pl.ds(k0, t), :])
        p1, alpha1 = softmax_update(s1, m1_ref, l1_ref)
        p2, alpha2 = softmax_update(s2, m2_ref, l2_ref)
        a1_ref[...] = alpha1 * a1_ref[...] + jnp.dot(p1, vc, preferred_element_type=F32)
        a2_ref[...] = alpha2 * a2_ref[...] + jnp.dot(p2, vc, preferred_element_type=F32)

    def before(c, carry):
        key_tile(c, 0, -1.0, False)
        return carry

    def after(c, carry):
        key_tile(c, 2, 1.0, False)
        return carry

    lax.fori_loop(0, i, before, 0)
    key_tile(i, 1, 0.0, True)
    lax.fori_loop(i + 1, n_chunks, after, 0)

    lv = lam_ref[...]
    s01 = jnp.sum(lv[0:1, :] * lv[1:2, :], axis=-1, keepdims=True)
    s23 = jnp.sum(lv[2:3, :] * lv[3:4, :], axis=-1, keepdims=True)
    lam = jnp.exp(s01) - jnp.exp(s23) + lam_init
    o = a1_ref[...] / l1_ref[...] - lam * (a2_ref[...] / l2_ref[...])
    ms = jnp.mean(o * o, axis=-1, keepdims=True)
    o = o * lax.rsqrt(ms + LN_EPS) * g_ref[...] * (1.0 - lam_init)
    o_ref[0] = o.astype(BF16)


def _diff_attention(proj, lam_vecs, subln_g, layer_idx):
    B, S, _ = proj.shape
    t = min(S, 1024)
    lam_init = 0.8 - 0.6 * math.exp(-0.3 * layer_idx)
    slopes2 = jnp.asarray(_alibi_slopes(A_HEADS) * LOG2E)
    rest = slopes2[:, None] * jnp.arange(t, dtype=F32)[None, :]
    pieces = []
    for _ in range(N_POS_PIECES):
        piece = rest.astype(BF16)
        pieces.append(piece)
        rest = rest - piece.astype(F32)
    cext = jnp.zeros((A_HEADS, t, HEAD_DIM), BF16).at[:, :, :N_POS_PIECES].set(jnp.stack(pieces, axis=-1))
    nq = A_QK // A_VDIM
    kern = functools.partial(_diff_kernel, t=t, n_chunks=S // t, lam_init=lam_init)
    return pl.pallas_call(
        kern,
        grid=(B, A_HEADS, S // t),
        in_specs=[
            pl.BlockSpec(memory_space=pltpu.SMEM),
            pl.BlockSpec((1, t, A_VDIM), lambda b, h, i: (b, i, h)),
            pl.BlockSpec((1, S, A_VDIM), lambda b, h, i: (b, 0, nq + h)),
            pl.BlockSpec((1, S, A_VDIM), lambda b, h, i: (b, 0, 2 * nq + h)),
            pl.BlockSpec((1, t, HEAD_DIM), lambda b, h, i: (h, 0, 0)),
            pl.BlockSpec((4, HEAD_DIM), lambda b, h, i: (0, 0)),
            pl.BlockSpec((1, A_VDIM), lambda b, h, i: (0, 0)),
        ],
        out_specs=pl.BlockSpec((1, t, A_VDIM), lambda b, h, i: (b, i, h)),
        out_shape=jax.ShapeDtypeStruct((B, S, A_V), BF16),
        scratch_shapes=[
            pltpu.VMEM((S, 2 * HEAD_DIM), BF16), pltpu.VMEM((S, 2 * HEAD_DIM), BF16),
            pltpu.VMEM((3, t, 2 * HEAD_DIM), BF16), pltpu.VMEM((3, t, 2 * HEAD_DIM), BF16),
            pltpu.VMEM((t, t), F32),
            pltpu.VMEM((t, 1), F32), pltpu.VMEM((t, 1), F32), pltpu.VMEM((t, A_VDIM), F32),
            pltpu.VMEM((t, 1), F32), pltpu.VMEM((t, 1), F32), pltpu.VMEM((t, A_VDIM), F32),
        ],
        compiler_params=_params("parallel", "parallel", "arbitrary"),
        name="diff_attn",
    )(slopes2, proj, proj, proj, cext, lam_vecs, subln_g.reshape(1, A_VDIM))


def _banded_kernel(slope_ref, sink_ref, q_ref, kp_ref, kc_ref, kn_ref, vp_ref, vc_ref, vn_ref, o_ref, kwin, vwin, *,
                   tq, radius, n_q, group, n_rows, dist_scale, use_sink):
    t = pl.program_id(2)
    w = tq + 2 * radius
    kwin[0:radius, :] = kp_ref[0]
    kwin[radius:radius + tq, :] = kc_ref[0]
    kwin[radius + tq:w, :] = kn_ref[0]
    vwin[0:radius, :] = vp_ref[0]
    vwin[radius:radius + tq, :] = vc_ref[0]
    vwin[radius + tq:w, :] = vn_ref[0]

    ii = lax.broadcasted_iota(I32, (tq, w), 0)
    jj = lax.broadcasted_iota(I32, (tq, w), 1)
    rel = jnp.abs(jj - radius - ii)
    kpos = t * tq - radius + jj
    valid = (rel <= radius) & (kpos >= 0) & (kpos < n_rows)
    dist = rel.astype(F32) * float(dist_scale)

    def head_cols(h):
        return slice(h * HEAD_DIM, (h + 1) * HEAD_DIM)

    for hk in range(n_q // group):
        heads = range(hk * group, (hk + 1) * group)
        kh = kwin[:, head_cols(hk)]
        vh = vwin[:, head_cols(hk)]
        scores = [jnp.where(valid, _nt_dot(q_ref[0, :, head_cols(h)], kh) - slope_ref[h] * dist, NEG) for h in heads]
        soft = []
        for h, s in zip(heads, scores):
            m = jnp.max(s, axis=-1, keepdims=True)
            if use_sink:
                m = jnp.maximum(m, sink_ref[h])
            p = jnp.exp2(s - m)
            den = jnp.sum(p, axis=-1, keepdims=True)
            if use_sink:
                den = den + jnp.exp2(sink_ref[h] - m)
            soft.append((p.astype(BF16), den))
        for h, (p, den) in zip(heads, soft):
            o_ref[0, :, head_cols(h)] = (jnp.dot(p, vh, preferred_element_type=F32) / den).astype(BF16)


def _banded_attention(src, *, n_rows, n_res, src_cols, q_blk, k_blk, v_blk, n_q, group, radius,
                      dist_scale, slopes2, sink2):
    B = src.shape[0]
    qw = n_q * HEAD_DIM
    kw = (n_q // group) * HEAD_DIM
    tq = min(n_rows, 256)
    nt = n_rows // tq
    per_t = tq // radius
    last_halo = n_rows // radius - 1
    qpg = src_cols // qw
    kpg = src_cols // kw

    def q_map(b, r, t):
        return (b, t, r * qpg + q_blk)

    def cur_map(blk):
        return lambda b, r, t: (b, t, r * kpg + blk)

    def prev_map(blk):
        return lambda b, r, t: (b, jnp.maximum(t * per_t - 1, 0), r * kpg + blk)

    def next_map(blk):
        return lambda b, r, t: (b, jnp.minimum((t + 1) * per_t, last_halo), r * kpg + blk)

    out_shape = jax.ShapeDtypeStruct((B, n_rows, n_res * qw), BF16)
    out_specs = pl.BlockSpec((1, tq, qw), lambda b, r, t: (b, t, r))
    kern = functools.partial(_banded_kernel, tq=tq, radius=radius, n_q=n_q, group=group, n_rows=n_rows,
                             dist_scale=dist_scale, use_sink=sink2 is not None)
    if sink2 is None:
        sink2 = jnp.zeros((n_q,), F32)
    return pl.pallas_call(
        kern,
        grid=(B, n_res, nt),
        in_specs=[
            pl.BlockSpec(memory_space=pltpu.SMEM),
            pl.BlockSpec(memory_space=pltpu.SMEM),
            pl.BlockSpec((1, tq, qw), q_map),
            pl.BlockSpec((1, radius, kw), prev_map(k_blk)),
            pl.BlockSpec((1, tq, kw), cur_map(k_blk)),
            pl.BlockSpec((1, radius, kw), next_map(k_blk)),
            pl.BlockSpec((1, radius, kw), prev_map(v_blk)),
            pl.BlockSpec((1, tq, kw), cur_map(v_blk)),
            pl.BlockSpec((1, radius, kw), next_map(v_blk)),
        ],
        out_specs=out_specs,
        out_shape=out_shape,
        scratch_shapes=[pltpu.VMEM((tq + 2 * radius, kw), BF16), pltpu.VMEM((tq + 2 * radius, kw), BF16)],
        compiler_params=_params("parallel", "parallel", "parallel"),
        name="banded_attn",
    )(slopes2, sink2, src, src, src, src, src, src, src)


def _band_bias(tq, wlen, offset, radius, slope_dist):
    rel = jnp.abs(offset + lax.broadcasted_iota(I32, (tq, wlen), 1) - lax.broadcasted_iota(I32, (tq, wlen), 0))
    return jnp.where(rel <= radius, -slope_dist * rel.astype(F32), NEG)


def _band_softmax(s):
    m = jnp.max(s, axis=-1, keepdims=True)
    p = jnp.exp2(s - m)
    return m, p.astype(BF16), jnp.sum(p, axis=-1, keepdims=True)


TILE_UNROLL = 4


def _loop_tiles(n, scores_fn, finish_fn):
    unroll = TILE_UNROLL if n % TILE_UNROLL == 0 else 1

    def body(i, carry):
        staged = [scores_fn(i * unroll + u) for u in range(unroll)]
        soft = [_band_softmax(s) for _, s in staged]
        for (ctx, _), (m, p, l) in zip(staged, soft):
            finish_fn(ctx, m, p, l)
        return carry

    lax.fori_loop(0, n // unroll, body, 0)


def _band_geometry(seq, tq, radius, dil):
    n_rows = seq // dil
    tile = min(tq, n_rows)
    nt = n_rows // tile
    wlen = min(tile + 2 * radius, n_rows)
    offsets = (0,) if nt == 1 else (0, -radius, tile - wlen)
    return n_rows, tile, nt, wlen, offsets


def _dilated_kernel(slope_ref, q_ref, k_ref, v_ref, o_ref, acc_s, m_s, l_s, bias_s, *, seq, tq, branches):
    slope = slope_ref[pl.program_id(1)]
    lanes = (tq, HEAD_DIM)

    first_bias = []
    n_bias = 0
    for radius, dil in branches:
        _, tile, _, wlen, offsets = _band_geometry(seq, tq, radius, dil)
        first_bias.append(n_bias)
        for off in offsets:
            bias_s[n_bias, 0:tile, 0:wlen] = _band_bias(tile, wlen, off, radius, slope * dil)
            n_bias += 1

    def tile_window(bi, t):
        radius, dil = branches[bi]
        n_rows, tile, nt, wlen, _ = _band_geometry(seq, tq, radius, dil)
        q0 = t * tile
        k0 = jnp.clip(q0 - radius, 0, n_rows - wlen)
        case = 0 if nt == 1 else jnp.where(t == 0, 0, jnp.where(t == nt - 1, 2, 1))
        return q0, k0, bias_s[first_bias[bi] + case, 0:tile, 0:wlen]

    for bi, (radius, dil) in enumerate(branches[:-1]):
        _, tile, nt, wlen, _ = _band_geometry(seq, tq, radius, dil)

        def scores(it, bi=bi, dil=dil, tile=tile, nt=nt, wlen=wlen):
            r = it // nt
            q0, k0, bias = tile_window(bi, it % nt)
            qrows = pl.ds(r + dil * q0, tile, stride=dil)
            krows = pl.ds(r + dil * k0, wlen, stride=dil)
            s = _nt_dot(q_ref[0, qrows, :].astype(BF16), k_ref[0, krows, :].astype(BF16)) + bias
            return (qrows, krows), s

        def finish(ctx, m, p, l, bi=bi, tile=tile):
            qrows, krows = ctx
            acc_s[bi, qrows, :] = jnp.dot(p, v_ref[0, krows, :].astype(BF16), preferred_element_type=F32)
            m_s[bi, qrows, :] = jnp.broadcast_to(m, (tile, HEAD_DIM))
            l_s[bi, qrows, :] = jnp.broadcast_to(l, (tile, HEAD_DIM))

        _loop_tiles(dil * nt, scores, finish)

    assert branches[-1][1] == 1
    wlen = _band_geometry(seq, tq, *branches[-1])[3]

    def dense_scores(t):
        q0, k0, bias = tile_window(len(branches) - 1, t)
        qrows = pl.ds(pl.multiple_of(q0, tq), tq)
        krows = pl.ds(pl.multiple_of(k0, 8), wlen)
        s = _nt_dot(q_ref[0, qrows, :].astype(BF16), k_ref[0, krows, :].astype(BF16)) + bias
        return (qrows, krows), s

    def dense_finish(ctx, m, p, l):
        qrows, krows = ctx
        acc = jnp.dot(p, v_ref[0, krows, :].astype(BF16), preferred_element_type=F32)
        ms = [jnp.broadcast_to(m, lanes)] + [m_s[bi, qrows, :] for bi in range(len(branches) - 1)]
        ls = [jnp.broadcast_to(l, lanes)] + [l_s[bi, qrows, :] for bi in range(len(branches) - 1)]
        accs = [acc] + [acc_s[bi, qrows, :] for bi in range(len(branches) - 1)]
        top = functools.reduce(jnp.maximum, ms)
        es = [jnp.exp2(mi - top) for mi in ms]
        num = sum(e * a for e, a in zip(es, accs))
        den = sum(e * li for e, li in zip(es, ls))
        o_ref[0, qrows, :] = (num / den).astype(BF16)

    _loop_tiles(seq // tq, dense_scores, dense_finish)


def _dilated_mixture(projf, slopes2):
    B, S, _ = projf.shape
    branches = tuple((w // (2 * d), d) for w, d in sorted(B_BRANCHES, key=lambda wd: -wd[1]))
    tq = min(S, 256)
    kern = functools.partial(_dilated_kernel, seq=S, tq=tq, branches=branches)
    nb = len(branches) - 1
    geo = [_band_geometry(S, tq, radius, dil) for radius, dil in branches]
    bias_shape = (sum(len(g[4]) for g in geo), tq, max(g[3] for g in geo))
    col = lambda part: (lambda b, h: (b, 0, part * B_HEADS + h))
    return pl.pallas_call(
        kern,
        grid=(B, B_HEADS),
        in_specs=[pl.BlockSpec(memory_space=pltpu.SMEM)] + [pl.BlockSpec((1, S, HEAD_DIM), col(p)) for p in range(3)],
        out_specs=pl.BlockSpec((1, S, HEAD_DIM), col(0)),
        out_shape=jax.ShapeDtypeStruct((B, S, B_W), BF16),
        scratch_shapes=[pltpu.VMEM((nb, S, HEAD_DIM), F32)] * 3 + [pltpu.VMEM(bias_shape, F32)],
        compiler_params=_params("parallel", "parallel"),
        name="dilated_mix",
    )(slopes2, projf, projf, projf)


def _post_attention(y, x, g1, lng, lnb, sc2, sh2, rwh_ref, rwl_ref, x1_ref, hp_ref, lg_ref, alpha):
    z = alpha * x + g1 * y
    mu = jnp.mean(z, axis=-1, keepdims=True)
    zc = z - mu
    var = jnp.mean(zc * zc, axis=-1, keepdims=True)
    x1 = zc * lax.rsqrt(var + LN_EPS) * lng + lnb
    x1_ref[0] = x1
    h2 = x1 * (1.0 + sc2) + sh2
    hb = h2.astype(BF16)
    hf = hb.astype(F32)
    bits = lax.bitcast_convert_type(hf, U32)
    half = bits.shape[1] // 2
    packed = bits[:, :half] | (bits[:, half:] >> 16)
    nck = half // HEAD_DIM
    for ck in range(nck):
        hp_ref[0, pl.ds(ck, bits.shape[0], stride=nck), :] = packed[:, ck * HEAD_DIM:(ck + 1) * HEAD_DIM]
    lo = (h2 - hf).astype(BF16)
    rwh = rwh_ref[...]
    lg_ref[0] = (jnp.dot(hb, rwh, preferred_element_type=F32)
                 + jnp.dot(hb, rwl_ref[...], preferred_element_type=F32)
                 + jnp.dot(lo, rwh, preferred_element_type=F32))


def _outproj_ab_kernel(ya_ref, yb_ref, w_ref, x_ref, g1_ref, lng_ref, lnb_ref, sh2_ref, sc2_ref, rwh_ref, rwl_ref,
                       x1_ref, hp_ref, lg_ref, *, alpha):
    y = (jnp.dot(ya_ref[0], w_ref[0:A_V, :], preferred_element_type=F32)
         + jnp.dot(yb_ref[0], w_ref[A_V:A_V + B_W, :], preferred_element_type=F32))
    _post_attention(y, x_ref[0], g1_ref[0], lng_ref[...], lnb_ref[...], sc2_ref[0], sh2_ref[0],
                    rwh_ref, rwl_ref, x1_ref, hp_ref, lg_ref, alpha)


def _outproj_c_kernel(o_ref, w_ref, x_ref, g1_ref, lng_ref, lnb_ref, sh2_ref, sc2_ref, rwh_ref, rwl_ref,
                      x1_ref, hp_ref, lg_ref, *, alpha):
    y = jnp.dot(o_ref[0], w_ref[...], preferred_element_type=F32)
    _post_attention(y, x_ref[0], g1_ref[0], lng_ref[...], lnb_ref[...], sc2_ref[0], sh2_ref[0],
                    rwh_ref, rwl_ref, x1_ref, hp_ref, lg_ref, alpha)


def _out_projection(attn_inputs, w_bf16, x, mod, lng, lnb, rw_hi, rw_lo, alpha, mixer):
    B, S, D = x.shape
    E = rw_hi.shape[1]
    tm = min(S, 512)
    row = lambda b, i: (b, i, 0)
    const2 = lambda b, i: (0, 0)
    tail_specs = [
        pl.BlockSpec(w_bf16.shape, const2),
        pl.BlockSpec((1, tm, D), row),
        pl.BlockSpec((1, 1, D), lambda b, i: (b, 0, 2)),
        pl.BlockSpec((1, D), const2),
        pl.BlockSpec((1, D), const2),
        pl.BlockSpec((1, 1, D), lambda b, i: (b, 0, 3)),
        pl.BlockSpec((1, 1, D), lambda b, i: (b, 0, 4)),
        pl.BlockSpec((D, E), const2),
        pl.BlockSpec((D, E), const2),
    ]
    tail_args = (w_bf16, x, mod, lng.reshape(1, D), lnb.reshape(1, D), mod, mod, rw_hi, rw_lo)
    nck = D // 2 // HEAD_DIM
    out_shape = [jax.ShapeDtypeStruct((B, S, D), F32), jax.ShapeDtypeStruct((B, S * nck, HEAD_DIM), U32),
                 jax.ShapeDtypeStruct((B, S, E), F32)]
    out_specs = [pl.BlockSpec((1, tm, D), row), pl.BlockSpec((1, tm * nck, HEAD_DIM), row),
                 pl.BlockSpec((1, tm, E), row)]
    if mixer == "ab":
        ya, yb = attn_inputs
        in_specs = [pl.BlockSpec((1, tm, A_V), row), pl.BlockSpec((1, tm, B_W), row)] + tail_specs
        kern = functools.partial(_outproj_ab_kernel, alpha=alpha)
        scratch = []
        args = (ya, yb) + tail_args
    else:
        (o,) = attn_inputs
        in_specs = [pl.BlockSpec((1, tm, o.shape[2]), row)] + tail_specs
        kern = functools.partial(_outproj_c_kernel, alpha=alpha)
        scratch = []
        args = (o,) + tail_args
    return pl.pallas_call(
        kern,
        grid=(B, S // tm),
        in_specs=in_specs,
        out_specs=out_specs,
        out_shape=out_shape,
        scratch_shapes=scratch,
        compiler_params=_params("parallel", "parallel"),
        name="out_proj_" + mixer,
    )(*args)


def _excl_cumsum_lanes(x, tri):
    n = x.shape[1] // HEAD_DIM
    carry = jnp.zeros((x.shape[0], 1), F32)
    outs = []
    for c in range(n):
        xc = x[:, c * HEAD_DIM:(c + 1) * HEAD_DIM]
        outs.append(jnp.dot(xc.astype(BF16), tri, preferred_element_type=F32) + carry)
        carry = carry + jnp.sum(xc, axis=1, keepdims=True)
    return jnp.concatenate(outs, axis=1)


def _tables_kernel(lg_ref, cnt_ref, off_ref, tok_ref, gate_ref, place_ref, *, n_exp, cap):
    lg = lg_ref[0]
    S = lg.shape[1]
    ex = jnp.exp(lg - jnp.max(lg, axis=0, keepdims=True))
    aff = ex / jnp.sum(ex, axis=0, keepdims=True)
    bits = lax.bitcast_convert_type(aff, I32)

    def search(i, cur):
        cand = cur | jnp.left_shift(jnp.int32(1), 30 - i)
        cnt = jnp.sum(jnp.where(bits >= cand, 1.0, 0.0), axis=1, keepdims=True)
        return jnp.where(cnt >= cap, cand, cur)

    thr = lax.fori_loop(0, 31, search, jnp.zeros((n_exp, 1), I32))
    tri = jnp.where(lax.broadcasted_iota(I32, (HEAD_DIM, HEAD_DIM), 0)
                    < lax.broadcasted_iota(I32, (HEAD_DIM, HEAD_DIM), 1), 1.0, 0.0).astype(BF16)
    gt = bits > thr
    eq = jnp.where(bits == thr, 1.0, 0.0)
    need = cap - jnp.sum(jnp.where(gt, 1.0, 0.0), axis=1, keepdims=True)
    sel = jnp.where(gt, 1.0, jnp.where(_excl_cumsum_lanes(eq, tri) < need, eq, 0.0))
    pos = _excl_cumsum_lanes(sel, tri)

    ranks = []
    run = jnp.zeros((1, S), F32)
    for e in range(n_exp):
        ranks.append(run)
        run = run + sel[e:e + 1, :]
    cnt_ref[0] = run.astype(I32)
    first = _excl_cumsum_lanes(run, tri) + (pl.program_id(0) * (n_exp * cap)).astype(F32)
    off_ref[0] = first.astype(I32)

    def digits(x, base, n):
        out = []
        for i in reversed(range(n)):
            d = jnp.floor(x * (1.0 / base ** i))
            out.append(d)
            x = x - d * float(base ** i)
        return out

    def pieces(x, n):
        out = []
        for _ in range(n):
            p = x.astype(BF16)
            out.append(p.astype(F32))
            x = x - p.astype(F32)
        return out

    jcol = lax.broadcasted_iota(I32, (cap, S), 0).astype(F32)
    token = lax.broadcasted_iota(I32, (1, S), 1).astype(F32)
    tok_rows = digits(token, 64, 2)
    toks, gates, places = [], [], []
    for e in range(n_exp):
        mine = jnp.where(sel[e:e + 1, :] > 0.0, pos[e:e + 1, :], -1.0) == jcol
        vals = tok_rows + pieces(aff[e:e + 1, :], 3) + digits(first + ranks[e], 64, 3)
        got = _nt_dot(jnp.where(mine, 1.0, 0.0).astype(BF16), jnp.concatenate(vals, axis=0).astype(BF16))
        toks.append(got[:, 0:1] * 64.0 + got[:, 1:2])
        gates.append(got[:, 2:3] + got[:, 3:4] + got[:, 4:5])
        places.append(got[:, 5:6] * 4096.0 + got[:, 6:7] * 64.0 + got[:, 7:8])
    tok_ref[0] = jnp.concatenate(toks, axis=1).astype(I32) + pl.program_id(0) * S
    gate_ref[0] = jnp.concatenate(gates, axis=1)
    place_ref[0] = jnp.concatenate(places, axis=1).astype(I32)


def _routing_tables(logits_t, cap):
    B, E, S = logits_t.shape
    kern = functools.partial(_tables_kernel, n_exp=E, cap=cap)
    per_token = pl.BlockSpec((1, 1, S), lambda b: (b, 0, 0))
    per_slot = pl.BlockSpec((1, cap, E), lambda b: (b, 0, 0))
    cnt, off, tok, gate, place = pl.pallas_call(
        kern,
        grid=(B,),
        in_specs=[pl.BlockSpec((1, E, S), lambda b: (b, 0, 0))],
        out_specs=[per_token, per_token, per_slot, per_slot, per_slot],
        out_shape=[jax.ShapeDtypeStruct((B, 1, S), I32), jax.ShapeDtypeStruct((B, 1, S), I32),
                   jax.ShapeDtypeStruct((B, cap, E), I32), jax.ShapeDtypeStruct((B, cap, E), F32),
                   jax.ShapeDtypeStruct((B, cap, E), I32)],
        compiler_params=_params("parallel"),
        name="routing_tables",
    )(logits_t)
    expert_major = lambda a: jnp.transpose(a, (2, 0, 1)).reshape(E * B * cap)
    return cnt, off, expert_major(tok), expert_major(gate), expert_major(place)


def _token_rows(t, nck):
    start = t * nck
    return pl.ds(pl.multiple_of(start, nck) if nck > 1 else start, nck)


def _unpack_rows(xp_ref, first, xs_ref):
    rows, d = xs_ref.shape
    half = d // 2
    nck = half // HEAD_DIM
    for ck in range(nck):
        u = xp_ref[pl.ds(first * nck + ck, rows, stride=nck), :]
        lo = ck * HEAD_DIM
        xs_ref[:, lo:lo + HEAD_DIM] = lax.bitcast_convert_type(u & jnp.uint32(0xFFFF0000), F32).astype(BF16)
        xs_ref[:, half + lo:half + lo + HEAD_DIM] = lax.bitcast_convert_type(u << 16, F32).astype(BF16)


def _pack_pairs(hi, lo):
    hb = lax.bitcast_convert_type(hi.astype(BF16).astype(F32), U32)
    lb = lax.bitcast_convert_type(lo.astype(BF16).astype(F32), U32)
    return hb | (lb >> 16)


ROW_UNROLL = 8


def _ffn_up_kernel(tok_ref, tokn_ref, hp_ref, wg_ref, wu_ref, o_ref, xraw, xs_ref, sem, *, tm, mt, n_tiles, nck,
                   n_steps):
    tile = pl.program_id(0) * mt + pl.program_id(1)
    step = pl.program_id(2)
    par = tile % 2

    def request(idx_ref, slot, r):
        pltpu.make_async_copy(hp_ref.at[_token_rows(idx_ref[r], nck)],
                              xraw.at[_token_rows(slot * tm + r, nck)], sem.at[slot]).start()

    def arrived(slot):
        return pltpu.make_async_copy(hp_ref.at[pl.ds(0, tm * nck)], xraw.at[pl.ds(slot * tm * nck, tm * nck)],
                                     sem.at[slot])

    @pl.when(step == 0)
    def _():
        @pl.when(tile == 0)
        def _():
            def body(i, carry):
                for u in range(ROW_UNROLL):
                    request(tok_ref, 0, i * ROW_UNROLL + u)
                return carry

            lax.fori_loop(0, tm // ROW_UNROLL, body, 0)

        arrived(par).wait()
        _unpack_rows(xraw, par * tm, xs_ref)

    per_step = tm // n_steps
    for u in range(per_step):
        request(tokn_ref, 1 - par, step * per_step + u)

    x = xs_ref[...]
    g = jnp.dot(x, wg_ref[0, 0].astype(BF16), preferred_element_type=F32)
    u = jnp.dot(x, wu_ref[0, 0].astype(BF16), preferred_element_type=F32)
    o_ref[...] = (g * jax.nn.sigmoid(g) * u).astype(BF16)

    @pl.when((tile == n_tiles - 1) & (step == n_steps - 1))
    def _():
        arrived(1 - par).wait()


def _ffn_up(hp, tok_rows, w_gate, w_up, layer, rows_per_expert):
    _, E, D, Fh = w_gate.shape
    tm = min(rows_per_expert, 2048)
    mt = rows_per_expert // tm
    n_tiles = E * mt
    tn = min(Fh, 256)
    nck = D // 2 // HEAD_DIM
    kern = functools.partial(_ffn_up_kernel, tm=tm, mt=mt, n_tiles=n_tiles, nck=nck, n_steps=Fh // tn)
    smem = functools.partial(pl.BlockSpec, memory_space=pltpu.SMEM)
    return pl.pallas_call(
        kern,
        grid=(E, mt, Fh // tn),
        in_specs=[smem((tm,), lambda e, m, n: (e * mt + m,)),
                  smem((tm,), lambda e, m, n: (jnp.minimum(e * mt + m + 1, n_tiles - 1),)),
                  pl.BlockSpec(memory_space=pl.ANY),
                  pl.BlockSpec((1, 1, D, tn), lambda e, m, n: (layer, e, 0, n)),
                  pl.BlockSpec((1, 1, D, tn), lambda e, m, n: (layer, e, 0, n))],
        out_specs=pl.BlockSpec((tm, tn), lambda e, m, n: (e * mt + m, n)),
        out_shape=jax.ShapeDtypeStruct((E * rows_per_expert, Fh), BF16),
        scratch_shapes=[pltpu.VMEM((2 * tm * nck, HEAD_DIM), U32), pltpu.VMEM((tm, D), BF16),
                        pltpu.SemaphoreType.DMA((2,))],
        compiler_params=_params("arbitrary", "arbitrary", "arbitrary"),
        name="ffn_up",
    )(tok_rows, tok_rows, hp, w_gate, w_up)


def _ffn_down_kernel(pos_ref, h_ref, wa_ref, wb_ref, g_ref, yt_ref, ybuf, sem, *, n_steps, cps, tm, mt, n_tiles,
                     nck):
    tile = pl.program_id(0) * mt + pl.program_id(1)
    par = tile % 2
    h = h_ref[...]
    g = g_ref[...]
    ya = jnp.dot(h, wa_ref[0, 0].astype(BF16), preferred_element_type=F32) * g
    yb = jnp.dot(h, wb_ref[0, 0].astype(BF16), preferred_element_type=F32) * g
    packed = _pack_pairs(ya, yb)
    n = pl.program_id(2)
    for k in range(n_steps):
        @pl.when(n == k)
        def _():
            for cc in range(cps):
                ybuf[pl.ds(par * tm * nck + k * cps + cc, tm, stride=nck), :] = (
                    packed[:, cc * HEAD_DIM:(cc + 1) * HEAD_DIM])

    def scatter_done(slot):
        return pltpu.make_async_copy(ybuf.at[pl.ds(slot * tm * nck, tm * nck)], yt_ref.at[pl.ds(0, tm * nck)],
                                     sem.at[slot])

    @pl.when(n == n_steps - 1)
    def _():
        @pl.when(tile > 0)
        def _():
            scatter_done(1 - par).wait()

        def body(i, carry):
            for u in range(ROW_UNROLL):
                r = i * ROW_UNROLL + u
                pltpu.make_async_copy(ybuf.at[_token_rows(par * tm + r, nck)],
                                      yt_ref.at[_token_rows(pos_ref[r], nck)], sem.at[par]).start()
            return carry

        lax.fori_loop(0, tm // ROW_UNROLL, body, 0)

        @pl.when(tile == n_tiles - 1)
        def _():
            scatter_done(par).wait()


def _ffn_down(hid, w_down, layer, gate_col, pos_rows, rows_per_expert):
    _, E, Fh, D = w_down.shape
    tm = min(rows_per_expert, 2048)
    mt = rows_per_expert // tm
    n_tiles = E * mt
    half = D // 2
    tnh = min(half, 256)
    n_steps = half // tnh
    nck = half // HEAD_DIM
    kern = functools.partial(_ffn_down_kernel, n_steps=n_steps, cps=tnh // HEAD_DIM, tm=tm, mt=mt, n_tiles=n_tiles,
                             nck=nck)
    return pl.pallas_call(
        kern,
        grid=(E, mt, n_steps),
        in_specs=[pl.BlockSpec((tm,), lambda e, m, n: (e * mt + m,), memory_space=pltpu.SMEM),
                  pl.BlockSpec((tm, Fh), lambda e, m, n: (e * mt + m, 0)),
                  pl.BlockSpec((1, 1, Fh, tnh), lambda e, m, n: (layer, e, 0, n)),
                  pl.BlockSpec((1, 1, Fh, tnh), lambda e, m, n: (layer, e, 0, n + n_steps)),
                  pl.BlockSpec((tm, 1), lambda e, m, n: (e * mt + m, 0))],
        out_specs=pl.BlockSpec(memory_space=pl.ANY),
        out_shape=jax.ShapeDtypeStruct((E * rows_per_expert * nck, HEAD_DIM), U32),
        scratch_shapes=[pltpu.VMEM((2 * tm * nck, HEAD_DIM), U32), pltpu.SemaphoreType.DMA((2,))],
        compiler_params=_params("arbitrary", "arbitrary", "arbitrary"),
        name="ffn_down",
    )(pos_rows, hid, w_down, w_down, gate_col)


def _combine_kernel(offt_ref, x_ref, off_ref, cnt_ref, g2_ref, lng_ref, lnb_ref, yt_ref, o_ref,
                    buf, rows, acc, sem, *, tm, ch, n_total, n_tiles, n_tiles_s, alpha, nck):
    tile = pl.program_id(0) * n_tiles_s + pl.program_id(1)
    o0 = offt_ref[tile]
    o1 = offt_ref[tile + 1]
    n_chunks = jnp.maximum((o1 - o0 + ch - 1) // ch, 1)
    lo_col = off_ref[0]
    hi_col = lo_col + cnt_ref[0]
    lane = lax.broadcasted_iota(I32, (tm, ch), 1)
    acc[...] = jnp.zeros(acc.shape, F32)

    def fetch(want, slot):
        start = jnp.minimum(want, n_total - ch) * nck
        if nck > 1:
            start = pl.multiple_of(start, nck)
        return pltpu.make_async_copy(yt_ref.at[pl.ds(start, ch * nck)], buf.at[pl.ds(slot * ch * nck, ch * nck)],
                                     sem.at[slot])

    @pl.when(tile == 0)
    def _():
        fetch(o0, 0).start()

    def chunk(c, carry):
        slot = c % 2
        want = o0 + c * ch
        fetch(want, slot).wait()

        @pl.when(c + 1 < n_chunks)
        def _():
            fetch(want + ch, 1 - slot).start()

        _unpack_rows(buf, slot * ch, rows)
        row = lane + jnp.minimum(want, n_total - ch)
        own = (row >= lo_col) & (row < hi_col) & (row >= want)
        acc[...] += jnp.dot(jnp.where(own, 1.0, 0.0).astype(BF16), rows[...], preferred_element_type=F32)
        return carry

    lax.fori_loop(0, n_chunks, chunk, 0)

    @pl.when(tile + 1 < n_tiles)
    def _():
        fetch(o1, 0).start()

    z = alpha * x_ref[0] + g2_ref[0] * acc[...]
    mu = jnp.mean(z, axis=-1, keepdims=True)
    zc = z - mu
    var = jnp.mean(zc * zc, axis=-1, keepdims=True)
    o_ref[0] = zc * lax.rsqrt(var + LN_EPS) * lng_ref[...] + lnb_ref[...]


def _combine(x1, mod, lng, lnb, yt, off, cnt, alpha):
    B, S, D = x1.shape
    nck = D // 2 // HEAD_DIM
    n_total = yt.shape[0] // nck
    tm = min(S, 256)
    ch = min(n_total, 512)
    n_tiles_s = S // tm
    off_flat = off.reshape(B * S)
    offt = jnp.concatenate([off_flat[::tm], jnp.full((1,), n_total, I32)])
    kern = functools.partial(_combine_kernel, tm=tm, ch=ch, n_total=n_total, n_tiles=B * n_tiles_s,
                             n_tiles_s=n_tiles_s, alpha=alpha, nck=nck)
    grid_spec = pltpu.PrefetchScalarGridSpec(
        num_scalar_prefetch=1,
        grid=(B, n_tiles_s),
        in_specs=[
            pl.BlockSpec((1, tm, D), lambda b, i, o: (b, i, 0)),
            pl.BlockSpec((1, tm, 1), lambda b, i, o: (b, i, 0)),
            pl.BlockSpec((1, tm, 1), lambda b, i, o: (b, i, 0)),
            pl.BlockSpec((1, 1, D), lambda b, i, o: (b, 0, 5)),
            pl.BlockSpec((1, D), lambda b, i, o: (0, 0)),
            pl.BlockSpec((1, D), lambda b, i, o: (0, 0)),
            pl.BlockSpec(memory_space=pl.ANY),
        ],
        out_specs=pl.BlockSpec((1, tm, D), lambda b, i, o: (b, i, 0)),
        scratch_shapes=[pltpu.VMEM((2 * ch * nck, HEAD_DIM), U32), pltpu.VMEM((ch, D), BF16),
                        pltpu.VMEM((tm, D), F32), pltpu.SemaphoreType.DMA((2,))],
    )
    return pl.pallas_call(
        kern,
        grid_spec=grid_spec,
        out_shape=jax.ShapeDtypeStruct((B, S, D), F32),
        compiler_params=_params("arbitrary", "arbitrary"),
        name="moe_combine",
    )(offt, x1, off.reshape(B, S, 1), cnt.reshape(B, S, 1), mod, lng.reshape(1, D), lnb.reshape(1, D), yt)


def _moe_sublayer(x1, hp, logits, mod, lng, lnb, w_gate, w_up, w_down, layer, alpha):
    B, S, D = x1.shape
    E = logits.shape[-1]
    cap = EC_CAPACITY_FACTOR * S // E
    cnt, off, tok, grow, pos = _routing_tables(jnp.swapaxes(logits, 1, 2), cap)
    hid = _ffn_up(hp.reshape(-1, HEAD_DIM), tok, w_gate, w_up, layer, B * cap)
    yt = _ffn_down(hid, w_down, layer, grow.reshape(E * B * cap, 1), pos, B * cap)
    return _combine(x1, mod, lng, lnb, yt, off, cnt, alpha)


def _split_bf16(w):
    hi = w.astype(BF16)
    return hi, (w - hi.astype(F32)).astype(BF16)


def kernel(x, c, ada_w, ada_b, ln_g, ln_b, ab_w_in, ab_w_out, diff_lambda, diff_subln_g, c_w_in, c_w_out,
           c_sink, router_w, w_gate, w_up, w_down):
    B, S, D = x.shape
    depth = ada_w.shape[0]
    alpha = (2.0 * depth) ** 0.25
    qscale = HEAD_DIM ** -0.5 * LOG2E
    mod_all = _modulation(c, ada_w, ada_b)

    ab_scale = np.ones((1, AB_IN), np.float32)
    ab_scale[:, :A_QK] = qscale
    ab_scale[:, 2 * A_QK + A_V:2 * A_QK + A_V + B_W] = qscale
    c_scale = np.ones((1, C_IN), np.float32)
    c_scale[:, :C_QW] = qscale

    for l in range(depth):
        mod = mod_all[l][:, None, :]
        i = l // 2
        rw_hi, rw_lo = _split_bf16(router_w[l])
        if l % 2 == 0:
            proj, projf = _in_projection(x, mod, ab_w_in[i].astype(BF16), jnp.asarray(ab_scale), n_f32_cols=3 * B_W)
            ya = _diff_attention(proj, diff_lambda[i], diff_subln_g[i], l)
            yb = _dilated_mixture(projf, jnp.asarray(_alibi_slopes(B_HEADS) * LOG2E))
            x1, hp, logits = _out_projection((ya, yb), ab_w_out[i].astype(BF16), x, mod,
                                             ln_g[l, 0], ln_b[l, 0], rw_hi, rw_lo, alpha, "ab")
        else:
            proj = _in_projection(x, mod, c_w_in[i].astype(BF16), jnp.asarray(c_scale))
            o = _banded_attention(
                proj, n_rows=S, n_res=1, src_cols=C_IN, q_blk=0, k_blk=C_QW // C_KVW, v_blk=C_QW // C_KVW + 1,
                n_q=C_Q_HEADS, group=C_Q_HEADS // C_KV_HEADS, radius=C_RADIUS, dist_scale=1,
                slopes2=jnp.asarray(_alibi_slopes(C_Q_HEADS) * LOG2E), sink2=c_sink[i] * LOG2E)
            x1, hp, logits = _out_projection((o,), c_w_out[i].astype(BF16), x, mod, ln_g[l, 0], ln_b[l, 0],
                                             rw_hi, rw_lo, alpha, "c")
        x = _moe_sublayer(x1, hp, logits, mod, ln_g[l, 1], ln_b[l, 1], w_gate, w_up, w_down, l, alpha)
    return x
```

```python
import functools
import math

import numpy as np
import jax
import jax.numpy as jnp
from jax import lax
from jax.experimental import pallas as pl
from jax.experimental.pallas import tpu as pltpu

F32 = jnp.float32
BF16 = jnp.bfloat16
I32 = jnp.int32
U32 = jnp.uint32

HEAD_DIM = 128
A_HEADS = 4
A_VDIM = 2 * HEAD_DIM
B_HEADS = 8
B_BRANCHES = ((128, 1), (512, 4), (2048, 16))
C_Q_HEADS = 16
C_KV_HEADS = 4
C_RADIUS = 128
EC_CAPACITY_FACTOR = 2
LN_EPS = 1e-5
NEG = -1e30
LOG2E = 1.4426950408889634

A_QK = A_HEADS * 2 * HEAD_DIM
A_V = A_HEADS * A_VDIM
B_W = B_HEADS * HEAD_DIM
AB_IN = 2 * A_QK + A_V + 3 * B_W
C_QW = C_Q_HEADS * HEAD_DIM
C_KVW = C_KV_HEADS * HEAD_DIM
C_IN = C_QW + 2 * C_KVW

VMEM_LIMIT_BYTES = 56 * 1024 * 1024


def _params(*sem):
    return pltpu.CompilerParams(dimension_semantics=sem, vmem_limit_bytes=VMEM_LIMIT_BYTES)


def _tile(n, preferred):
    t = min(n, preferred)
    while n % t:
        t //= 2
    return t


def _alibi_slopes(n):
    return np.array([2.0 ** (-8.0 * (i + 1) / n) for i in range(n)], dtype=np.float32)


def _nt_dot(a, b):
    return lax.dot_general(a, b, (((1,), (1,)), ((), ())), preferred_element_type=F32)


def _mod_kernel(c_ref, w_ref, b_ref, o_ref):
    c = c_ref[...]
    cs = (c * jax.nn.sigmoid(c)).astype(BF16)
    o_ref[0] = jnp.dot(cs, w_ref[0].astype(BF16), preferred_element_type=F32) + b_ref[0]


def _modulation(c, ada_w, ada_b):
    L, D, N = ada_w.shape
    B = c.shape[0]
    tn = _tile(N, 1024)
    return pl.pallas_call(
        _mod_kernel,
        grid=(L, N // tn),
        in_specs=[
            pl.BlockSpec((B, D), lambda l, j: (0, 0)),
            pl.BlockSpec((1, D, tn), lambda l, j: (l, 0, j)),
            pl.BlockSpec((1, 1, tn), lambda l, j: (l, 0, j)),
        ],
        out_specs=pl.BlockSpec((1, B, tn), lambda l, j: (l, 0, j)),
        out_shape=jax.ShapeDtypeStruct((L, B, N), F32),
        compiler_params=_params("parallel", "parallel"),
        name="adaln_mod",
    )(c, ada_w, ada_b.reshape(L, 1, N))


def _inproj_kernel(x_ref, sh_ref, sc_ref, w_ref, cs_ref, o_ref, *rest, n_bf16):
    h_ref = rest[-1]
    j = pl.program_id(2)

    @pl.when(j == 0)
    def _():
        h_ref[...] = (x_ref[0] * (1.0 + sc_ref[0]) + sh_ref[0]).astype(BF16)

    acc = jnp.dot(h_ref[...], w_ref[...], preferred_element_type=F32) * cs_ref[...]
    if n_bf16 is None:
        o_ref[0] = acc.astype(BF16)
    else:
        of_ref = rest[0]

        @pl.when(j < n_bf16)
        def _():
            o_ref[0] = acc.astype(BF16)

        @pl.when(j >= n_bf16)
        def _():
            of_ref[0] = acc


def _in_projection(x, mod, w_bf16, colscale, n_f32_cols=0):
    B, S, D = x.shape
    N = w_bf16.shape[1]
    tm = min(S, 1024)
    tn = _tile(N, 1024)
    in_specs = [
        pl.BlockSpec((1, tm, D), lambda b, i, j: (b, i, 0)),
        pl.BlockSpec((1, 1, D), lambda b, i, j: (b, 0, 0)),
        pl.BlockSpec((1, 1, D), lambda b, i, j: (b, 0, 1)),
        pl.BlockSpec((D, tn), lambda b, i, j: (0, j)),
        pl.BlockSpec((1, tn), lambda b, i, j: (0, j)),
    ]
    if n_f32_cols:
        n_bf16 = (N - n_f32_cols) // tn
        out_specs = [pl.BlockSpec((1, tm, tn), lambda b, i, j: (b, i, jnp.minimum(j, n_bf16 - 1))),
                     pl.BlockSpec((1, tm, tn), lambda b, i, j: (b, i, jnp.maximum(j - n_bf16, 0)))]
        out_shape = [jax.ShapeDtypeStruct((B, S, N - n_f32_cols), BF16),
                     jax.ShapeDtypeStruct((B, S, n_f32_cols), F32)]
    else:
        n_bf16 = None
        out_specs = pl.BlockSpec((1, tm, tn), lambda b, i, j: (b, i, j))
        out_shape = jax.ShapeDtypeStruct((B, S, N), BF16)
    return pl.pallas_call(
        functools.partial(_inproj_kernel, n_bf16=n_bf16),
        grid=(B, S // tm, N // tn),
        in_specs=in_specs,
        out_specs=out_specs,
        out_shape=out_shape,
        scratch_shapes=[pltpu.VMEM((tm, D), BF16)],
        compiler_params=_params("parallel", "parallel", "arbitrary"),
        name="in_proj",
    )(x, mod, mod, w_bf16, colscale)


N_POS_PIECES = 3


def _diff_kernel(slope_ref, q_ref, k_ref, v_ref, cx_ref, lam_ref, g_ref, o_ref,
                 kx1, kx2, qx1, qx2, bdiag, m1_ref, l1_ref, a1_ref, m2_ref, l2_ref, a2_ref, *, t, n_chunks, lam_init):
    h = pl.program_id(1)
    i = pl.program_id(2)
    slope2 = slope_ref[h]

    @pl.when(i == 0)
    def _():
        cx = cx_ref[0]
        kx1[:, :HEAD_DIM] = k_ref[0, :, :HEAD_DIM]
        kx2[:, :HEAD_DIM] = k_ref[0, :, HEAD_DIM:]
        for c in range(n_chunks):
            kx1[c * t:(c + 1) * t, HEAD_DIM:] = cx
            kx2[c * t:(c + 1) * t, HEAD_DIM:] = cx
        d = lax.broadcasted_iota(I32, (t, t), 1) - lax.broadcasted_iota(I32, (t, t), 0)
        bdiag[...] = -slope2 * jnp.abs(d).astype(F32)

    q = q_ref[0]
    ones = jnp.where(lax.broadcasted_iota(I32, (t, HEAD_DIM), 1) < N_POS_PIECES, 1.0, 0.0)
    for side, sign in enumerate((1.0, 0.0, -1.0)):
        e = (sign * ones).astype(BF16)
        qx1[side, :, :HEAD_DIM] = q[:, :HEAD_DIM]
        qx1[side, :, HEAD_DIM:] = e
        qx2[side, :, :HEAD_DIM] = q[:, HEAD_DIM:]
        qx2[side, :, HEAD_DIM:] = e
    row_term = slope2 * (i * t + lax.broadcasted_iota(I32, (t, 1), 0)).astype(F32)

    m1_ref[...] = jnp.full(m1_ref.shape, NEG, F32)
    m2_ref[...] = jnp.full(m2_ref.shape, NEG, F32)
    l1_ref[...] = jnp.zeros(l1_ref.shape, F32)
    l2_ref[...] = jnp.zeros(l2_ref.shape, F32)
    a1_ref[...] = jnp.zeros(a1_ref.shape, F32)
    a2_ref[...] = jnp.zeros(a2_ref.shape, F32)

    def key_tile(c, side, row_sign, on_diagonal):
        k0 = pl.multiple_of(c * t, t)
        vc = v_ref[0, pl.ds(k0, t), :]
        shift = row_sign * (row_term - slope2 * k0.astype(F32))

        def softmax_update(s, m_ref, l_ref):
            if on_diagonal:
                s = s + bdiag[...]
            m_old = m_ref[...]
            m_new = jnp.maximum(m_old, jnp.max(s, axis=-1, keepdims=True) + shift)
            p = jnp.exp2(s - (m_new - shift))
            alpha = jnp.exp2(m_old - m_new)
            l_ref[...] = alpha * l_ref[...] + jnp.sum(p, axis=-1, keepdims=True)
            m_ref[...] = m_new
            return p.astype(BF16), alpha

        s1 = _nt_dot(qx1[side], kx1[pl.ds(k0, t), :])
        s2 = _nt_dot(qx2[side], kx2[pl.ds(k0, t), :])
        p1, alpha1 = softmax_update(s1, m1_ref, l1_ref)
        p2, alpha2 = softmax_update(s2, m2_ref, l2_ref)
        a1_ref[...] = alpha1 * a1_ref[...] + jnp.dot(p1, vc, preferred_element_type=F32)
        a2_ref[...] = alpha2 * a2_ref[...] + jnp.dot(p2, vc, preferred_element_type=F32)

    def before(c, carry):
        key_tile(c, 0, -1.0, False)
        return carry

    def after(c, carry):
        key_tile(c, 2, 1.0, False)
        return carry

    lax.fori_loop(0, i, before, 0)
    key_tile(i, 1, 0.0, True)
    lax.fori_loop(i + 1, n_chunks, after, 0)

    lv = lam_ref[...]
    s01 = jnp.sum(lv[0:1, :] * lv[1:2, :], axis=-1, keepdims=True)
    s23 = jnp.sum(lv[2:3, :] * lv[3:4, :], axis=-1, keepdims=True)
    lam = jnp.exp(s01) - jnp.exp(s23) + lam_init
    o = a1_ref[...] / l1_ref[...] - lam * (a2_ref[...] / l2_ref[...])
    ms = jnp.mean(o * o, axis=-1, keepdims=True)
    o = o * lax.rsqrt(ms + LN_EPS) * g_ref[...] * (1.0 - lam_init)
    o_ref[0] = o.astype(BF16)


def _diff_attention(proj, lam_vecs, subln_g, layer_idx):
    B, S, _ = proj.shape
    t = min(S, 1024)
    lam_init = 0.8 - 0.6 * math.exp(-0.3 * layer_idx)
    slopes2 = jnp.asarray(_alibi_slopes(A_HEADS) * LOG2E)
    rest = slopes2[:, None] * jnp.arange(t, dtype=F32)[None, :]
    pieces = []
    for _ in range(N_POS_PIECES):
        piece = rest.astype(BF16)
        pieces.append(piece)
        rest = rest - piece.astype(F32)
    cext = jnp.zeros((A_HEADS, t, HEAD_DIM), BF16).at[:, :, :N_POS_PIECES].set(jnp.stack(pieces, axis=-1))
    nq = A_QK // A_VDIM
    kern = functools.partial(_diff_kernel, t=t, n_chunks=S // t, lam_init=lam_init)
    return pl.pallas_call(
        kern,
        grid=(B, A_HEADS, S // t),
        in_specs=[
            pl.BlockSpec(memory_space=pltpu.SMEM),
            pl.BlockSpec((1, t, A_VDIM), lambda b, h, i: (b, i, h)),
            pl.BlockSpec((1, S, A_VDIM), lambda b, h, i: (b, 0, nq + h)),
            pl.BlockSpec((1, S, A_VDIM), lambda b, h, i: (b, 0, 2 * nq + h)),
            pl.BlockSpec((1, t, HEAD_DIM), lambda b, h, i: (h, 0, 0)),
            pl.BlockSpec((4, HEAD_DIM), lambda b, h, i: (0, 0)),
            pl.BlockSpec((1, A_VDIM), lambda b, h, i: (0, 0)),
        ],
        out_specs=pl.BlockSpec((1, t, A_VDIM), lambda b, h, i: (b, i, h)),
        out_shape=jax.ShapeDtypeStruct((B, S, A_V), BF16),
        scratch_shapes=[
            pltpu.VMEM((S, 2 * HEAD_DIM), BF16), pltpu.VMEM((S, 2 * HEAD_DIM), BF16),
            pltpu.VMEM((3, t, 2 * HEAD_DIM), BF16), pltpu.VMEM((3, t, 2 * HEAD_DIM), BF16),
            pltpu.VMEM((t, t), F32),
            pltpu.VMEM((t, 1), F32), pltpu.VMEM((t, 1), F32), pltpu.VMEM((t, A_VDIM), F32),
            pltpu.VMEM((t, 1), F32), pltpu.VMEM((t, 1), F32), pltpu.VMEM((t, A_VDIM), F32),
        ],
        compiler_params=_params("parallel", "parallel", "arbitrary"),
        name="diff_attn",
    )(slopes2, proj, proj, proj, cext, lam_vecs, subln_g.reshape(1, A_VDIM))


def _banded_kernel(slope_ref, sink_ref, q_ref, kp_ref, kc_ref, kn_ref, vp_ref, vc_ref, vn_ref, o_ref, kwin, vwin, *,
                   tq, radius, n_q, group, n_rows, dist_scale, use_sink):
    t = pl.program_id(2)
    w = tq + 2 * radius
    kwin[0:radius, :] = kp_ref[0]
    kwin[radius:radius + tq, :] = kc_ref[0]
    kwin[radius + tq:w, :] = kn_ref[0]
    vwin[0:radius, :] = vp_ref[0]
    vwin[radius:radius + tq, :] = vc_ref[0]
    vwin[radius + tq:w, :] = vn_ref[0]

    ii = lax.broadcasted_iota(I32, (tq, w), 0)
    jj = lax.broadcasted_iota(I32, (tq, w), 1)
    rel = jnp.abs(jj - radius - ii)
    kpos = t * tq - radius + jj
    valid = (rel <= radius) & (kpos >= 0) & (kpos < n_rows)
    dist = rel.astype(F32) * float(dist_scale)

    def head_cols(h):
        return slice(h * HEAD_DIM, (h + 1) * HEAD_DIM)

    for hk in range(n_q // group):
        heads = range(hk * group, (hk + 1) * group)
        kh = kwin[:, head_cols(hk)]
        vh = vwin[:, head_cols(hk)]
        scores = [jnp.where(valid, _nt_dot(q_ref[0, :, head_cols(h)], kh) - slope_ref[h] * dist, NEG) for h in heads]
        soft = []
        for h, s in zip(heads, scores):
            m = jnp.max(s, axis=-1, keepdims=True)
            if use_sink:
                m = jnp.maximum(m, sink_ref[h])
            p = jnp.exp2(s - m)
            den = jnp.sum(p, axis=-1, keepdims=True)
            if use_sink:
                den = den + jnp.exp2(sink_ref[h] - m)
            soft.append((p.astype(BF16), den))
        for h, (p, den) in zip(heads, soft):
            o_ref[0, :, head_cols(h)] = (jnp.dot(p, vh, preferred_element_type=F32) / den).astype(BF16)


def _banded_attention(src, *, n_rows, n_res, src_cols, q_blk, k_blk, v_blk, n_q, group, radius,
                      dist_scale, slopes2, sink2):
    B = src.shape[0]
    qw = n_q * HEAD_DIM
    kw = (n_q // group) * HEAD_DIM
    tq = min(n_rows, 256)
    nt = n_rows // tq
    per_t = tq // radius
    last_halo = n_rows // radius - 1
    qpg = src_cols // qw
    kpg = src_cols // kw

    def q_map(b, r, t):
        return (b, t, r * qpg + q_blk)

    def cur_map(blk):
        return lambda b, r, t: (b, t, r * kpg + blk)

    def prev_map(blk):
        return lambda b, r, t: (b, jnp.maximum(t * per_t - 1, 0), r * kpg + blk)

    def next_map(blk):
        return lambda b, r, t: (b, jnp.minimum((t + 1) * per_t, last_halo), r * kpg + blk)

    out_shape = jax.ShapeDtypeStruct((B, n_rows, n_res * qw), BF16)
    out_specs = pl.BlockSpec((1, tq, qw), lambda b, r, t: (b, t, r))
    kern = functools.partial(_banded_kernel, tq=tq, radius=radius, n_q=n_q, group=group, n_rows=n_rows,
                             dist_scale=dist_scale, use_sink=sink2 is not None)
    if sink2 is None:
        sink2 = jnp.zeros((n_q,), F32)
    return pl.pallas_call(
        kern,
        grid=(B, n_res, nt),
        in_specs=[
            pl.BlockSpec(memory_space=pltpu.SMEM),
            pl.BlockSpec(memory_space=pltpu.SMEM),
            pl.BlockSpec((1, tq, qw), q_map),
            pl.BlockSpec((1, radius, kw), prev_map(k_blk)),
            pl.BlockSpec((1, tq, kw), cur_map(k_blk)),
            pl.BlockSpec((1, radius, kw), next_map(k_blk)),
            pl.BlockSpec((1, radius, kw), prev_map(v_blk)),
            pl.BlockSpec((1, tq, kw), cur_map(v_blk)),
            pl.BlockSpec((1, radius, kw), next_map(v_blk)),
        ],
        out_specs=out_specs,
        out_shape=out_shape,
        scratch_shapes=[pltpu.VMEM((tq + 2 * radius, kw), BF16), pltpu.VMEM((tq + 2 * radius, kw), BF16)],
        compiler_params=_params("parallel", "parallel", "parallel"),
        name="banded_attn",
    )(slopes2, sink2, src, src, src, src, src, src, src)


def _band_bias(tq, wlen, offset, radius, slope_dist):
    rel = jnp.abs(offset + lax.broadcasted_iota(I32, (tq, wlen), 1) - lax.broadcasted_iota(I32, (tq, wlen), 0))
    return jnp.where(rel <= radius, -slope_dist * rel.astype(F32), NEG)


def _band_softmax(s):
    m = jnp.max(s, axis=-1, keepdims=True)
    p = jnp.exp2(s - m)
    return m, p.astype(BF16), jnp.sum(p, axis=-1, keepdims=True)


TILE_UNROLL = 4


def _loop_tiles(n, scores_fn, finish_fn):
    unroll = TILE_UNROLL if n % TILE_UNROLL == 0 else 1

    def body(i, carry):
        staged = [scores_fn(i * unroll + u) for u in range(unroll)]
        soft = [_band_softmax(s) for _, s in staged]
        for (ctx, _), (m, p, l) in zip(staged, soft):
            finish_fn(ctx, m, p, l)
        return carry

    lax.fori_loop(0, n // unroll, body, 0)


def _band_geometry(seq, tq, radius, dil):
    n_rows = seq // dil
    tile = min(tq, n_rows)
    nt = n_rows // tile
    wlen = min(tile + 2 * radius, n_rows)
    offsets = (0,) if nt == 1 else (0, -radius, tile - wlen)
    return n_rows, tile, nt, wlen, offsets


def _dilated_kernel(slope_ref, q_ref, k_ref, v_ref, o_ref, acc_s, m_s, l_s, bias_s, *, seq, tq, branches):
    slope = slope_ref[pl.program_id(1)]
    lanes = (tq, HEAD_DIM)

    first_bias = []
    n_bias = 0
    for radius, dil in branches:
        _, tile, _, wlen, offsets = _band_geometry(seq, tq, radius, dil)
        first_bias.append(n_bias)
        for off in offsets:
            bias_s[n_bias, 0:tile, 0:wlen] = _band_bias(tile, wlen, off, radius, slope * dil)
            n_bias += 1

    def tile_window(bi, t):
        radius, dil = branches[bi]
        n_rows, tile, nt, wlen, _ = _band_geometry(seq, tq, radius, dil)
        q0 = t * tile
        k0 = jnp.clip(q0 - radius, 0, n_rows - wlen)
        case = 0 if nt == 1 else jnp.where(t == 0, 0, jnp.where(t == nt - 1, 2, 1))
        return q0, k0, bias_s[first_bias[bi] + case, 0:tile, 0:wlen]

    for bi, (radius, dil) in enumerate(branches[:-1]):
        _, tile, nt, wlen, _ = _band_geometry(seq, tq, radius, dil)

        def scores(it, bi=bi, dil=dil, tile=tile, nt=nt, wlen=wlen):
            r = it // nt
            q0, k0, bias = tile_window(bi, it % nt)
            qrows = pl.ds(r + dil * q0, tile, stride=dil)
            krows = pl.ds(r + dil * k0, wlen, stride=dil)
            s = _nt_dot(q_ref[0, qrows, :].astype(BF16), k_ref[0, krows, :].astype(BF16)) + bias
            return (qrows, krows), s

        def finish(ctx, m, p, l, bi=bi, tile=tile):
            qrows, krows = ctx
            acc_s[bi, qrows, :] = jnp.dot(p, v_ref[0, krows, :].astype(BF16), preferred_element_type=F32)
            m_s[bi, qrows, :] = jnp.broadcast_to(m, (tile, HEAD_DIM))
            l_s[bi, qrows, :] = jnp.broadcast_to(l, (tile, HEAD_DIM))

        _loop_tiles(dil * nt, scores, finish)

    assert branches[-1][1] == 1
    wlen = _band_geometry(seq, tq, *branches[-1])[3]

    def dense_scores(t):
        q0, k0, bias = tile_window(len(branches) - 1, t)
        qrows = pl.ds(pl.multiple_of(q0, tq), tq)
        krows = pl.ds(pl.multiple_of(k0, 8), wlen)
        s = _nt_dot(q_ref[0, qrows, :].astype(BF16), k_ref[0, krows, :].astype(BF16)) + bias
        return (qrows, krows), s

    def dense_finish(ctx, m, p, l):
        qrows, krows = ctx
        acc = jnp.dot(p, v_ref[0, krows, :].astype(BF16), preferred_element_type=F32)
        ms = [jnp.broadcast_to(m, lanes)] + [m_s[bi, qrows, :] for bi in range(len(branches) - 1)]
        ls = [jnp.broadcast_to(l, lanes)] + [l_s[bi, qrows, :] for bi in range(len(branches) - 1)]
        accs = [acc] + [acc_s[bi, qrows, :] for bi in range(len(branches) - 1)]
        top = functools.reduce(jnp.maximum, ms)
        es = [jnp.exp2(mi - top) for mi in ms]
        num = sum(e * a for e, a in zip(es, accs))
        den = sum(e * li for e, li in zip(es, ls))
        o_ref[0, qrows, :] = (num / den).astype(BF16)

    _loop_tiles(seq // tq, dense_scores, dense_finish)


def _dilated_mixture(projf, slopes2):
    B, S, _ = projf.shape
    branches = tuple((w // (2 * d), d) for w, d in sorted(B_BRANCHES, key=lambda wd: -wd[1]))
    tq = min(S, 256)
    kern = functools.partial(_dilated_kernel, seq=S, tq=tq, branches=branches)
    nb = len(branches) - 1
    geo = [_band_geometry(S, tq, radius, dil) for radius, dil in branches]
    bias_shape = (sum(len(g[4]) for g in geo), tq, max(g[3] for g in geo))
    col = lambda part: (lambda b, h: (b, 0, part * B_HEADS + h))
    return pl.pallas_call(
        kern,
        grid=(B, B_HEADS),
        in_specs=[pl.BlockSpec(memory_space=pltpu.SMEM)] + [pl.BlockSpec((1, S, HEAD_DIM), col(p)) for p in range(3)],
        out_specs=pl.BlockSpec((1, S, HEAD_DIM), col(0)),
        out_shape=jax.ShapeDtypeStruct((B, S, B_W), BF16),
        scratch_shapes=[pltpu.VMEM((nb, S, HEAD_DIM), F32)] * 3 + [pltpu.VMEM(bias_shape, F32)],
        compiler_params=_params("parallel", "parallel"),
        name="dilated_mix",
    )(slopes2, projf, projf, projf)


OUTPROJ_ROW_GROUPS = 2


def _post_attention(project, x_ref, g1_ref, lng_ref, lnb_ref, sh2_ref, sc2_ref, rwh_ref, rwl_ref,
                    x1_ref, hp_ref, lg_ref, alpha):
    tm, d = x_ref.shape[1:]
    rg = tm // OUTPROJ_ROW_GROUPS
    nck = d // 2 // HEAD_DIM
    groups = [slice(r0, r0 + rg) for r0 in range(0, tm, rg)]
    ys = [project(rows) for rows in groups]
    for rows, y in zip(groups, ys):
        z = alpha * x_ref[0, rows, :] + g1_ref[0] * y
        mu = jnp.mean(z, axis=-1, keepdims=True)
        zc = z - mu
        var = jnp.mean(zc * zc, axis=-1, keepdims=True)
        x1 = zc * lax.rsqrt(var + LN_EPS) * lng_ref[...] + lnb_ref[...]
        x1_ref[0, rows, :] = x1
        h2 = x1 * (1.0 + sc2_ref[0]) + sh2_ref[0]
        hb = h2.astype(BF16)
        hf = hb.astype(F32)
        bits = lax.bitcast_convert_type(hf, U32)
        packed = bits[:, :d // 2] | (bits[:, d // 2:] >> 16)
        for ck in range(nck):
            hp_ref[0, pl.ds(rows.start * nck + ck, rg, stride=nck), :] = packed[:, ck * HEAD_DIM:(ck + 1) * HEAD_DIM]
        lo = (h2 - hf).astype(BF16)
        rwh = rwh_ref[...]
        lg_ref[0, rows, :] = (jnp.dot(hb, rwh, preferred_element_type=F32)
                              + jnp.dot(hb, rwl_ref[...], preferred_element_type=F32)
                              + jnp.dot(lo, rwh, preferred_element_type=F32))


def _outproj_ab_kernel(ya_ref, yb_ref, w_ref, *rest, alpha):
    def project(rows):
        return (jnp.dot(ya_ref[0, rows, :], w_ref[0:A_V, :], preferred_element_type=F32)
                + jnp.dot(yb_ref[0, rows, :], w_ref[A_V:A_V + B_W, :], preferred_element_type=F32))

    _post_attention(project, *rest, alpha)


def _outproj_c_kernel(o_ref, w_ref, *rest, alpha):
    _post_attention(lambda rows: jnp.dot(o_ref[0, rows, :], w_ref[...], preferred_element_type=F32), *rest, alpha)


def _out_projection(attn_inputs, w_bf16, x, mod, lng, lnb, rw_hi, rw_lo, alpha, mixer):
    B, S, D = x.shape
    E = rw_hi.shape[1]
    tm = min(S, 512)
    row = lambda b, i: (b, i, 0)
    const2 = lambda b, i: (0, 0)
    tail_specs = [
        pl.BlockSpec(w_bf16.shape, const2),
        pl.BlockSpec((1, tm, D), row),
        pl.BlockSpec((1, 1, D), lambda b, i: (b, 0, 2)),
        pl.BlockSpec((1, D), const2),
        pl.BlockSpec((1, D), const2),
        pl.BlockSpec((1, 1, D), lambda b, i: (b, 0, 3)),
        pl.BlockSpec((1, 1, D), lambda b, i: (b, 0, 4)),
        pl.BlockSpec((D, E), const2),
        pl.BlockSpec((D, E), const2),
    ]
    tail_args = (w_bf16, x, mod, lng.reshape(1, D), lnb.reshape(1, D), mod, mod, rw_hi, rw_lo)
    nck = D // 2 // HEAD_DIM
    out_shape = [jax.ShapeDtypeStruct((B, S, D), F32), jax.ShapeDtypeStruct((B, S * nck, HEAD_DIM), U32),
                 jax.ShapeDtypeStruct((B, S, E), F32)]
    out_specs = [pl.BlockSpec((1, tm, D), row), pl.BlockSpec((1, tm * nck, HEAD_DIM), row),
                 pl.BlockSpec((1, tm, E), row)]
    if mixer == "ab":
        ya, yb = attn_inputs
        in_specs = [pl.BlockSpec((1, tm, A_V), row), pl.BlockSpec((1, tm, B_W), row)] + tail_specs
        kern = functools.partial(_outproj_ab_kernel, alpha=alpha)
        scratch = []
        args = (ya, yb) + tail_args
    else:
        (o,) = attn_inputs
        in_specs = [pl.BlockSpec((1, tm, o.shape[2]), row)] + tail_specs
        kern = functools.partial(_outproj_c_kernel, alpha=alpha)
        scratch = []
        args = (o,) + tail_args
    return pl.pallas_call(
        kern,
        grid=(B, S // tm),
        in_specs=in_specs,
        out_specs=out_specs,
        out_shape=out_shape,
        scratch_shapes=scratch,
        compiler_params=_params("parallel", "parallel"),
        name="out_proj_" + mixer,
    )(*args)


def _excl_cumsum_lanes(x, tri):
    n = x.shape[1] // HEAD_DIM
    carry = jnp.zeros((x.shape[0], 1), F32)
    outs = []
    for c in range(n):
        xc = x[:, c * HEAD_DIM:(c + 1) * HEAD_DIM]
        outs.append(jnp.dot(xc.astype(BF16), tri, preferred_element_type=F32) + carry)
        carry = carry + jnp.sum(xc, axis=1, keepdims=True)
    return jnp.concatenate(outs, axis=1)


def _tables_kernel(lg_ref, cnt_ref, off_ref, tok_ref, gate_ref, place_ref, *, n_exp, cap):
    lg = lg_ref[0]
    S = lg.shape[1]
    ex = jnp.exp(lg - jnp.max(lg, axis=0, keepdims=True))
    aff = ex / jnp.sum(ex, axis=0, keepdims=True)
    bits = lax.bitcast_convert_type(aff, I32)

    def search(i, cur):
        cand = cur | jnp.left_shift(jnp.int32(1), 30 - i)
        cnt = jnp.sum(jnp.where(bits >= cand, 1.0, 0.0), axis=1, keepdims=True)
        return jnp.where(cnt >= cap, cand, cur)

    thr = lax.fori_loop(0, 31, search, jnp.zeros((n_exp, 1), I32))
    tri = jnp.where(lax.broadcasted_iota(I32, (HEAD_DIM, HEAD_DIM), 0)
                    < lax.broadcasted_iota(I32, (HEAD_DIM, HEAD_DIM), 1), 1.0, 0.0).astype(BF16)
    gt = bits > thr
    eq = jnp.where(bits == thr, 1.0, 0.0)
    need = cap - jnp.sum(jnp.where(gt, 1.0, 0.0), axis=1, keepdims=True)
    sel = jnp.where(gt, 1.0, jnp.where(_excl_cumsum_lanes(eq, tri) < need, eq, 0.0))
    pos = _excl_cumsum_lanes(sel, tri)

    ranks = []
    run = jnp.zeros((1, S), F32)
    for e in range(n_exp):
        ranks.append(run)
        run = run + sel[e:e + 1, :]
    cnt_ref[0] = run.astype(I32)
    first = _excl_cumsum_lanes(run, tri) + (pl.program_id(0) * (n_exp * cap)).astype(F32)
    off_ref[0] = first.astype(I32)

    def digits(x, base, n):
        out = []
        for i in reversed(range(n)):
            d = jnp.floor(x * (1.0 / base ** i))
            out.append(d)
            x = x - d * float(base ** i)
        return out

    def pieces(x, n):
        out = []
        for _ in range(n):
            p = x.astype(BF16)
            out.append(p.astype(F32))
            x = x - p.astype(F32)
        return out

    jcol = lax.broadcasted_iota(I32, (cap, S), 0).astype(F32)
    token = lax.broadcasted_iota(I32, (1, S), 1).astype(F32)
    tok_rows = digits(token, 64, 2)
    toks, gates, places = [], [], []
    for e in range(n_exp):
        mine = jnp.where(sel[e:e + 1, :] > 0.0, pos[e:e + 1, :], -1.0) == jcol
        vals = tok_rows + pieces(aff[e:e + 1, :], 3) + digits(first + ranks[e], 64, 3)
        got = _nt_dot(jnp.where(mine, 1.0, 0.0).astype(BF16), jnp.concatenate(vals, axis=0).astype(BF16))
        toks.append(got[:, 0:1] * 64.0 + got[:, 1:2])
        gates.append(got[:, 2:3] + got[:, 3:4] + got[:, 4:5])
        places.append(got[:, 5:6] * 4096.0 + got[:, 6:7] * 64.0 + got[:, 7:8])
    tok_ref[0] = jnp.concatenate(toks, axis=1).astype(I32) + pl.program_id(0) * S
    gate_ref[0] = jnp.concatenate(gates, axis=1)
    place_ref[0] = jnp.concatenate(places, axis=1).astype(I32)


def _routing_tables(logits_t, cap):
    B, E, S = logits_t.shape
    kern = functools.partial(_tables_kernel, n_exp=E, cap=cap)
    per_token = pl.BlockSpec((1, 1, S), lambda b: (b, 0, 0))
    per_slot = pl.BlockSpec((1, cap, E), lambda b: (b, 0, 0))
    cnt, off, tok, gate, place = pl.pallas_call(
        kern,
        grid=(B,),
        in_specs=[pl.BlockSpec((1, E, S), lambda b: (b, 0, 0))],
        out_specs=[per_token, per_token, per_slot, per_slot, per_slot],
        out_shape=[jax.ShapeDtypeStruct((B, 1, S), I32), jax.ShapeDtypeStruct((B, 1, S), I32),
                   jax.ShapeDtypeStruct((B, cap, E), I32), jax.ShapeDtypeStruct((B, cap, E), F32),
                   jax.ShapeDtypeStruct((B, cap, E), I32)],
        compiler_params=_params("parallel"),
        name="routing_tables",
    )(logits_t)
    expert_major = lambda a: jnp.transpose(a, (2, 0, 1)).reshape(E * B * cap)
    return cnt, off, expert_major(tok), expert_major(gate), expert_major(place)


def _token_rows(t, nck):
    start = t * nck
    return pl.ds(pl.multiple_of(start, nck) if nck > 1 else start, nck)


def _unpack_rows(xp_ref, first, xs_ref):
    rows, d = xs_ref.shape
    half = d // 2
    nck = half // HEAD_DIM
    for ck in range(nck):
        u = xp_ref[pl.ds(first * nck + ck, rows, stride=nck), :]
        lo = ck * HEAD_DIM
        xs_ref[:, lo:lo + HEAD_DIM] = lax.bitcast_convert_type(u & jnp.uint32(0xFFFF0000), F32).astype(BF16)
        xs_ref[:, half + lo:half + lo + HEAD_DIM] = lax.bitcast_convert_type(u << 16, F32).astype(BF16)


def _pack_pairs(hi, lo):
    hb = lax.bitcast_convert_type(hi.astype(BF16).astype(F32), U32)
    lb = lax.bitcast_convert_type(lo.astype(BF16).astype(F32), U32)
    return hb | (lb >> 16)


ROW_UNROLL = 8


def _ffn_up_kernel(tok_ref, tokn_ref, hp_ref, wg_ref, wu_ref, o_ref, xraw, xs_ref, sem, *, tm, mt, n_tiles, nck,
                   n_steps):
    tile = pl.program_id(0) * mt + pl.program_id(1)
    step = pl.program_id(2)
    par = tile % 2

    def request(idx_ref, slot, r):
        pltpu.make_async_copy(hp_ref.at[_token_rows(idx_ref[r], nck)],
                              xraw.at[_token_rows(slot * tm + r, nck)], sem.at[slot]).start()

    def arrived(slot):
        return pltpu.make_async_copy(hp_ref.at[pl.ds(0, tm * nck)], xraw.at[pl.ds(slot * tm * nck, tm * nck)],
                                     sem.at[slot])

    @pl.when(step == 0)
    def _():
        @pl.when(tile == 0)
        def _():
            def body(i, carry):
                for u in range(ROW_UNROLL):
                    request(tok_ref, 0, i * ROW_UNROLL + u)
                return carry

            lax.fori_loop(0, tm // ROW_UNROLL, body, 0)

        arrived(par).wait()
        _unpack_rows(xraw, par * tm, xs_ref)

    per_step = tm // n_steps
    for u in range(per_step):
        request(tokn_ref, 1 - par, step * per_step + u)

    wg = wg_ref[0, 0].astype(BF16)
    wu = wu_ref[0, 0].astype(BF16)
    rg = tm // FFN_ROW_GROUPS
    groups = [slice(r0, r0 + rg) for r0 in range(0, tm, rg)]
    gus = [(jnp.dot(xs_ref[rows, :], wg, preferred_element_type=F32),
            jnp.dot(xs_ref[rows, :], wu, preferred_element_type=F32)) for rows in groups]
    for rows, (g, u) in zip(groups, gus):
        o_ref[rows, :] = (g * jax.nn.sigmoid(g) * u).astype(BF16)

    @pl.when((tile == n_tiles - 1) & (step == n_steps - 1))
    def _():
        arrived(1 - par).wait()


def _ffn_up(hp, tok_rows, w_gate, w_up, layer, rows_per_expert):
    _, E, D, Fh = w_gate.shape
    tm = min(rows_per_expert, 2048)
    mt = rows_per_expert // tm
    n_tiles = E * mt
    tn = min(Fh, 256)
    nck = D // 2 // HEAD_DIM
    kern = functools.partial(_ffn_up_kernel, tm=tm, mt=mt, n_tiles=n_tiles, nck=nck, n_steps=Fh // tn)
    smem = functools.partial(pl.BlockSpec, memory_space=pltpu.SMEM)
    return pl.pallas_call(
        kern,
        grid=(E, mt, Fh // tn),
        in_specs=[smem((tm,), lambda e, m, n: (e * mt + m,)),
                  smem((tm,), lambda e, m, n: (jnp.minimum(e * mt + m + 1, n_tiles - 1),)),
                  pl.BlockSpec(memory_space=pl.ANY),
                  pl.BlockSpec((1, 1, D, tn), lambda e, m, n: (layer, e, 0, n)),
                  pl.BlockSpec((1, 1, D, tn), lambda e, m, n: (layer, e, 0, n))],
        out_specs=pl.BlockSpec((tm, tn), lambda e, m, n: (e * mt + m, n)),
        out_shape=jax.ShapeDtypeStruct((E * rows_per_expert, Fh), BF16),
        scratch_shapes=[pltpu.VMEM((2 * tm * nck, HEAD_DIM), U32), pltpu.VMEM((tm, D), BF16),
                        pltpu.SemaphoreType.DMA((2,))],
        compiler_params=_params("arbitrary", "arbitrary", "arbitrary"),
        name="ffn_up",
    )(tok_rows, tok_rows, hp, w_gate, w_up)


FFN_ROW_GROUPS = 2


def _ffn_down_kernel(pos_ref, h_ref, wa_ref, wb_ref, g_ref, yt_ref, ybuf, sem, *, n_steps, cps, tm, mt, n_tiles,
                     nck):
    tile = pl.program_id(0) * mt + pl.program_id(1)
    par = tile % 2
    n = pl.program_id(2)
    wa = wa_ref[0, 0].astype(BF16)
    wb = wb_ref[0, 0].astype(BF16)
    rg = tm // FFN_ROW_GROUPS
    groups = [slice(r0, r0 + rg) for r0 in range(0, tm, rg)]
    ys = [(jnp.dot(h_ref[rows, :], wa, preferred_element_type=F32),
           jnp.dot(h_ref[rows, :], wb, preferred_element_type=F32)) for rows in groups]
    for rows, (ya, yb) in zip(groups, ys):
        g = g_ref[rows, :]
        packed = _pack_pairs(ya * g, yb * g)
        for cc in range(cps):
            ybuf[pl.ds((par * tm + rows.start) * nck + n * cps + cc, rg, stride=nck), :] = (
                packed[:, cc * HEAD_DIM:(cc + 1) * HEAD_DIM])

    def scatter_done(slot):
        return pltpu.make_async_copy(ybuf.at[pl.ds(slot * tm * nck, tm * nck)], yt_ref.at[pl.ds(0, tm * nck)],
                                     sem.at[slot])

    @pl.when(n == n_steps - 1)
    def _():
        @pl.when(tile > 0)
        def _():
            scatter_done(1 - par).wait()

        def body(i, carry):
            for u in range(ROW_UNROLL):
                r = i * ROW_UNROLL + u
                pltpu.make_async_copy(ybuf.at[_token_rows(par * tm + r, nck)],
                                      yt_ref.at[_token_rows(pos_ref[r], nck)], sem.at[par]).start()
            return carry

        lax.fori_loop(0, tm // ROW_UNROLL, body, 0)

        @pl.when(tile == n_tiles - 1)
        def _():
            scatter_done(par).wait()


def _ffn_down(hid, w_down, layer, gate_col, pos_rows, rows_per_expert):
    _, E, Fh, D = w_down.shape
    tm = min(rows_per_expert, 2048)
    mt = rows_per_expert // tm
    n_tiles = E * mt
    half = D // 2
    tnh = min(half, 256)
    n_steps = half // tnh
    nck = half // HEAD_DIM
    kern = functools.partial(_ffn_down_kernel, n_steps=n_steps, cps=tnh // HEAD_DIM, tm=tm, mt=mt, n_tiles=n_tiles,
                             nck=nck)
    return pl.pallas_call(
        kern,
        grid=(E, mt, n_steps),
        in_specs=[pl.BlockSpec((tm,), lambda e, m, n: (e * mt + m,), memory_space=pltpu.SMEM),
                  pl.BlockSpec((tm, Fh), lambda e, m, n: (e * mt + m, 0)),
                  pl.BlockSpec((1, 1, Fh, tnh), lambda e, m, n: (layer, e, 0, n)),
                  pl.BlockSpec((1, 1, Fh, tnh), lambda e, m, n: (layer, e, 0, n + n_steps)),
                  pl.BlockSpec((tm, 1), lambda e, m, n: (e * mt + m, 0))],
        out_specs=pl.BlockSpec(memory_space=pl.ANY),
        out_shape=jax.ShapeDtypeStruct((E * rows_per_expert * nck, HEAD_DIM), U32),
        scratch_shapes=[pltpu.VMEM((2 * tm * nck, HEAD_DIM), U32), pltpu.SemaphoreType.DMA((2,))],
        compiler_params=_params("arbitrary", "arbitrary", "arbitrary"),
        name="ffn_down",
    )(pos_rows, hid, w_down, w_down, gate_col)


def _combine_kernel(offt_ref, x_ref, off_ref, cnt_ref, g2_ref, lng_ref, lnb_ref, yt_ref, o_ref,
                    buf, rows, acc, sem, *, tm, ch, n_total, n_tiles, n_tiles_s, alpha, nck):
    tile = pl.program_id(0) * n_tiles_s + pl.program_id(1)
    o0 = offt_ref[tile]
    o1 = offt_ref[tile + 1]
    n_chunks = jnp.maximum((o1 - o0 + ch - 1) // ch, 1)
    lo_col = off_ref[0]
    hi_col = lo_col + cnt_ref[0]
    lane = lax.broadcasted_iota(I32, (tm, ch), 1)
    acc[...] = jnp.zeros(acc.shape, F32)

    def fetch(want, slot):
        start = jnp.minimum(want, n_total - ch) * nck
        if nck > 1:
            start = pl.multiple_of(start, nck)
        return pltpu.make_async_copy(yt_ref.at[pl.ds(start, ch * nck)], buf.at[pl.ds(slot * ch * nck, ch * nck)],
                                     sem.at[slot])

    @pl.when(tile == 0)
    def _():
        fetch(o0, 0).start()

    def chunk(c, carry):
        slot = c % 2
        want = o0 + c * ch
        fetch(want, slot).wait()

        @pl.when(c + 1 < n_chunks)
        def _():
            fetch(want + ch, 1 - slot).start()

        _unpack_rows(buf, slot * ch, rows)
        row = lane + jnp.minimum(want, n_total - ch)
        own = (row >= lo_col) & (row < hi_col) & (row >= want)
        acc[...] += jnp.dot(jnp.where(own, 1.0, 0.0).astype(BF16), rows[...], preferred_element_type=F32)
        return carry

    lax.fori_loop(0, n_chunks, chunk, 0)

    @pl.when(tile + 1 < n_tiles)
    def _():
        fetch(o1, 0).start()

    z = alpha * x_ref[0] + g2_ref[0] * acc[...]
    mu = jnp.mean(z, axis=-1, keepdims=True)
    zc = z - mu
    var = jnp.mean(zc * zc, axis=-1, keepdims=True)
    o_ref[0] = zc * lax.rsqrt(var + LN_EPS) * lng_ref[...] + lnb_ref[...]


def _combine(x1, mod, lng, lnb, yt, off, cnt, alpha):
    B, S, D = x1.shape
    nck = D // 2 // HEAD_DIM
    n_total = yt.shape[0] // nck
    tm = min(S, 256)
    ch = min(n_total, 512)
    n_tiles_s = S // tm
    off_flat = off.reshape(B * S)
    offt = jnp.concatenate([off_flat[::tm], jnp.full((1,), n_total, I32)])
    kern = functools.partial(_combine_kernel, tm=tm, ch=ch, n_total=n_total, n_tiles=B * n_tiles_s,
                             n_tiles_s=n_tiles_s, alpha=alpha, nck=nck)
    grid_spec = pltpu.PrefetchScalarGridSpec(
        num_scalar_prefetch=1,
        grid=(B, n_tiles_s),
        in_specs=[
            pl.BlockSpec((1, tm, D), lambda b, i, o: (b, i, 0)),
            pl.BlockSpec((1, tm, 1), lambda b, i, o: (b, i, 0)),
            pl.BlockSpec((1, tm, 1), lambda b, i, o: (b, i, 0)),
            pl.BlockSpec((1, 1, D), lambda b, i, o: (b, 0, 5)),
            pl.BlockSpec((1, D), lambda b, i, o: (0, 0)),
            pl.BlockSpec((1, D), lambda b, i, o: (0, 0)),
            pl.BlockSpec(memory_space=pl.ANY),
        ],
        out_specs=pl.BlockSpec((1, tm, D), lambda b, i, o: (b, i, 0)),
        scratch_shapes=[pltpu.VMEM((2 * ch * nck, HEAD_DIM), U32), pltpu.VMEM((ch, D), BF16),
                        pltpu.VMEM((tm, D), F32), pltpu.SemaphoreType.DMA((2,))],
    )
    return pl.pallas_call(
        kern,
        grid_spec=grid_spec,
        out_shape=jax.ShapeDtypeStruct((B, S, D), F32),
        compiler_params=_params("arbitrary", "arbitrary"),
        name="moe_combine",
    )(offt, x1, off.reshape(B, S, 1), cnt.reshape(B, S, 1), mod, lng.reshape(1, D), lnb.reshape(1, D), yt)


def _moe_sublayer(x1, hp, logits, mod, lng, lnb, w_gate, w_up, w_down, layer, alpha):
    B, S, D = x1.shape
    E = logits.shape[-1]
    cap = EC_CAPACITY_FACTOR * S // E
    cnt, off, tok, grow, pos = _routing_tables(jnp.swapaxes(logits, 1, 2), cap)
    hid = _ffn_up(hp.reshape(-1, HEAD_DIM), tok, w_gate, w_up, layer, B * cap)
    yt = _ffn_down(hid, w_down, layer, grow.reshape(E * B * cap, 1), pos, B * cap)
    return _combine(x1, mod, lng, lnb, yt, off, cnt, alpha)


def _split_bf16(w):
    hi = w.astype(BF16)
    return hi, (w - hi.astype(F32)).astype(BF16)


def kernel(x, c, ada_w, ada_b, ln_g, ln_b, ab_w_in, ab_w_out, diff_lambda, diff_subln_g, c_w_in, c_w_out,
           c_sink, router_w, w_gate, w_up, w_down):
    B, S, D = x.shape
    depth = ada_w.shape[0]
    alpha = (2.0 * depth) ** 0.25
    qscale = HEAD_DIM ** -0.5 * LOG2E
    mod_all = _modulation(c, ada_w, ada_b)

    ab_scale = np.ones((1, AB_IN), np.float32)
    ab_scale[:, :A_QK] = qscale
    ab_scale[:, 2 * A_QK + A_V:2 * A_QK + A_V + B_W] = qscale
    c_scale = np.ones((1, C_IN), np.float32)
    c_scale[:, :C_QW] = qscale

    for l in range(depth):
        mod = mod_all[l][:, None, :]
        i = l // 2
        rw_hi, rw_lo = _split_bf16(router_w[l])
        if l % 2 == 0:
            proj, projf = _in_projection(x, mod, ab_w_in[i].astype(BF16), jnp.asarray(ab_scale), n_f32_cols=3 * B_W)
            ya = _diff_attention(proj, diff_lambda[i], diff_subln_g[i], l)
            yb = _dilated_mixture(projf, jnp.asarray(_alibi_slopes(B_HEADS) * LOG2E))
            x1, hp, logits = _out_projection((ya, yb), ab_w_out[i].astype(BF16), x, mod,
                                             ln_g[l, 0], ln_b[l, 0], rw_hi, rw_lo, alpha, "ab")
        else:
            proj = _in_projection(x, mod, c_w_in[i].astype(BF16), jnp.asarray(c_scale))
            o = _banded_attention(
                proj, n_rows=S, n_res=1, src_cols=C_IN, q_blk=0, k_blk=C_QW // C_KVW, v_blk=C_QW // C_KVW + 1,
                n_q=C_Q_HEADS, group=C_Q_HEADS // C_KV_HEADS, radius=C_RADIUS, dist_scale=1,
                slopes2=jnp.asarray(_alibi_slopes(C_Q_HEADS) * LOG2E), sink2=c_sink[i] * LOG2E)
            x1, hp, logits = _out_projection((o,), c_w_out[i].astype(BF16), x, mod, ln_g[l, 0], ln_b[l, 0],
                                             rw_hi, rw_lo, alpha, "c")
        x = _moe_sublayer(x1, hp, logits, mod, ln_g[l, 1], ln_b[l, 1], w_gate, w_up, w_down, l, alpha)
    return x
```

```python
import functools
import math

import numpy as np
import jax
import jax.numpy as jnp
from jax import lax
from jax.experimental import pallas as pl
from jax.experimental.pallas import tpu as pltpu

F32 = jnp.float32
BF16 = jnp.bfloat16
I32 = jnp.int32
U32 = jnp.uint32

HEAD_DIM = 128
A_HEADS = 4
A_VDIM = 2 * HEAD_DIM
B_HEADS = 8
B_BRANCHES = ((128, 1), (512, 4), (2048, 16))
C_Q_HEADS = 16
C_KV_HEADS = 4
C_RADIUS = 128
EC_CAPACITY_FACTOR = 2
LN_EPS = 1e-5
NEG = -1e30
LOG2E = 1.4426950408889634

A_QK = A_HEADS * 2 * HEAD_DIM
A_V = A_HEADS * A_VDIM
B_W = B_HEADS * HEAD_DIM
AB_IN = 2 * A_QK + A_V + 3 * B_W
C_QW = C_Q_HEADS * HEAD_DIM
C_KVW = C_KV_HEADS * HEAD_DIM
C_IN = C_QW + 2 * C_KVW

VMEM_LIMIT_BYTES = 56 * 1024 * 1024
MATMUL_ROW_GROUPS = 2


def _params(*sem):
    return pltpu.CompilerParams(dimension_semantics=sem, vmem_limit_bytes=VMEM_LIMIT_BYTES)


def _tile(n, preferred):
    t = min(n, preferred)
    while n % t:
        t //= 2
    return t


def _alibi_slopes(n):
    return np.array([2.0 ** (-8.0 * (i + 1) / n) for i in range(n)], dtype=np.float32)


def _nt_dot(a, b):
    return lax.dot_general(a, b, (((1,), (1,)), ((), ())), preferred_element_type=F32)


def _mod_kernel(c_ref, w_ref, b_ref, o_ref):
    c = c_ref[...]
    cs = (c * jax.nn.sigmoid(c)).astype(BF16)
    o_ref[0] = jnp.dot(cs, w_ref[0].astype(BF16), preferred_element_type=F32) + b_ref[0]


def _modulation(c, ada_w, ada_b):
    L, D, N = ada_w.shape
    B = c.shape[0]
    tn = _tile(N, 1024)
    return pl.pallas_call(
        _mod_kernel,
        grid=(L, N // tn),
        in_specs=[
            pl.BlockSpec((B, D), lambda l, j: (0, 0)),
            pl.BlockSpec((1, D, tn), lambda l, j: (l, 0, j)),
            pl.BlockSpec((1, 1, tn), lambda l, j: (l, 0, j)),
        ],
        out_specs=pl.BlockSpec((1, B, tn), lambda l, j: (l, 0, j)),
        out_shape=jax.ShapeDtypeStruct((L, B, N), F32),
        compiler_params=_params("parallel", "parallel"),
        name="adaln_mod",
    )(c, ada_w, ada_b.reshape(L, 1, N))


def _inproj_kernel(x_ref, sh_ref, sc_ref, w_ref, cs_ref, o_ref, *rest, n_bf16):
    h_ref = rest[-1]
    j = pl.program_id(2)

    @pl.when(j == 0)
    def _():
        h_ref[...] = (x_ref[0] * (1.0 + sc_ref[0]) + sh_ref[0]).astype(BF16)

    acc = jnp.dot(h_ref[...], w_ref[...], preferred_element_type=F32) * cs_ref[...]
    if n_bf16 is None:
        o_ref[0] = acc.astype(BF16)
    else:
        of_ref = rest[0]

        @pl.when(j < n_bf16)
        def _():
            o_ref[0] = acc.astype(BF16)

        @pl.when(j >= n_bf16)
        def _():
            of_ref[0] = acc


def _in_projection(x, mod, w_bf16, colscale, n_f32_cols=0):
    B, S, D = x.shape
    N = w_bf16.shape[1]
    tm = min(S, 1024)
    tn = _tile(N, 1024)
    in_specs = [
        pl.BlockSpec((1, tm, D), lambda b, i, j: (b, i, 0)),
        pl.BlockSpec((1, 1, D), lambda b, i, j: (b, 0, 0)),
        pl.BlockSpec((1, 1, D), lambda b, i, j: (b, 0, 1)),
        pl.BlockSpec((D, tn), lambda b, i, j: (0, j)),
        pl.BlockSpec((1, tn), lambda b, i, j: (0, j)),
    ]
    if n_f32_cols:
        n_bf16 = (N - n_f32_cols) // tn
        out_specs = [pl.BlockSpec((1, tm, tn), lambda b, i, j: (b, i, jnp.minimum(j, n_bf16 - 1))),
                     pl.BlockSpec((1, tm, tn), lambda b, i, j: (b, i, jnp.maximum(j - n_bf16, 0)))]
        out_shape = [jax.ShapeDtypeStruct((B, S, N - n_f32_cols), BF16),
                     jax.ShapeDtypeStruct((B, S, n_f32_cols), F32)]
    else:
        n_bf16 = None
        out_specs = pl.BlockSpec((1, tm, tn), lambda b, i, j: (b, i, j))
        out_shape = jax.ShapeDtypeStruct((B, S, N), BF16)
    return pl.pallas_call(
        functools.partial(_inproj_kernel, n_bf16=n_bf16),
        grid=(B, S // tm, N // tn),
        in_specs=in_specs,
        out_specs=out_specs,
        out_shape=out_shape,
        scratch_shapes=[pltpu.VMEM((tm, D), BF16)],
        compiler_params=_params("parallel", "parallel", "arbitrary"),
        name="in_proj",
    )(x, mod, mod, w_bf16, colscale)


N_POS_PIECES = 3


def _diff_kernel(slope_ref, q_ref, k_ref, v_ref, cx_ref, lam_ref, g_ref, o_ref,
                 kx1, kx2, qx1, qx2, bdiag, m1_ref, l1_ref, a1_ref, m2_ref, l2_ref, a2_ref, *, t, n_chunks, lam_init):
    h = pl.program_id(1)
    i = pl.program_id(2)
    slope2 = slope_ref[h]

    @pl.when(i == 0)
    def _():
        cx = cx_ref[0]
        kx1[:, :HEAD_DIM] = k_ref[0, :, :HEAD_DIM]
        kx2[:, :HEAD_DIM] = k_ref[0, :, HEAD_DIM:]
        for c in range(n_chunks):
            kx1[c * t:(c + 1) * t, HEAD_DIM:] = cx
            kx2[c * t:(c + 1) * t, HEAD_DIM:] = cx
        d = lax.broadcasted_iota(I32, (t, t), 1) - lax.broadcasted_iota(I32, (t, t), 0)
        bdiag[...] = -slope2 * jnp.abs(d).astype(F32)

    q = q_ref[0]
    ones = jnp.where(lax.broadcasted_iota(I32, (t, HEAD_DIM), 1) < N_POS_PIECES, 1.0, 0.0)
    for side, sign in enumerate((1.0, 0.0, -1.0)):
        e = (sign * ones).astype(BF16)
        qx1[side, :, :HEAD_DIM] = q[:, :HEAD_DIM]
        qx1[side, :, HEAD_DIM:] = e
        qx2[side, :, :HEAD_DIM] = q[:, HEAD_DIM:]
        qx2[side, :, HEAD_DIM:] = e
    row_term = slope2 * (i * t + lax.broadcasted_iota(I32, (t, 1), 0)).astype(F32)

    m1_ref[...] = jnp.full(m1_ref.shape, NEG, F32)
    m2_ref[...] = jnp.full(m2_ref.shape, NEG, F32)
    l1_ref[...] = jnp.zeros(l1_ref.shape, F32)
    l2_ref[...] = jnp.zeros(l2_ref.shape, F32)
    a1_ref[...] = jnp.zeros(a1_ref.shape, F32)
    a2_ref[...] = jnp.zeros(a2_ref.shape, F32)

    def key_tile(c, side, row_sign, on_diagonal):
        k0 = pl.multiple_of(c * t, t)
        vc = v_ref[0, pl.ds(k0, t), :]
        shift = row_sign * (row_term - slope2 * k0.astype(F32))

        def softmax_update(s, m_ref, l_ref):
            if on_diagonal:
                s = s + bdiag[...]
            m_old = m_ref[...]
            m_new = jnp.maximum(m_old, jnp.max(s, axis=-1, keepdims=True) + shift)
            p = jnp.exp2(s - (m_new - shift))
            alpha = jnp.exp2(m_old - m_new)
            l_ref[...] = alpha * l_ref[...] + jnp.sum(p, axis=-1, keepdims=True)
            m_ref[...] = m_new
            return p.astype(BF16), alpha

        s1 = _nt_dot(qx1[side], kx1[pl.ds(k0, t), :])
        s2 = _nt_dot(qx2[side], kx2[pl.ds(k0, t), :])
        p1, alpha1 = softmax_update(s1, m1_ref, l1_ref)
        p2, alpha2 = softmax_update(s2, m2_ref, l2_ref)
        a1_ref[...] = alpha1 * a1_ref[...] + jnp.dot(p1, vc, preferred_element_type=F32)
        a2_ref[...] = alpha2 * a2_ref[...] + jnp.dot(p2, vc, preferred_element_type=F32)

    def before(c, carry):
        key_tile(c, 0, -1.0, False)
        return carry

    def after(c, carry):
        key_tile(c, 2, 1.0, False)
        return carry

    lax.fori_loop(0, i, before, 0)
    key_tile(i, 1, 0.0, True)
    lax.fori_loop(i + 1, n_chunks, after, 0)

    lv = lam_ref[...]
    s01 = jnp.sum(lv[0:1, :] * lv[1:2, :], axis=-1, keepdims=True)
    s23 = jnp.sum(lv[2:3, :] * lv[3:4, :], axis=-1, keepdims=True)
    lam = jnp.exp(s01) - jnp.exp(s23) + lam_init
    o = a1_ref[...] / l1_ref[...] - lam * (a2_ref[...] / l2_ref[...])
    ms = jnp.mean(o * o, axis=-1, keepdims=True)
    o = o * lax.rsqrt(ms + LN_EPS) * g_ref[...] * (1.0 - lam_init)
    o_ref[0] = o.astype(BF16)


def _diff_attention(proj, lam_vecs, subln_g, layer_idx):
    B, S, _ = proj.shape
    t = min(S, 1024)
    lam_init = 0.8 - 0.6 * math.exp(-0.3 * layer_idx)
    slopes2 = jnp.asarray(_alibi_slopes(A_HEADS) * LOG2E)
    rest = slopes2[:, None] * jnp.arange(t, dtype=F32)[None, :]
    pieces = []
    for _ in range(N_POS_PIECES):
        piece = rest.astype(BF16)
        pieces.append(piece)
        rest = rest - piece.astype(F32)
    cext = jnp.zeros((A_HEADS, t, HEAD_DIM), BF16).at[:, :, :N_POS_PIECES].set(jnp.stack(pieces, axis=-1))
    nq = A_QK // A_VDIM
    kern = functools.partial(_diff_kernel, t=t, n_chunks=S // t, lam_init=lam_init)
    return pl.pallas_call(
        kern,
        grid=(B, A_HEADS, S // t),
        in_specs=[
            pl.BlockSpec(memory_space=pltpu.SMEM),
            pl.BlockSpec((1, t, A_VDIM), lambda b, h, i: (b, i, h)),
            pl.BlockSpec((1, S, A_VDIM), lambda b, h, i: (b, 0, nq + h)),
            pl.BlockSpec((1, S, A_VDIM), lambda b, h, i: (b, 0, 2 * nq + h)),
            pl.BlockSpec((1, t, HEAD_DIM), lambda b, h, i: (h, 0, 0)),
            pl.BlockSpec((4, HEAD_DIM), lambda b, h, i: (0, 0)),
            pl.BlockSpec((1, A_VDIM), lambda b, h, i: (0, 0)),
        ],
        out_specs=pl.BlockSpec((1, t, A_VDIM), lambda b, h, i: (b, i, h)),
        out_shape=jax.ShapeDtypeStruct((B, S, A_V), BF16),
        scratch_shapes=[
            pltpu.VMEM((S, 2 * HEAD_DIM), BF16), pltpu.VMEM((S, 2 * HEAD_DIM), BF16),
            pltpu.VMEM((3, t, 2 * HEAD_DIM), BF16), pltpu.VMEM((3, t, 2 * HEAD_DIM), BF16),
            pltpu.VMEM((t, t), F32),
            pltpu.VMEM((t, 1), F32), pltpu.VMEM((t, 1), F32), pltpu.VMEM((t, A_VDIM), F32),
            pltpu.VMEM((t, 1), F32), pltpu.VMEM((t, 1), F32), pltpu.VMEM((t, A_VDIM), F32),
        ],
        compiler_params=_params("parallel", "parallel", "arbitrary"),
        name="diff_attn",
    )(slopes2, proj, proj, proj, cext, lam_vecs, subln_g.reshape(1, A_VDIM))


def _banded_kernel(slope_ref, sink_ref, q_ref, kp_ref, kc_ref, kn_ref, vp_ref, vc_ref, vn_ref, o_ref, kwin, vwin, *,
                   tq, radius, n_q, group, n_rows, dist_scale, use_sink):
    t = pl.program_id(2)
    w = tq + 2 * radius
    kwin[0:radius, :] = kp_ref[0]
    kwin[radius:radius + tq, :] = kc_ref[0]
    kwin[radius + tq:w, :] = kn_ref[0]
    vwin[0:radius, :] = vp_ref[0]
    vwin[radius:radius + tq, :] = vc_ref[0]
    vwin[radius + tq:w, :] = vn_ref[0]

    ii = lax.broadcasted_iota(I32, (tq, w), 0)
    jj = lax.broadcasted_iota(I32, (tq, w), 1)
    rel = jnp.abs(jj - radius - ii)
    kpos = t * tq - radius + jj
    valid = (rel <= radius) & (kpos >= 0) & (kpos < n_rows)
    dist = rel.astype(F32) * float(dist_scale)

    def head_cols(h):
        return slice(h * HEAD_DIM, (h + 1) * HEAD_DIM)

    for hk in range(n_q // group):
        heads = range(hk * group, (hk + 1) * group)
        kh = kwin[:, head_cols(hk)]
        vh = vwin[:, head_cols(hk)]
        scores = [jnp.where(valid, _nt_dot(q_ref[0, :, head_cols(h)], kh) - slope_ref[h] * dist, NEG) for h in heads]
        soft = []
        for h, s in zip(heads, scores):
            m = jnp.max(s, axis=-1, keepdims=True)
            if use_sink:
                m = jnp.maximum(m, sink_ref[h])
            p = jnp.exp2(s - m)
            den = jnp.sum(p, axis=-1, keepdims=True)
            if use_sink:
                den = den + jnp.exp2(sink_ref[h] - m)
            soft.append((p.astype(BF16), den))
        for h, (p, den) in zip(heads, soft):
            o_ref[0, :, head_cols(h)] = (jnp.dot(p, vh, preferred_element_type=F32) / den).astype(BF16)


def _banded_attention(src, *, n_rows, n_res, src_cols, q_blk, k_blk, v_blk, n_q, group, radius,
                      dist_scale, slopes2, sink2):
    B = src.shape[0]
    qw = n_q * HEAD_DIM
    kw = (n_q // group) * HEAD_DIM
    tq = min(n_rows, 256)
    nt = n_rows // tq
    per_t = tq // radius
    last_halo = n_rows // radius - 1
    qpg = src_cols // qw
    kpg = src_cols // kw

    def q_map(b, r, t):
        return (b, t, r * qpg + q_blk)

    def cur_map(blk):
        return lambda b, r, t: (b, t, r * kpg + blk)

    def prev_map(blk):
        return lambda b, r, t: (b, jnp.maximum(t * per_t - 1, 0), r * kpg + blk)

    def next_map(blk):
        return lambda b, r, t: (b, jnp.minimum((t + 1) * per_t, last_halo), r * kpg + blk)

    out_shape = jax.ShapeDtypeStruct((B, n_rows, n_res * qw), BF16)
    out_specs = pl.BlockSpec((1, tq, qw), lambda b, r, t: (b, t, r))
    kern = functools.partial(_banded_kernel, tq=tq, radius=radius, n_q=n_q, group=group, n_rows=n_rows,
                             dist_scale=dist_scale, use_sink=sink2 is not None)
    if sink2 is None:
        sink2 = jnp.zeros((n_q,), F32)
    return pl.pallas_call(
        kern,
        grid=(B, n_res, nt),
        in_specs=[
            pl.BlockSpec(memory_space=pltpu.SMEM),
            pl.BlockSpec(memory_space=pltpu.SMEM),
            pl.BlockSpec((1, tq, qw), q_map),
            pl.BlockSpec((1, radius, kw), prev_map(k_blk)),
            pl.BlockSpec((1, tq, kw), cur_map(k_blk)),
            pl.BlockSpec((1, radius, kw), next_map(k_blk)),
            pl.BlockSpec((1, radius, kw), prev_map(v_blk)),
            pl.BlockSpec((1, tq, kw), cur_map(v_blk)),
            pl.BlockSpec((1, radius, kw), next_map(v_blk)),
        ],
        out_specs=out_specs,
        out_shape=out_shape,
        scratch_shapes=[pltpu.VMEM((tq + 2 * radius, kw), BF16), pltpu.VMEM((tq + 2 * radius, kw), BF16)],
        compiler_params=_params("parallel", "parallel", "parallel"),
        name="banded_attn",
    )(slopes2, sink2, src, src, src, src, src, src, src)


def _band_bias(tq, wlen, offset, radius, slope_dist):
    rel = jnp.abs(offset + lax.broadcasted_iota(I32, (tq, wlen), 1) - lax.broadcasted_iota(I32, (tq, wlen), 0))
    return jnp.where(rel <= radius, -slope_dist * rel.astype(F32), NEG)


def _band_softmax(s):
    m = jnp.max(s, axis=-1, keepdims=True)
    p = jnp.exp2(s - m)
    return m, p.astype(BF16), jnp.sum(p, axis=-1, keepdims=True)


TILE_UNROLL = 4


def _loop_tiles(n, scores_fn, finish_fn):
    unroll = TILE_UNROLL if n % TILE_UNROLL == 0 else 1

    def body(i, carry):
        staged = [scores_fn(i * unroll + u) for u in range(unroll)]
        soft = [_band_softmax(s) for _, s in staged]
        for (ctx, _), (m, p, l) in zip(staged, soft):
            finish_fn(ctx, m, p, l)
        return carry

    lax.fori_loop(0, n // unroll, body, 0)


def _band_geometry(seq, tq, radius, dil):
    n_rows = seq // dil
    tile = min(tq, n_rows)
    nt = n_rows // tile
    wlen = min(tile + 2 * radius, n_rows)
    offsets = (0,) if nt == 1 else (0, -radius, tile - wlen)
    return n_rows, tile, nt, wlen, offsets


def _dilated_kernel(slope_ref, q_ref, k_ref, v_ref, o_ref, acc_s, m_s, l_s, bias_s, *, seq, tq, branches):
    slope = slope_ref[pl.program_id(1)]
    lanes = (tq, HEAD_DIM)

    first_bias = []
    n_bias = 0
    for radius, dil in branches:
        _, tile, _, wlen, offsets = _band_geometry(seq, tq, radius, dil)
        first_bias.append(n_bias)
        for off in offsets:
            bias_s[n_bias, 0:tile, 0:wlen] = _band_bias(tile, wlen, off, radius, slope * dil)
            n_bias += 1

    def tile_window(bi, t):
        radius, dil = branches[bi]
        n_rows, tile, nt, wlen, _ = _band_geometry(seq, tq, radius, dil)
        q0 = t * tile
        k0 = jnp.clip(q0 - radius, 0, n_rows - wlen)
        case = 0 if nt == 1 else jnp.where(t == 0, 0, jnp.where(t == nt - 1, 2, 1))
        return q0, k0, bias_s[first_bias[bi] + case, 0:tile, 0:wlen]

    for bi, (radius, dil) in enumerate(branches[:-1]):
        _, tile, nt, wlen, _ = _band_geometry(seq, tq, radius, dil)

        def scores(it, bi=bi, dil=dil, tile=tile, nt=nt, wlen=wlen):
            r = it // nt
            q0, k0, bias = tile_window(bi, it % nt)
            qrows = pl.ds(r + dil * q0, tile, stride=dil)
            krows = pl.ds(r + dil * k0, wlen, stride=dil)
            s = _nt_dot(q_ref[0, qrows, :].astype(BF16), k_ref[0, krows, :].astype(BF16)) + bias
            return (qrows, krows), s

        def finish(ctx, m, p, l, bi=bi, tile=tile):
            qrows, krows = ctx
            acc_s[bi, qrows, :] = jnp.dot(p, v_ref[0, krows, :].astype(BF16), preferred_element_type=F32)
            m_s[bi, qrows, :] = jnp.broadcast_to(m, (tile, HEAD_DIM))
            l_s[bi, qrows, :] = jnp.broadcast_to(l, (tile, HEAD_DIM))

        _loop_tiles(dil * nt, scores, finish)

    assert branches[-1][1] == 1
    wlen = _band_geometry(seq, tq, *branches[-1])[3]

    def dense_scores(t):
        q0, k0, bias = tile_window(len(branches) - 1, t)
        qrows = pl.ds(pl.multiple_of(q0, tq), tq)
        krows = pl.ds(pl.multiple_of(k0, 8), wlen)
        s = _nt_dot(q_ref[0, qrows, :].astype(BF16), k_ref[0, krows, :].astype(BF16)) + bias
        return (qrows, krows), s

    def dense_finish(ctx, m, p, l):
        qrows, krows = ctx
        acc = jnp.dot(p, v_ref[0, krows, :].astype(BF16), preferred_element_type=F32)
        ms = [jnp.broadcast_to(m, lanes)] + [m_s[bi, qrows, :] for bi in range(len(branches) - 1)]
        ls = [jnp.broadcast_to(l, lanes)] + [l_s[bi, qrows, :] for bi in range(len(branches) - 1)]
        accs = [acc] + [acc_s[bi, qrows, :] for bi in range(len(branches) - 1)]
        top = functools.reduce(jnp.maximum, ms)
        es = [jnp.exp2(mi - top) for mi in ms]
        num = sum(e * a for e, a in zip(es, accs))
        den = sum(e * li for e, li in zip(es, ls))
        o_ref[0, qrows, :] = (num / den).astype(BF16)

    _loop_tiles(seq // tq, dense_scores, dense_finish)


def _dilated_mixture(projf, slopes2):
    B, S, _ = projf.shape
    branches = tuple((w // (2 * d), d) for w, d in sorted(B_BRANCHES, key=lambda wd: -wd[1]))
    tq = min(S, 256)
    kern = functools.partial(_dilated_kernel, seq=S, tq=tq, branches=branches)
    nb = len(branches) - 1
    geo = [_band_geometry(S, tq, radius, dil) for radius, dil in branches]
    bias_shape = (sum(len(g[4]) for g in geo), tq, max(g[3] for g in geo))
    col = lambda part: (lambda b, h: (b, 0, part * B_HEADS + h))
    return pl.pallas_call(
        kern,
        grid=(B, B_HEADS),
        in_specs=[pl.BlockSpec(memory_space=pltpu.SMEM)] + [pl.BlockSpec((1, S, HEAD_DIM), col(p)) for p in range(3)],
        out_specs=pl.BlockSpec((1, S, HEAD_DIM), col(0)),
        out_shape=jax.ShapeDtypeStruct((B, S, B_W), BF16),
        scratch_shapes=[pltpu.VMEM((nb, S, HEAD_DIM), F32)] * 3 + [pltpu.VMEM(bias_shape, F32)],
        compiler_params=_params("parallel", "parallel"),
        name="dilated_mix",
    )(slopes2, projf, projf, projf)


def _post_attention(project, x_ref, g1_ref, lng_ref, lnb_ref, sh2_ref, sc2_ref, rwh_ref, rwl_ref,
                    x1_ref, hp_ref, lg_ref, alpha):
    tm, d = x_ref.shape[1:]
    rg = tm // MATMUL_ROW_GROUPS
    nck = d // 2 // HEAD_DIM
    groups = [slice(r0, r0 + rg) for r0 in range(0, tm, rg)]
    ys = [project(rows) for rows in groups]
    for rows, y in zip(groups, ys):
        z = alpha * x_ref[0, rows, :] + g1_ref[0] * y
        mu = jnp.mean(z, axis=-1, keepdims=True)
        zc = z - mu
        var = jnp.mean(zc * zc, axis=-1, keepdims=True)
        x1 = zc * lax.rsqrt(var + LN_EPS) * lng_ref[...] + lnb_ref[...]
        x1_ref[0, rows, :] = x1
        h2 = x1 * (1.0 + sc2_ref[0]) + sh2_ref[0]
        hb = h2.astype(BF16)
        hf = hb.astype(F32)
        bits = lax.bitcast_convert_type(hf, U32)
        packed = bits[:, :d // 2] | (bits[:, d // 2:] >> 16)
        for ck in range(nck):
            hp_ref[0, pl.ds(rows.start * nck + ck, rg, stride=nck), :] = packed[:, ck * HEAD_DIM:(ck + 1) * HEAD_DIM]
        lo = (h2 - hf).astype(BF16)
        rwh = rwh_ref[...]
        lg_ref[0, rows, :] = (jnp.dot(hb, rwh, preferred_element_type=F32)
                              + jnp.dot(hb, rwl_ref[...], preferred_element_type=F32)
                              + jnp.dot(lo, rwh, preferred_element_type=F32))


def _outproj_ab_kernel(ya_ref, yb_ref, w_ref, *rest, alpha):
    def project(rows):
        return (jnp.dot(ya_ref[0, rows, :], w_ref[0:A_V, :], preferred_element_type=F32)
                + jnp.dot(yb_ref[0, rows, :], w_ref[A_V:A_V + B_W, :], preferred_element_type=F32))

    _post_attention(project, *rest, alpha)


def _outproj_c_kernel(o_ref, w_ref, *rest, alpha):
    _post_attention(lambda rows: jnp.dot(o_ref[0, rows, :], w_ref[...], preferred_element_type=F32), *rest, alpha)


def _out_projection(attn_inputs, w_bf16, x, mod, lng, lnb, rw_hi, rw_lo, alpha, mixer):
    B, S, D = x.shape
    E = rw_hi.shape[1]
    tm = min(S, 512)
    row = lambda b, i: (b, i, 0)
    const2 = lambda b, i: (0, 0)
    tail_specs = [
        pl.BlockSpec(w_bf16.shape, const2),
        pl.BlockSpec((1, tm, D), row),
        pl.BlockSpec((1, 1, D), lambda b, i: (b, 0, 2)),
        pl.BlockSpec((1, D), const2),
        pl.BlockSpec((1, D), const2),
        pl.BlockSpec((1, 1, D), lambda b, i: (b, 0, 3)),
        pl.BlockSpec((1, 1, D), lambda b, i: (b, 0, 4)),
        pl.BlockSpec((D, E), const2),
        pl.BlockSpec((D, E), const2),
    ]
    tail_args = (w_bf16, x, mod, lng.reshape(1, D), lnb.reshape(1, D), mod, mod, rw_hi, rw_lo)
    nck = D // 2 // HEAD_DIM
    out_shape = [jax.ShapeDtypeStruct((B, S, D), F32), jax.ShapeDtypeStruct((B, S * nck, HEAD_DIM), U32),
                 jax.ShapeDtypeStruct((B, S, E), F32)]
    out_specs = [pl.BlockSpec((1, tm, D), row), pl.BlockSpec((1, tm * nck, HEAD_DIM), row),
                 pl.BlockSpec((1, tm, E), row)]
    if mixer == "ab":
        ya, yb = attn_inputs
        in_specs = [pl.BlockSpec((1, tm, A_V), row), pl.BlockSpec((1, tm, B_W), row)] + tail_specs
        kern = functools.partial(_outproj_ab_kernel, alpha=alpha)
        scratch = []
        args = (ya, yb) + tail_args
    else:
        (o,) = attn_inputs
        in_specs = [pl.BlockSpec((1, tm, o.shape[2]), row)] + tail_specs
        kern = functools.partial(_outproj_c_kernel, alpha=alpha)
        scratch = []
        args = (o,) + tail_args
    return pl.pallas_call(
        kern,
        grid=(B, S // tm),
        in_specs=in_specs,
        out_specs=out_specs,
        out_shape=out_shape,
        scratch_shapes=scratch,
        compiler_params=_params("parallel", "parallel"),
        name="out_proj_" + mixer,
    )(*args)


def _excl_cumsum_lanes(x, tri):
    n = x.shape[1] // HEAD_DIM
    carry = jnp.zeros((x.shape[0], 1), F32)
    outs = []
    for c in range(n):
        xc = x[:, c * HEAD_DIM:(c + 1) * HEAD_DIM]
        outs.append(jnp.dot(xc.astype(BF16), tri, preferred_element_type=F32) + carry)
        carry = carry + jnp.sum(xc, axis=1, keepdims=True)
    return jnp.concatenate(outs, axis=1)


def _tables_kernel(lg_ref, cnt_ref, off_ref, tok_ref, gate_ref, place_ref, *, n_exp, cap):
    lg = lg_ref[0]
    S = lg.shape[1]
    ex = jnp.exp(lg - jnp.max(lg, axis=0, keepdims=True))
    aff = ex / jnp.sum(ex, axis=0, keepdims=True)
    bits = lax.bitcast_convert_type(aff, I32)

    def search(i, cur):
        cand = cur | jnp.left_shift(jnp.int32(1), 30 - i)
        cnt = jnp.sum(jnp.where(bits >= cand, 1.0, 0.0), axis=1, keepdims=True)
        return jnp.where(cnt >= cap, cand, cur)

    thr = lax.fori_loop(0, 31, search, jnp.zeros((n_exp, 1), I32))
    tri = jnp.where(lax.broadcasted_iota(I32, (HEAD_DIM, HEAD_DIM), 0)
                    < lax.broadcasted_iota(I32, (HEAD_DIM, HEAD_DIM), 1), 1.0, 0.0).astype(BF16)
    gt = bits > thr
    eq = jnp.where(bits == thr, 1.0, 0.0)
    need = cap - jnp.sum(jnp.where(gt, 1.0, 0.0), axis=1, keepdims=True)
    sel = jnp.where(gt, 1.0, jnp.where(_excl_cumsum_lanes(eq, tri) < need, eq, 0.0))
    pos = _excl_cumsum_lanes(sel, tri)

    ranks = []
    run = jnp.zeros((1, S), F32)
    for e in range(n_exp):
        ranks.append(run)
        run = run + sel[e:e + 1, :]
    cnt_ref[0] = run.astype(I32)
    first = _excl_cumsum_lanes(run, tri) + (pl.program_id(0) * (n_exp * cap)).astype(F32)
    off_ref[0] = first.astype(I32)

    def digits(x, base, n):
        out = []
        for i in reversed(range(n)):
            d = jnp.floor(x * (1.0 / base ** i))
            out.append(d)
            x = x - d * float(base ** i)
        return out

    def pieces(x, n):
        out = []
        for _ in range(n):
            p = x.astype(BF16)
            out.append(p.astype(F32))
            x = x - p.astype(F32)
        return out

    jcol = lax.broadcasted_iota(I32, (cap, S), 0).astype(F32)
    token = lax.broadcasted_iota(I32, (1, S), 1).astype(F32)
    tok_rows = digits(token, 64, 2)
    toks, gates, places = [], [], []
    for e in range(n_exp):
        mine = jnp.where(sel[e:e + 1, :] > 0.0, pos[e:e + 1, :], -1.0) == jcol
        vals = tok_rows + pieces(aff[e:e + 1, :], 3) + digits(first + ranks[e], 64, 3)
        got = _nt_dot(jnp.where(mine, 1.0, 0.0).astype(BF16), jnp.concatenate(vals, axis=0).astype(BF16))
        toks.append(got[:, 0:1] * 64.0 + got[:, 1:2])
        gates.append(got[:, 2:3] + got[:, 3:4] + got[:, 4:5])
        places.append(got[:, 5:6] * 4096.0 + got[:, 6:7] * 64.0 + got[:, 7:8])
    tok_ref[0] = jnp.concatenate(toks, axis=1).astype(I32) + pl.program_id(0) * S
    gate_ref[0] = jnp.concatenate(gates, axis=1)
    place_ref[0] = jnp.concatenate(places, axis=1).astype(I32)


def _routing_tables(logits_t, cap):
    B, E, S = logits_t.shape
    kern = functools.partial(_tables_kernel, n_exp=E, cap=cap)
    per_token = pl.BlockSpec((1, 1, S), lambda b: (b, 0, 0))
    per_slot = pl.BlockSpec((1, cap, E), lambda b: (b, 0, 0))
    cnt, off, tok, gate, place = pl.pallas_call(
        kern,
        grid=(B,),
        in_specs=[pl.BlockSpec((1, E, S), lambda b: (b, 0, 0))],
        out_specs=[per_token, per_token, per_slot, per_slot, per_slot],
        out_shape=[jax.ShapeDtypeStruct((B, 1, S), I32), jax.ShapeDtypeStruct((B, 1, S), I32),
                   jax.ShapeDtypeStruct((B, cap, E), I32), jax.ShapeDtypeStruct((B, cap, E), F32),
                   jax.ShapeDtypeStruct((B, cap, E), I32)],
        compiler_params=_params("parallel"),
        name="routing_tables",
    )(logits_t)
    expert_major = lambda a: jnp.transpose(a, (2, 0, 1)).reshape(E * B * cap)
    return cnt, off, expert_major(tok), expert_major(gate), expert_major(place)


def _token_rows(t, nck):
    start = t * nck
    return pl.ds(pl.multiple_of(start, nck) if nck > 1 else start, nck)


def _unpack_rows(xp_ref, first, xs_ref):
    rows, d = xs_ref.shape
    half = d // 2
    nck = half // HEAD_DIM
    for ck in range(nck):
        u = xp_ref[pl.ds(first * nck + ck, rows, stride=nck), :]
        lo = ck * HEAD_DIM
        xs_ref[:, lo:lo + HEAD_DIM] = lax.bitcast_convert_type(u & jnp.uint32(0xFFFF0000), F32).astype(BF16)
        xs_ref[:, half + lo:half + lo + HEAD_DIM] = lax.bitcast_convert_type(u << 16, F32).astype(BF16)


def _pack_pairs(hi, lo):
    hb = lax.bitcast_convert_type(hi.astype(BF16).astype(F32), U32)
    lb = lax.bitcast_convert_type(lo.astype(BF16).astype(F32), U32)
    return hb | (lb >> 16)


ROW_UNROLL = 8


def _ffn_up_kernel(tok_ref, tokn_ref, hp_ref, wg_ref, wu_ref, o_ref, xraw, xs_ref, sem, *, tm, mt, n_tiles, nck,
                   n_steps):
    tile = pl.program_id(0) * mt + pl.program_id(1)
    step = pl.program_id(2)
    par = tile % 2

    def request(idx_ref, slot, r):
        pltpu.make_async_copy(hp_ref.at[_token_rows(idx_ref[r], nck)],
                              xraw.at[_token_rows(slot * tm + r, nck)], sem.at[slot]).start()

    def arrived(slot):
        return pltpu.make_async_copy(hp_ref.at[pl.ds(0, tm * nck)], xraw.at[pl.ds(slot * tm * nck, tm * nck)],
                                     sem.at[slot])

    @pl.when(step == 0)
    def _():
        @pl.when(tile == 0)
        def _():
            def body(i, carry):
                for u in range(ROW_UNROLL):
                    request(tok_ref, 0, i * ROW_UNROLL + u)
                return carry

            lax.fori_loop(0, tm // ROW_UNROLL, body, 0)

        arrived(par).wait()
        _unpack_rows(xraw, par * tm, xs_ref)

    per_step = tm // n_steps
    for u in range(per_step):
        request(tokn_ref, 1 - par, step * per_step + u)

    wg = wg_ref[0, 0].astype(BF16)
    wu = wu_ref[0, 0].astype(BF16)
    rg = tm // MATMUL_ROW_GROUPS
    groups = [slice(r0, r0 + rg) for r0 in range(0, tm, rg)]
    gus = [(jnp.dot(xs_ref[rows, :], wg, preferred_element_type=F32),
            jnp.dot(xs_ref[rows, :], wu, preferred_element_type=F32)) for rows in groups]
    for rows, (g, u) in zip(groups, gus):
        o_ref[rows, :] = (g * jax.nn.sigmoid(g) * u).astype(BF16)

    @pl.when((tile == n_tiles - 1) & (step == n_steps - 1))
    def _():
        arrived(1 - par).wait()


def _ffn_up(hp, tok_rows, w_gate, w_up, layer, rows_per_expert):
    _, E, D, Fh = w_gate.shape
    tm = min(rows_per_expert, 2048)
    mt = rows_per_expert // tm
    n_tiles = E * mt
    tn = min(Fh, 256)
    nck = D // 2 // HEAD_DIM
    kern = functools.partial(_ffn_up_kernel, tm=tm, mt=mt, n_tiles=n_tiles, nck=nck, n_steps=Fh // tn)
    smem = functools.partial(pl.BlockSpec, memory_space=pltpu.SMEM)
    return pl.pallas_call(
        kern,
        grid=(E, mt, Fh // tn),
        in_specs=[smem((tm,), lambda e, m, n: (e * mt + m,)),
                  smem((tm,), lambda e, m, n: (jnp.minimum(e * mt + m + 1, n_tiles - 1),)),
                  pl.BlockSpec(memory_space=pl.ANY),
                  pl.BlockSpec((1, 1, D, tn), lambda e, m, n: (layer, e, 0, n)),
                  pl.BlockSpec((1, 1, D, tn), lambda e, m, n: (layer, e, 0, n))],
        out_specs=pl.BlockSpec((tm, tn), lambda e, m, n: (e * mt + m, n)),
        out_shape=jax.ShapeDtypeStruct((E * rows_per_expert, Fh), BF16),
        scratch_shapes=[pltpu.VMEM((2 * tm * nck, HEAD_DIM), U32), pltpu.VMEM((tm, D), BF16),
                        pltpu.SemaphoreType.DMA((2,))],
        compiler_params=_params("arbitrary", "arbitrary", "arbitrary"),
        name="ffn_up",
    )(tok_rows, tok_rows, hp, w_gate, w_up)


def _ffn_down_kernel(pos_ref, posp_ref, h_ref, wa_ref, wb_ref, g_ref, yt_ref, ybuf, sem, *, n_steps, cps, tm, mt,
                     n_tiles, nck):
    tile = pl.program_id(0) * mt + pl.program_id(1)
    par = tile % 2
    n = pl.program_id(2)

    def scatter(table_ref, slot, r):
        pltpu.make_async_copy(ybuf.at[_token_rows(slot * tm + r, nck)],
                              yt_ref.at[_token_rows(table_ref[r], nck)], sem.at[slot]).start()

    def scatter_done(slot):
        return pltpu.make_async_copy(ybuf.at[pl.ds(slot * tm * nck, tm * nck)], yt_ref.at[pl.ds(0, tm * nck)],
                                     sem.at[slot])

    @pl.when(n == 0)
    def _():
        @pl.when(tile == 0)
        def _():
            ybuf[pl.ds(tm * nck, tm * nck), :] = jnp.zeros((tm * nck, HEAD_DIM), U32)

        @pl.when(tile > 0)
        def _():
            scatter_done(par).wait()

    per_step = tm // n_steps
    for u in range(per_step):
        scatter(posp_ref, 1 - par, n * per_step + u)

    wa = wa_ref[0, 0].astype(BF16)
    wb = wb_ref[0, 0].astype(BF16)
    rg = tm // MATMUL_ROW_GROUPS
    groups = [slice(r0, r0 + rg) for r0 in range(0, tm, rg)]
    ys = [(jnp.dot(h_ref[rows, :], wa, preferred_element_type=F32),
           jnp.dot(h_ref[rows, :], wb, preferred_element_type=F32)) for rows in groups]
    for rows, (ya, yb) in zip(groups, ys):
        g = g_ref[rows, :]
        packed = _pack_pairs(ya * g, yb * g)
        for cc in range(cps):
            ybuf[pl.ds((par * tm + rows.start) * nck + n * cps + cc, rg, stride=nck), :] = (
                packed[:, cc * HEAD_DIM:(cc + 1) * HEAD_DIM])

    @pl.when((tile == n_tiles - 1) & (n == n_steps - 1))
    def _():
        scatter_done(1 - par).wait()

        def body(i, carry):
            for u in range(ROW_UNROLL):
                scatter(pos_ref, par, i * ROW_UNROLL + u)
            return carry

        lax.fori_loop(0, tm // ROW_UNROLL, body, 0)
        scatter_done(par).wait()


def _ffn_down(hid, w_down, layer, gate_col, pos_rows, rows_per_expert):
    _, E, Fh, D = w_down.shape
    tm = min(rows_per_expert, 2048)
    mt = rows_per_expert // tm
    n_tiles = E * mt
    n_rows = E * rows_per_expert
    half = D // 2
    tnh = min(half, 256)
    n_steps = half // tnh
    nck = half // HEAD_DIM
    kern = functools.partial(_ffn_down_kernel, n_steps=n_steps, cps=tnh // HEAD_DIM, tm=tm, mt=mt, n_tiles=n_tiles,
                             nck=nck)
    prev_rows = jnp.concatenate([n_rows + jnp.arange(tm, dtype=I32), pos_rows[:-tm]])
    table = functools.partial(pl.BlockSpec, (tm,), lambda e, m, n: (e * mt + m,), memory_space=pltpu.SMEM)
    return pl.pallas_call(
        kern,
        grid=(E, mt, n_steps),
        in_specs=[table(), table(),
                  pl.BlockSpec((tm, Fh), lambda e, m, n: (e * mt + m, 0)),
                  pl.BlockSpec((1, 1, Fh, tnh), lambda e, m, n: (layer, e, 0, n)),
                  pl.BlockSpec((1, 1, Fh, tnh), lambda e, m, n: (layer, e, 0, n + n_steps)),
                  pl.BlockSpec((tm, 1), lambda e, m, n: (e * mt + m, 0))],
        out_specs=pl.BlockSpec(memory_space=pl.ANY),
        out_shape=jax.ShapeDtypeStruct(((n_rows + tm) * nck, HEAD_DIM), U32),
        scratch_shapes=[pltpu.VMEM((2 * tm * nck, HEAD_DIM), U32), pltpu.SemaphoreType.DMA((2,))],
        compiler_params=_params("arbitrary", "arbitrary", "arbitrary"),
        name="ffn_down",
    )(pos_rows, prev_rows, hid, w_down, w_down, gate_col)


def _combine_kernel(offt_ref, x_ref, off_ref, cnt_ref, g2_ref, lng_ref, lnb_ref, yt_ref, o_ref,
                    buf, rows, acc, sem, *, tm, ch, n_total, n_tiles, n_tiles_s, alpha, nck):
    tile = pl.program_id(0) * n_tiles_s + pl.program_id(1)
    o0 = offt_ref[tile]
    o1 = offt_ref[tile + 1]
    n_chunks = jnp.maximum((o1 - o0 + ch - 1) // ch, 1)
    lo_col = off_ref[0]
    hi_col = lo_col + cnt_ref[0]
    lane = lax.broadcasted_iota(I32, (tm, ch), 1)
    acc[...] = jnp.zeros(acc.shape, F32)

    def fetch(want, slot):
        start = jnp.minimum(want, n_total - ch) * nck
        if nck > 1:
            start = pl.multiple_of(start, nck)
        return pltpu.make_async_copy(yt_ref.at[pl.ds(start, ch * nck)], buf.at[pl.ds(slot * ch * nck, ch * nck)],
                                     sem.at[slot])

    @pl.when(tile == 0)
    def _():
        fetch(o0, 0).start()

    def chunk(c, carry):
        slot = c % 2
        want = o0 + c * ch
        fetch(want, slot).wait()

        @pl.when(c + 1 < n_chunks)
        def _():
            fetch(want + ch, 1 - slot).start()

        _unpack_rows(buf, slot * ch, rows)
        row = lane + jnp.minimum(want, n_total - ch)
        own = (row >= lo_col) & (row < hi_col) & (row >= want)
        acc[...] += jnp.dot(jnp.where(own, 1.0, 0.0).astype(BF16), rows[...], preferred_element_type=F32)
        return carry

    lax.fori_loop(0, n_chunks, chunk, 0)

    @pl.when(tile + 1 < n_tiles)
    def _():
        fetch(o1, 0).start()

    z = alpha * x_ref[0] + g2_ref[0] * acc[...]
    mu = jnp.mean(z, axis=-1, keepdims=True)
    zc = z - mu
    var = jnp.mean(zc * zc, axis=-1, keepdims=True)
    o_ref[0] = zc * lax.rsqrt(var + LN_EPS) * lng_ref[...] + lnb_ref[...]


def _combine(x1, mod, lng, lnb, yt, n_total, off, cnt, alpha):
    B, S, D = x1.shape
    nck = D // 2 // HEAD_DIM
    tm = min(S, 256)
    ch = min(n_total, 512)
    n_tiles_s = S // tm
    off_flat = off.reshape(B * S)
    offt = jnp.concatenate([off_flat[::tm], jnp.full((1,), n_total, I32)])
    kern = functools.partial(_combine_kernel, tm=tm, ch=ch, n_total=n_total, n_tiles=B * n_tiles_s,
                             n_tiles_s=n_tiles_s, alpha=alpha, nck=nck)
    grid_spec = pltpu.PrefetchScalarGridSpec(
        num_scalar_prefetch=1,
        grid=(B, n_tiles_s),
        in_specs=[
            pl.BlockSpec((1, tm, D), lambda b, i, o: (b, i, 0)),
            pl.BlockSpec((1, tm, 1), lambda b, i, o: (b, i, 0)),
            pl.BlockSpec((1, tm, 1), lambda b, i, o: (b, i, 0)),
            pl.BlockSpec((1, 1, D), lambda b, i, o: (b, 0, 5)),
            pl.BlockSpec((1, D), lambda b, i, o: (0, 0)),
            pl.BlockSpec((1, D), lambda b, i, o: (0, 0)),
            pl.BlockSpec(memory_space=pl.ANY),
        ],
        out_specs=pl.BlockSpec((1, tm, D), lambda b, i, o: (b, i, 0)),
        scratch_shapes=[pltpu.VMEM((2 * ch * nck, HEAD_DIM), U32), pltpu.VMEM((ch, D), BF16),
                        pltpu.VMEM((tm, D), F32), pltpu.SemaphoreType.DMA((2,))],
    )
    return pl.pallas_call(
        kern,
        grid_spec=grid_spec,
        out_shape=jax.ShapeDtypeStruct((B, S, D), F32),
        compiler_params=_params("arbitrary", "arbitrary"),
        name="moe_combine",
    )(offt, x1, off.reshape(B, S, 1), cnt.reshape(B, S, 1), mod, lng.reshape(1, D), lnb.reshape(1, D), yt)


def _moe_sublayer(x1, hp, logits, mod, lng, lnb, w_gate, w_up, w_down, layer, alpha):
    B, S, D = x1.shape
    E = logits.shape[-1]
    cap = EC_CAPACITY_FACTOR * S // E
    cnt, off, tok, grow, pos = _routing_tables(jnp.swapaxes(logits, 1, 2), cap)
    hid = _ffn_up(hp.reshape(-1, HEAD_DIM), tok, w_gate, w_up, layer, B * cap)
    yt = _ffn_down(hid, w_down, layer, grow.reshape(E * B * cap, 1), pos, B * cap)
    return _combine(x1, mod, lng, lnb, yt, E * B * cap, off, cnt, alpha)


def _split_bf16(w):
    hi = w.astype(BF16)
    return hi, (w - hi.astype(F32)).astype(BF16)


def kernel(x, c, ada_w, ada_b, ln_g, ln_b, ab_w_in, ab_w_out, diff_lambda, diff_subln_g, c_w_in, c_w_out,
           c_sink, router_w, w_gate, w_up, w_down):
    B, S, D = x.shape
    depth = ada_w.shape[0]
    alpha = (2.0 * depth) ** 0.25
    qscale = HEAD_DIM ** -0.5 * LOG2E
    mod_all = _modulation(c, ada_w, ada_b)

    ab_scale = np.ones((1, AB_IN), np.float32)
    ab_scale[:, :A_QK] = qscale
    ab_scale[:, 2 * A_QK + A_V:2 * A_QK + A_V + B_W] = qscale
    c_scale = np.ones((1, C_IN), np.float32)
    c_scale[:, :C_QW] = qscale

    for l in range(depth):
        mod = mod_all[l][:, None, :]
        i = l // 2
        rw_hi, rw_lo = _split_bf16(router_w[l])
        if l % 2 == 0:
            proj, projf = _in_projection(x, mod, ab_w_in[i].astype(BF16), jnp.asarray(ab_scale), n_f32_cols=3 * B_W)
            ya = _diff_attention(proj, diff_lambda[i], diff_subln_g[i], l)
            yb = _dilated_mixture(projf, jnp.asarray(_alibi_slopes(B_HEADS) * LOG2E))
            x1, hp, logits = _out_projection((ya, yb), ab_w_out[i].astype(BF16), x, mod,
                                             ln_g[l, 0], ln_b[l, 0], rw_hi, rw_lo, alpha, "ab")
        else:
            proj = _in_projection(x, mod, c_w_in[i].astype(BF16), jnp.asarray(c_scale))
            o = _banded_attention(
                proj, n_rows=S, n_res=1, src_cols=C_IN, q_blk=0, k_blk=C_QW // C_KVW, v_blk=C_QW // C_KVW + 1,
                n_q=C_Q_HEADS, group=C_Q_HEADS // C_KV_HEADS, radius=C_RADIUS, dist_scale=1,
                slopes2=jnp.asarray(_alibi_slopes(C_Q_HEADS) * LOG2E), sink2=c_sink[i] * LOG2E)
            x1, hp, logits = _out_projection((o,), c_w_out[i].astype(BF16), x, mod, ln_g[l, 0], ln_b[l, 0],
                                             rw_hi, rw_lo, alpha, "c")
        x = _moe_sublayer(x1, hp, logits, mod, ln_g[l, 1], ln_b[l, 1], w_gate, w_up, w_down, l, alpha)
    return x
```
